```python
import math
import jax
import jax.numpy as jnp
from jax import lax
import numpy as np

D_MODEL = 2048
BATCH = 4
SEQ = 2048
DEPTH = 2

RMS_EPS = 1e-6
PLE_DIM = 256
SSD_HEAD_DIM = 64
SSD_INNER = D_MODEL
SSD_HEADS = SSD_INNER // SSD_HEAD_DIM
SSD_GROUPS = 4
SSD_STATE = 128
SSD_BC = SSD_GROUPS * SSD_STATE
SSD_CHUNK = 256
CONV_K = 4
CONV_CH = SSD_INNER + 2 * SSD_BC
POOL_WINDOWS = (2, 4, 8, 16)
POOL_WIDTH = D_MODEL
POOL_GROUPS = len(POOL_WINDOWS)
POOL_GROUP_W = POOL_WIDTH // POOL_GROUPS
ATTN_PATTERNS = ((128, 1), (512, 4), (2048, 16))
ATTN_HEAD_DIM = 128
ATTN_HEADS_PER_GROUP = 4
ATTN_HEADS = ATTN_HEADS_PER_GROUP * len(ATTN_PATTERNS)
ATTN_QKV_W = ATTN_HEADS * ATTN_HEAD_DIM
ATTN_OUT = ATTN_HEADS_PER_GROUP * ATTN_HEAD_DIM
ATTN_BLOCK = 128
N_BRANCHES = 3
IN_SIZES = (SSD_INNER, CONV_CH, SSD_HEADS,
            POOL_WIDTH, POOL_WIDTH,
            ATTN_QKV_W, ATTN_QKV_W, ATTN_QKV_W,
            ATTN_OUT,
            N_BRANCHES * D_MODEL)
N_IN = sum(IN_SIZES)
IN_SPLITS = tuple(int(s) for s in np.cumsum(IN_SIZES)[:-1])

kernel_name = "hybrid_ssd_pool_dilated_attn_gated_merge"


def _rms_norm(x, g):
    x32 = x.astype(jnp.float32)
    y = x32 * lax.rsqrt(jnp.mean(x32 * x32, axis=-1, keepdims=True) + RMS_EPS)
    return y.astype(x.dtype) * g


def _causal_depthwise_conv(u, w, bias):
    k, c = w.shape
    out = lax.conv_general_dilated(u, w[:, None, :], window_strides=(1,), padding=[(k - 1, 0)],
                                   dimension_numbers=("NWC", "WIO", "NWC"), feature_group_count=c)
    return out + bias


def _segsum(a):
    t = a.shape[-1]
    cs = jnp.cumsum(a, axis=-1)
    seg = cs[..., :, None] - cs[..., None, :]
    mask = jnp.tril(jnp.ones((t, t), dtype=bool))
    return jnp.where(mask, seg, -jnp.inf)


def _ssd_scan(xh, da, bm, cm):
    b, seq, nh, hp = xh.shape
    ng, ns = bm.shape[2], bm.shape[3]
    r = nh // ng
    q = math.gcd(SSD_CHUNK, seq)
    nc = seq // q
    x_c = xh.reshape(b, nc, q, ng, r, hp)
    a_c = da.reshape(b, nc, q, ng, r).transpose(0, 3, 4, 1, 2)
    b_c = bm.reshape(b, nc, q, ng, ns)
    c_c = cm.reshape(b, nc, q, ng, ns)
    a_cum = jnp.cumsum(a_c, axis=-1)
    l_mat = jnp.exp(_segsum(a_c))
    cb = jnp.einsum("bclgn,bcsgn->bgcls", c_c, b_c)
    y_diag = jnp.einsum("bgrcls,bcsgrp->bclgrp", cb[:, :, None] * l_mat, x_c)
    decay_states = jnp.exp(a_cum[..., -1:] - a_cum)
    states = jnp.einsum("bclgn,bgrcl,bclgrp->bcgrpn", b_c, decay_states, x_c)
    states = jnp.concatenate([jnp.zeros_like(states[:, :1]), states], axis=1)
    chunk_decay = jnp.exp(_segsum(jnp.pad(a_cum[..., -1], ((0, 0), (0, 0), (0, 0), (1, 0)))))
    states = jnp.einsum("bgrzc,bcgrpn->bzgrpn", chunk_decay, states)[:, :-1]
    y_off = jnp.einsum("bclgn,bcgrpn,bgrcl->bclgrp", c_c, states, jnp.exp(a_cum))
    return (y_diag + y_off).reshape(b, seq, nh, hp)


def _ssd_branch(z, xbc, dt, conv_w, conv_b, dt_bias, a_log, d_skip, norm_g):
    b, seq, _ = z.shape
    xbc = jax.nn.silu(_causal_depthwise_conv(xbc, conv_w, conv_b))
    xs = xbc[..., :SSD_INNER].reshape(b, seq, SSD_HEADS, SSD_HEAD_DIM)
    bm = xbc[..., SSD_INNER:SSD_INNER + SSD_BC].reshape(b, seq, SSD_GROUPS, SSD_STATE)
    cm = xbc[..., SSD_INNER + SSD_BC:].reshape(b, seq, SSD_GROUPS, SSD_STATE)
    dt32 = jax.nn.softplus(dt.astype(jnp.float32) + dt_bias.astype(jnp.float32))
    a = -jnp.exp(a_log.astype(jnp.float32))
    xs32 = xs.astype(jnp.float32)
    y = _ssd_scan(xs32 * dt32[..., None], dt32 * a, bm.astype(jnp.float32), cm.astype(jnp.float32))
    y = y + d_skip.astype(jnp.float32)[:, None] * xs32
    y = y.reshape(b, seq, SSD_INNER).astype(z.dtype)
    return _rms_norm(y * jax.nn.silu(z), norm_g)


def _pool_branch(u, z, pool_w, pool_scale):
    b, seq, _ = u.shape
    ug = u.reshape(b, seq, POOL_GROUPS, POOL_GROUP_W).astype(jnp.float32)
    cs = jnp.cumsum(ug, axis=1)
    pos = jnp.arange(seq)
    pooled = []
    for gi, w in enumerate(POOL_WINDOWS):
        c = cs[:, :, gi]
        shifted = jnp.pad(c, ((0, 0), (w, 0), (0, 0)))[:, :seq]
        count = jnp.minimum(pos + 1, w).astype(jnp.float32)[:, None]
        pooled.append((c - shifted) / count)
    pooled = (jnp.stack(pooled, axis=2) - ug).astype(u.dtype)
    mixed = jnp.einsum("blgc,gcd->blgd", pooled, pool_w).reshape(b, seq, POOL_WIDTH)
    return mixed * pool_scale * jax.nn.silu(z)


def _dilated_window_attn(q, k, v, window, dilation):
    b, seq, nh, hd = q.shape
    wk = window // dilation
    ld = seq // dilation
    bq = math.gcd(ATTN_BLOCK, ld)
    nb = ld // bq

    def strided(t):
        return t.reshape(b, ld, dilation, nh, hd).transpose(0, 2, 3, 1, 4)

    qs, ks, vs = strided(q), strided(k), strided(v)
    qb = qs.reshape(b, dilation, nh, nb, bq, hd)
    pad = ((0, 0), (0, 0), (0, 0), (wk, 0), (0, 0))
    idx = jnp.arange(nb)[:, None] * bq + jnp.arange(bq + wk)[None, :]
    kb = jnp.pad(ks, pad)[:, :, :, idx]
    vb = jnp.pad(vs, pad)[:, :, :, idx]
    s = jnp.einsum("bdhnqc,bdhnkc->bdhnqk", qb, kb).astype(jnp.float32) * (hd ** -0.5)
    qi = jnp.arange(bq)[:, None]
    kj = jnp.arange(bq + wk)[None, :]
    kpos = jnp.arange(nb)[:, None, None] * bq + kj[None] - wk
    valid = (kj >= qi)[None] & (kj <= qi + wk)[None] & (kpos >= 0)
    s = jnp.where(valid, s, -jnp.inf)
    m = jnp.max(s, axis=-1, keepdims=True)
    e = jnp.exp(s - m)
    den = jnp.sum(e, axis=-1, keepdims=True)
    o = jnp.einsum("bdhnqk,bdhnkc->bdhnqc", (e / den).astype(v.dtype), vb)
    lse = (m + jnp.log(den))[..., 0]
    o = o.reshape(b, dilation, nh, ld, hd).transpose(0, 3, 1, 2, 4).reshape(b, seq, nh, hd)
    lse = lse.reshape(b, dilation, nh, ld).transpose(0, 3, 1, 2).reshape(b, seq, nh)
    return o, lse


def _attn_branch(q, k, v, z, q_norm_g, k_norm_g):
    b, seq, _ = q.shape
    q = _rms_norm(q.reshape(b, seq, ATTN_HEADS, ATTN_HEAD_DIM), q_norm_g)
    k = _rms_norm(k.reshape(b, seq, ATTN_HEADS, ATTN_HEAD_DIM), k_norm_g)
    v = v.reshape(b, seq, ATTN_HEADS, ATTN_HEAD_DIM)
    outs, lses = [], []
    for gi, (window, dilation) in enumerate(ATTN_PATTERNS):
        hs = slice(gi * ATTN_HEADS_PER_GROUP, (gi + 1) * ATTN_HEADS_PER_GROUP)
        o, lse = _dilated_window_attn(q[:, :, hs], k[:, :, hs], v[:, :, hs], window, dilation)
        outs.append(o)
        lses.append(lse)
    wts = jax.nn.softmax(jnp.stack(lses, axis=0), axis=0)
    comb = jnp.sum(wts[..., None].astype(v.dtype) * jnp.stack(outs, axis=0), axis=0)
    return comb.reshape(b, seq, ATTN_OUT) * jax.nn.silu(z)


def setup_inputs(seed: int = 0) -> dict:
    key = jax.random.key(seed)
    ks = jax.random.split(key, 24)
    f32 = jnp.float32

    def nrm(k, shape, scale):
        return jax.random.normal(k, shape, f32) * scale

    dt0 = jnp.exp(jax.random.uniform(ks[6], (DEPTH, SSD_HEADS), f32,
                                     minval=math.log(1e-3), maxval=math.log(1e-1)))
    return {
        "x": nrm(ks[0], (BATCH, SEQ, D_MODEL), 1.0),
        "p": nrm(ks[1], (DEPTH, BATCH, SEQ, PLE_DIM), 1.0),
        "norm_g": 1.0 + nrm(ks[2], (DEPTH, D_MODEL), 0.02),
        "w_in": nrm(ks[3], (DEPTH, D_MODEL, N_IN), D_MODEL ** -0.5),
        "conv_w": nrm(ks[4], (DEPTH, CONV_K, CONV_CH), CONV_K ** -0.5),
        "conv_b": nrm(ks[5], (DEPTH, CONV_CH), 0.02),
        "dt_bias": dt0 + jnp.log(-jnp.expm1(-dt0)),
        "a_log": jnp.log(jax.random.uniform(ks[7], (DEPTH, SSD_HEADS), f32, minval=1.0, maxval=16.0)),
        "d_skip": 1.0 + nrm(ks[8], (DEPTH, SSD_HEADS), 0.1),
        "ssd_norm_g": 1.0 + nrm(ks[9], (DEPTH, SSD_INNER), 0.02),
        "w_br_ssd": nrm(ks[10], (DEPTH, SSD_INNER, D_MODEL), SSD_INNER ** -0.5),
        "pool_w": nrm(ks[11], (DEPTH, POOL_GROUPS, POOL_GROUP_W, POOL_GROUP_W), POOL_GROUP_W ** -0.5),
        "pool_scale": 1.0 + nrm(ks[12], (DEPTH, POOL_WIDTH), 0.1),
        "w_br_pool": nrm(ks[13], (DEPTH, POOL_WIDTH, D_MODEL), POOL_WIDTH ** -0.5),
        "q_norm_g": 1.0 + nrm(ks[14], (DEPTH, ATTN_HEAD_DIM), 0.02),
        "k_norm_g": 1.0 + nrm(ks[15], (DEPTH, ATTN_HEAD_DIM), 0.02),
        "w_br_attn": nrm(ks[16], (DEPTH, ATTN_OUT, D_MODEL), ATTN_OUT ** -0.5),
        "w_out": nrm(ks[17], (DEPTH, D_MODEL, D_MODEL), D_MODEL ** -0.5),
        "ple_norm_g": 1.0 + nrm(ks[18], (DEPTH, D_MODEL), 0.02),
        "w_ple_gate": nrm(ks[19], (DEPTH, D_MODEL, D_MODEL), D_MODEL ** -0.5),
        "w_ple_proj": nrm(ks[20], (DEPTH, PLE_DIM, D_MODEL), PLE_DIM ** -0.5),
    }


def reference(x, p, norm_g, w_in, conv_w, conv_b, dt_bias, a_log, d_skip, ssd_norm_g, w_br_ssd,
              pool_w, pool_scale, w_br_pool, q_norm_g, k_norm_g, w_br_attn, w_out,
              ple_norm_g, w_ple_gate, w_ple_proj):
    b, seq, _ = x.shape
    for i in range(DEPTH):
        h = _rms_norm(x, norm_g[i])
        proj = jnp.einsum("bld,dn->bln", h, w_in[i])
        (z_ssd, xbc, dt, u_pool, z_pool, q, k, v, z_attn, gate_logits) = jnp.split(proj, IN_SPLITS, axis=-1)
        y_ssd = _ssd_branch(z_ssd, xbc, dt, conv_w[i], conv_b[i], dt_bias[i], a_log[i], d_skip[i], ssd_norm_g[i])
        y_pool = _pool_branch(u_pool, z_pool, pool_w[i], pool_scale[i])
        y_attn = _attn_branch(q, k, v, z_attn, q_norm_g[i], k_norm_g[i])
        gates = jax.nn.sigmoid(gate_logits.reshape(b, seq, N_BRANCHES, D_MODEL))
        merged = (gates[:, :, 0] * (y_ssd @ w_br_ssd[i])
                  + gates[:, :, 1] * (y_pool @ w_br_pool[i])
                  + gates[:, :, 2] * (y_attn @ w_br_attn[i]))
        x = x + merged @ w_out[i]
        ple_gate = jax.nn.sigmoid(_rms_norm(x, ple_norm_g[i]) @ w_ple_gate[i])
        x = x + ple_gate * (p[i] @ w_ple_proj[i])
    return x
```

```python
import functools

import jax
import jax.numpy as jnp
from jax import lax
from jax.experimental import pallas as pl
from jax.experimental.pallas import tpu as pltpu

F32 = jnp.float32
BF16 = jnp.bfloat16

RMS_EPS = 1e-6
SSD_HEAD_DIM = 64
SSD_HEADS = 32
SSD_GROUPS = 4
SSD_STATE = 128
SSD_CHUNK = 256
CONV_K = 4
POOL_WINDOWS = (2, 4, 8, 16)
POOL_HALO = 16
ATTN_PATTERNS = ((128, 1), (512, 4), (2048, 16))
ATTN_HEAD_DIM = 128
ATTN_HEADS_PER_GROUP = 4
ATTN_BLOCK = 128

LANE = 128
VMEM_LIMIT = 56 * 1024 * 1024


def _params(*sem):
    return pltpu.CompilerParams(dimension_semantics=sem, vmem_limit_bytes=VMEM_LIMIT)


def _silu(v):
    return v * jax.nn.sigmoid(v)


def _dot(a, b):
    return jnp.dot(a, b, preferred_element_type=F32)


def _dot_nt(a, b):
    return lax.dot_general(a, b, (((1,), (1,)), ((), ())), preferred_element_type=F32)


def _dot_tn(a, b):
    return lax.dot_general(a, b, (((0,), (0,)), ((), ())), preferred_element_type=F32)


def _split3(v):
    hi = v.astype(BF16)
    r1 = v - hi.astype(F32)
    mid = r1.astype(BF16)
    lo = (r1 - mid.astype(F32)).astype(BF16)
    return hi, mid, lo


def _dot_f32_by_01(v, m01):
    hi, mid, lo = _split3(v)
    return _dot(hi, m01) + _dot(mid, m01) + _dot(lo, m01)


def _dot_01_by_f32(m01, v):
    hi, mid, lo = _split3(v)
    return _dot(m01, hi) + _dot(m01, mid) + _dot(m01, lo)


def _rms_kernel(x_ref, g_ref, o_ref):
    x = x_ref[...]
    ms = jnp.mean(x * x, axis=-1, keepdims=True)
    o_ref[...] = ((x * lax.rsqrt(ms + RMS_EPS)) * g_ref[...]).astype(o_ref.dtype)


def _rms_norm_bf16(x2d, g, bm=512):
    t, d = x2d.shape
    return pl.pallas_call(
        _rms_kernel,
        grid=(t // bm,),
        in_specs=[pl.BlockSpec((bm, d), lambda i: (i, 0)),
                  pl.BlockSpec((1, d), lambda i: (0, 0))],
        out_specs=pl.BlockSpec((bm, d), lambda i: (i, 0)),
        out_shape=jax.ShapeDtypeStruct((t, d), BF16),
        compiler_params=_params("parallel"),
        name="rms_norm",
    )(x2d, g.reshape(1, d))


def _mm_kernel(a_ref, w_ref, o_ref):
    o_ref[...] = _dot(a_ref[...], w_ref[...]).astype(o_ref.dtype)


def _matmul(a, w, bm, bn, out_dtype=F32, name="matmul"):
    m, k = a.shape
    n = w.shape[1]
    return pl.pallas_call(
        _mm_kernel,
        grid=(m // bm, n // bn),
        in_specs=[pl.BlockSpec((bm, k), lambda i, j: (i, 0)),
                  pl.BlockSpec((k, bn), lambda i, j: (0, j))],
        out_specs=pl.BlockSpec((bm, bn), lambda i, j: (i, j)),
        out_shape=jax.ShapeDtypeStruct((m, n), out_dtype),
        compiler_params=_params("parallel", "arbitrary"),
        name=name,
    )(a, w)


def _ssd_kernel(z_ref, x_ref, b_ref, c_ref, dt_ref, cw_ref, cb_ref, dtb_ref, alog_ref, dsk_ref,
                ng_ref, e_ref, o_ref, xpad, xs_scr, bb_scr, cc_scr, s_scr, y_scr):
    q = SSD_CHUNK
    inner = x_ref.shape[-1]
    gw = inner // SSD_GROUPS
    n_state = SSD_STATE
    halo = 8

    @pl.when(pl.program_id(1) == 0)
    def _():
        xpad[0:halo, :] = jnp.zeros((halo, xpad.shape[1]), F32)
        s_scr[...] = jnp.zeros(s_scr.shape, F32)

    xpad[halo:halo + q, 0:inner] = x_ref[...]
    xpad[halo:halo + q, inner:inner + b_ref.shape[-1]] = b_ref[...]
    xpad[halo:halo + q, inner + b_ref.shape[-1]:] = c_ref[...]

    n_cc = xpad.shape[1] // gw
    for j in range(n_cc):
        cs = slice(j * gw, (j + 1) * gw)
        acc = cb_ref[:, cs] + cw_ref[0:1, cs] * xpad[halo - CONV_K + 1:halo - CONV_K + 1 + q, cs]
        for k in range(1, CONV_K):
            r0 = halo - CONV_K + 1 + k
            acc = acc + cw_ref[k:k + 1, cs] * xpad[r0:r0 + q, cs]
        xc = _silu(acc)
        if j < SSD_GROUPS:
            xs_scr[:, cs] = xc
        elif j == SSD_GROUPS:
            bb_scr[...] = xc.astype(BF16)
        else:
            cc_scr[...] = xc.astype(BF16)
    xpad[0:halo, :] = xpad[q:q + halo, :]

    dtr = dt_ref[...] + dtb_ref[...]
    dtv = jnp.maximum(dtr, 0.0) + jnp.log1p(jnp.exp(-jnp.abs(dtr)))
    da = dtv * (-jnp.exp(alog_ref[...]))
    ri = lax.broadcasted_iota(jnp.int32, (q, q), 0)
    ci = lax.broadcasted_iota(jnp.int32, (q, q), 1)
    causal = ri >= ci
    tri = jnp.where(causal, 1.0, 0.0).astype(BF16)
    tri_t = jnp.where(ri <= ci, 1.0, 0.0).astype(BF16)
    acum = _dot_01_by_f32(tri, da)
    acum_row = _dot_f32_by_01(da.T, tri_t)

    first_lane_half = lax.broadcasted_iota(jnp.int32, (q, LANE), 1) < SSD_HEAD_DIM
    heads_per_group = SSD_HEADS // SSD_GROUPS
    for g in range(SSD_GROUPS):
        gs = slice(g * gw, (g + 1) * gw)
        e_g = e_ref[:, gs]
        acum_e = _dot_f32_by_01(acum, e_g)
        dt_e = _dot_f32_by_01(dtv, e_g)
        xh = xs_scr[:, gs] * dt_e
        xh_b = xh.astype(BF16)
        last_e = acum_e[q - 1:q, :]
        xdec = (xh * jnp.exp(last_e - acum_e)).astype(BF16)
        bg = bb_scr[:, g * n_state:(g + 1) * n_state]
        cg = cc_scr[:, g * n_state:(g + 1) * n_state]
        s_old = s_scr[g]
        y_off = _dot(cg, s_old.astype(BF16)) * jnp.exp(acum_e)
        s_scr[g] = jnp.exp(last_e) * s_old + _dot_tn(bg, xdec)
        cb = _dot_nt(cg, bg)
        for hh in range(heads_per_group // 2):
            xp = xh_b[:, hh * LANE:(hh + 1) * LANE]
            ys = []
            for e in range(2):
                h = g * heads_per_group + 2 * hh + e
                seg = acum[:, h:h + 1] - acum_row[h:h + 1, :]
                lmat = jnp.exp(jnp.where(causal, seg, -jnp.inf))
                ys.append(_dot((cb * lmat).astype(BF16), xp))
            y_pair = jnp.where(first_lane_half, ys[0], ys[1])
            c0 = g * gw + hh * LANE
            y_scr[:, c0:c0 + LANE] = y_pair + y_off[:, hh * LANE:(hh + 1) * LANE]

    y = y_scr[...] + dsk_ref[...] * xs_scr[...]
    gated = y * _silu(z_ref[...])
    ms = jnp.mean(gated * gated, axis=-1, keepdims=True)
    o_ref[...] = ((gated * lax.rsqrt(ms + RMS_EPS)) * ng_ref[...]).astype(o_ref.dtype)


def _ssd_branch(proj3, dt3, col, conv_w, conv_b, dt_bias, a_log, d_skip, norm_g):
    b, seq, _ = proj3.shape
    q = SSD_CHUNK
    inner = SSD_HEADS * SSD_HEAD_DIM
    bc = SSD_GROUPS * SSD_STATE
    conv_ch = inner + 2 * bc
    pad = LANE - SSD_HEADS
    expand = (jnp.arange(inner)[None, :] // SSD_HEAD_DIM == jnp.arange(LANE)[:, None]).astype(BF16)
    row = lambda v: v.reshape(1, -1)
    const = lambda shape: pl.BlockSpec(shape, lambda i, c: (0,) * len(shape))
    return pl.pallas_call(
        _ssd_kernel,
        grid=(b, seq // q),
        in_specs=[
            pl.BlockSpec((None, q, inner), lambda i, c: (i, c, col["z_ssd"] // inner)),
            pl.BlockSpec((None, q, inner), lambda i, c: (i, c, col["x"] // inner)),
            pl.BlockSpec((None, q, bc), lambda i, c: (i, c, col["B"] // bc)),
            pl.BlockSpec((None, q, bc), lambda i, c: (i, c, col["C"] // bc)),
            pl.BlockSpec((None, q, LANE), lambda i, c: (i, c, 0)),
            const((CONV_K, conv_ch)), const((1, conv_ch)), const((1, LANE)), const((1, LANE)),
            const((1, inner)), const((1, inner)), const((LANE, inner)),
        ],
        out_specs=pl.BlockSpec((None, q, inner), lambda i, c: (i, c, 0)),
        out_shape=jax.ShapeDtypeStruct((b, seq, inner), BF16),
        scratch_shapes=[
            pltpu.VMEM((q + 8, conv_ch), F32),
            pltpu.VMEM((q, inner), F32),
            pltpu.VMEM((q, bc), BF16),
            pltpu.VMEM((q, bc), BF16),
            pltpu.VMEM((SSD_GROUPS, SSD_STATE, inner // SSD_GROUPS), F32),
            pltpu.VMEM((q, inner), F32),
        ],
        compiler_params=_params("parallel", "arbitrary"),
        name="ssd_branch",
    )(proj3, proj3, proj3, proj3, dt3, conv_w, row(conv_b),
      row(jnp.pad(dt_bias, (0, pad))), row(jnp.pad(a_log, (0, pad))),
      row(jnp.repeat(d_skip, SSD_HEAD_DIM)), row(norm_g), expand)


def _pool_kernel(u_ref, halo_ref, z_ref, w_ref, sc_ref, o_ref, upad):
    tm = u_ref.shape[0]
    gw = w_ref.shape[-1]
    i = pl.program_id(1)
    upad[POOL_HALO:POOL_HALO + tm, :] = u_ref[...]

    @pl.when(i == 0)
    def _():
        upad[0:POOL_HALO, :] = jnp.zeros((POOL_HALO, upad.shape[1]), F32)

    @pl.when(i > 0)
    def _():
        upad[0:POOL_HALO, :] = halo_ref[...]

    pos = i * tm + lax.broadcasted_iota(jnp.int32, (tm, gw), 0)
    for gi, w in enumerate(POOL_WINDOWS):
        cs = slice(gi * gw, (gi + 1) * gw)
        a = upad[:, cs]
        s = a
        k = 1
        while k < w:
            s = s + pltpu.roll(s, k, axis=0)
            k *= 2
        count = jnp.minimum(pos + 1, w).astype(F32)
        pooled = s[POOL_HALO:, :] / count - a[POOL_HALO:, :]
        mixed = _dot(pooled.astype(BF16), w_ref[gi])
        o_ref[:, cs] = (mixed * sc_ref[:, cs] * _silu(z_ref[:, cs])).astype(o_ref.dtype)


def _pool_branch(proj3, col, pool_w_b, pool_scale, tm=512):
    b, seq, _ = proj3.shape
    ng, gw, _ = pool_w_b.shape
    width = ng * gw
    hb = tm // POOL_HALO
    return pl.pallas_call(
        _pool_kernel,
        grid=(b, seq // tm),
        in_specs=[
            pl.BlockSpec((None, tm, width), lambda bi, i: (bi, i, col["u_pool"] // width)),
            pl.BlockSpec((None, POOL_HALO, width),
                         lambda bi, i: (bi, jnp.maximum(i * hb - 1, 0), col["u_pool"] // width)),
            pl.BlockSpec((None, tm, width), lambda bi, i: (bi, i, col["z_pool"] // width)),
            pl.BlockSpec((ng, gw, gw), lambda bi, i: (0, 0, 0)),
            pl.BlockSpec((1, width), lambda bi, i: (0, 0)),
        ],
        out_specs=pl.BlockSpec((None, tm, width), lambda bi, i: (bi, i, 0)),
        out_shape=jax.ShapeDtypeStruct((b, seq, width), BF16),
        scratch_shapes=[pltpu.VMEM((tm + POOL_HALO, width), F32)],
        compiler_params=_params("parallel", "arbitrary"),
        name="pool_branch",
    )(proj3, proj3, proj3, pool_w_b, pool_scale.reshape(1, width))


def _attn_kernel(q0, q1, q2, k0, k1, k2, v0, v1, v2, z_ref, gq_ref, gk_ref, o_ref,
                 qp, kp, vp, op_scr, lp_scr, o0, o1, o2, l0, l1, l2):
    seq, hd = q0.shape
    blk = ATTN_BLOCK
    scale = hd ** -0.5
    qs, ks, vs = (q0, q1, q2), (k0, k1, k2), (v0, v1, v2)
    o_scrs, l_scrs = (o0, o1, o2), (l0, l1, l2)
    qi = lax.broadcasted_iota(jnp.int32, (blk, 2 * blk), 0)
    ci = lax.broadcasted_iota(jnp.int32, (blk, 2 * blk), 1)

    def qk_norm(v, g_ref):
        ms = jnp.mean(v * v, axis=-1, keepdims=True)
        return ((v * lax.rsqrt(ms + RMS_EPS)) * g_ref[...]).astype(BF16)

    kp[0:blk, :] = jnp.zeros((blk, hd), BF16)
    vp[0:blk, :] = jnp.zeros((blk, hd), BF16)
    for g, (window, dil) in enumerate(ATTN_PATTERNS):
        ld = seq // dil
        nbs = ld // blk
        for r in range(dil):
            rows = slice(None) if dil == 1 else pl.ds(r, ld, stride=dil)
            qp[r * ld:(r + 1) * ld, :] = qk_norm(qs[g][rows, :], gq_ref)
            kp[blk + r * ld:blk + (r + 1) * ld, :] = qk_norm(ks[g][rows, :], gk_ref)
            vp[blk + r * ld:blk + (r + 1) * ld, :] = vs[g][rows, :].astype(BF16)
        o_dst = o_scrs[g] if dil == 1 else op_scr
        l_dst = l_scrs[g] if dil == 1 else lp_scr

        def block(n, carry, nbs=nbs, o_dst=o_dst, l_dst=l_dst):
            off = pl.multiple_of(n * blk, blk)
            qb = qp[pl.ds(off, blk), :]
            kb = kp[pl.ds(off, 2 * blk), :]
            vb = vp[pl.ds(off, 2 * blk), :]
            s = _dot_nt(qb, kb) * scale
            lo = jnp.where(n % nbs == 0, blk, 0)
            valid = (ci >= jnp.maximum(qi, lo)) & (ci <= qi + blk)
            s = jnp.where(valid, s, -jnp.inf)
            m = jnp.max(s, axis=-1, keepdims=True)
            e = jnp.exp(s - m)
            den = jnp.sum(e, axis=-1, keepdims=True)
            o_dst[pl.ds(off, blk), :] = _dot((e / den).astype(BF16), vb)
            l_dst[pl.ds(off, blk), :] = jnp.broadcast_to(m + jnp.log(den), (blk, hd))
            return carry

        lax.fori_loop(0, seq // blk, block, 0)
        if dil > 1:
            for r in range(dil):
                o_scrs[g][pl.ds(r, ld, stride=dil), :] = op_scr[r * ld:(r + 1) * ld, :]
                l_scrs[g][pl.ds(r, ld, stride=dil), :] = lp_scr[r * ld:(r + 1) * ld, :]

    la, lb, lc = l0[...], l1[...], l2[...]
    mx = jnp.maximum(jnp.maximum(la, lb), lc)
    wa, wb, wc = jnp.exp(la - mx), jnp.exp(lb - mx), jnp.exp(lc - mx)
    tot = wa + wb + wc
    comb = (wa / tot) * o0[...] + (wb / tot) * o1[...] + (wc / tot) * o2[...]
    o_ref[...] = (comb * _silu(z_ref[...])).astype(o_ref.dtype)


def _attn_branch(proj3, col, q_norm_g, k_norm_g):
    b, seq, _ = proj3.shape
    hd = ATTN_HEAD_DIM
    hpg = ATTN_HEADS_PER_GROUP
    ngroups = len(ATTN_PATTERNS)

    def head_spec(name, g):
        base = col[name] // hd + g * hpg
        return pl.BlockSpec((None, seq, hd), lambda bi, j: (bi, 0, base + j))

    in_specs = [head_spec(nm, g) for nm in ("q", "k", "v") for g in range(ngroups)]
    zbase = col["z_attn"] // hd
    in_specs += [pl.BlockSpec((None, seq, hd), lambda bi, j: (bi, 0, zbase + j)),
                 pl.BlockSpec((1, hd), lambda bi, j: (0, 0)),
                 pl.BlockSpec((1, hd), lambda bi, j: (0, 0))]
    scratch = [pltpu.VMEM((seq, hd), BF16),
               pltpu.VMEM((seq + ATTN_BLOCK, hd), BF16),
               pltpu.VMEM((seq + ATTN_BLOCK, hd), BF16),
               pltpu.VMEM((seq, hd), F32),
               pltpu.VMEM((seq, hd), F32)] + [pltpu.VMEM((seq, hd), F32)] * (2 * ngroups)
    return pl.pallas_call(
        _attn_kernel,
        grid=(b, hpg),
        in_specs=in_specs,
        out_specs=pl.BlockSpec((None, seq, hd), lambda bi, j: (bi, 0, j)),
        out_shape=jax.ShapeDtypeStruct((b, seq, hpg * hd), BF16),
        scratch_shapes=scratch,
        compiler_params=_params("parallel", "parallel"),
        name="attn_branch",
    )(*([proj3] * 10), q_norm_g.reshape(1, hd), k_norm_g.reshape(1, hd))


def _merge_kernel(ys_ref, yp_ref, ya_ref, w1_ref, w2_ref, w3_ref, g0_ref, g1_ref, g2_ref, o_ref):
    m = jax.nn.sigmoid(g0_ref[...]) * _dot(ys_ref[...], w1_ref[...])
    m = m + jax.nn.sigmoid(g1_ref[...]) * _dot(yp_ref[...], w2_ref[...])
    m = m + jax.nn.sigmoid(g2_ref[...]) * _dot(ya_ref[...], w3_ref[...])
    o_ref[...] = m.astype(o_ref.dtype)


def _merge(y_ssd, y_pool, y_attn, w1, w2, w3, proj2, col, bm=512, bn=1024):
    t, d = y_ssd.shape
    n = w1.shape[1]
    gate_blocks = n // bn
    gbase = col["gates"] // bn

    def gate_spec(i):
        return pl.BlockSpec((bm, bn), lambda mi, ni: (mi, gbase + i * gate_blocks + ni))

    return pl.pallas_call(
        _merge_kernel,
        grid=(t // bm, n // bn),
        in_specs=[pl.BlockSpec((bm, d), lambda mi, ni: (mi, 0)),
                  pl.BlockSpec((bm, y_pool.shape[1]), lambda mi, ni: (mi, 0)),
                  pl.BlockSpec((bm, y_attn.shape[1]), lambda mi, ni: (mi, 0)),
                  pl.BlockSpec((w1.shape[0], bn), lambda mi, ni: (0, ni)),
                  pl.BlockSpec((w2.shape[0], bn), lambda mi, ni: (0, ni)),
                  pl.BlockSpec((w3.shape[0], bn), lambda mi, ni: (0, ni)),
                  gate_spec(0), gate_spec(1), gate_spec(2)],
        out_specs=pl.BlockSpec((bm, bn), lambda mi, ni: (mi, ni)),
        out_shape=jax.ShapeDtypeStruct((t, n), BF16),
        compiler_params=_params("parallel", "arbitrary"),
        name="gated_merge",
    )(y_ssd, y_pool, y_attn, w1, w2, w3, proj2, proj2, proj2)


def _out_kernel(a_ref, w_ref, r_ref, o_ref):
    o_ref[...] = r_ref[...] + _dot(a_ref[...], w_ref[...])


def _out_proj(merged, w_out, x2d, bm=1024, bn=1024):
    t, k = merged.shape
    n = w_out.shape[1]
    return pl.pallas_call(
        _out_kernel,
        grid=(t // bm, n // bn),
        in_specs=[pl.BlockSpec((bm, k), lambda i, j: (i, 0)),
                  pl.BlockSpec((k, bn), lambda i, j: (0, j)),
                  pl.BlockSpec((bm, bn), lambda i, j: (i, j))],
        out_specs=pl.BlockSpec((bm, bn), lambda i, j: (i, j)),
        out_shape=jax.ShapeDtypeStruct((t, n), F32),
        compiler_params=_params("parallel", "arbitrary"),
        name="out_proj",
    )(merged, w_out, x2d)


def _ple_kernel(h_ref, wg_ref, p_ref, wp_ref, x_ref, o_ref):
    gate = jax.nn.sigmoid(_dot(h_ref[...], wg_ref[...]))
    emb = _dot(p_ref[...].astype(BF16), wp_ref[...])
    o_ref[...] = x_ref[...] + gate * emb


def _ple_update(h, w_gate, p2d, w_proj, x2d, bm=1024, bn=1024):
    t, k = h.shape
    n = w_gate.shape[1]
    kp = p2d.shape[1]
    return pl.pallas_call(
        _ple_kernel,
        grid=(t // bm, n // bn),
        in_specs=[pl.BlockSpec((bm, k), lambda i, j: (i, 0)),
                  pl.BlockSpec((k, bn), lambda i, j: (0, j)),
                  pl.BlockSpec((bm, kp), lambda i, j: (i, 0)),
                  pl.BlockSpec((kp, bn), lambda i, j: (0, j)),
                  pl.BlockSpec((bm, bn), lambda i, j: (i, j))],
        out_specs=pl.BlockSpec((bm, bn), lambda i, j: (i, j)),
        out_shape=jax.ShapeDtypeStruct((t, n), F32),
        compiler_params=_params("parallel", "arbitrary"),
        name="ple_update",
    )(h, w_gate, p2d, w_proj, x2d)


def _in_proj_layout(d_model):
    inner = SSD_HEADS * SSD_HEAD_DIM
    bc = SSD_GROUPS * SSD_STATE
    attn_w = len(ATTN_PATTERNS) * ATTN_HEADS_PER_GROUP * ATTN_HEAD_DIM
    attn_out = ATTN_HEADS_PER_GROUP * ATTN_HEAD_DIM
    sizes = [("z_ssd", inner), ("x", inner), ("B", bc), ("C", bc), ("dt", SSD_HEADS),
             ("u_pool", d_model), ("z_pool", d_model), ("q", attn_w), ("k", attn_w), ("v", attn_w),
             ("z_attn", attn_out), ("gates", 3 * d_model)]
    src, start = {}, 0
    for name, size in sizes:
        src[name] = (start, start + size)
        start += size
    order = ["gates", "z_ssd", "x", "u_pool", "z_pool", "B", "C", "q", "k", "v", "z_attn"]
    col, pos = {}, 0
    for name in order:
        col[name] = pos
        pos += src[name][1] - src[name][0]
    return col, [src[name] for name in order], src["dt"]


def kernel(x, p, norm_g, w_in, conv_w, conv_b, dt_bias, a_log, d_skip, ssd_norm_g, w_br_ssd, pool_w,
           pool_scale, w_br_pool, q_norm_g, k_norm_g, w_br_attn, w_out, ple_norm_g, w_ple_gate,
           w_ple_proj):
    b, seq, d = x.shape
    depth = w_in.shape[0]
    t = b * seq
    col, main_slices, dt_slice = _in_proj_layout(d)
    x2 = x.reshape(t, d)
    for i in range(depth):
        w = w_in[i]
        w_main = jnp.concatenate([w[:, s:e] for s, e in main_slices], axis=1).astype(BF16)
        w_dt = jnp.pad(w[:, dt_slice[0]:dt_slice[1]], ((0, 0), (0, LANE - SSD_HEADS))).astype(BF16)

        h = _rms_norm_bf16(x2, norm_g[i])
        proj2 = _matmul(h, w_main, 1024, 1024, name="in_proj")
        dt2 = _matmul(h, w_dt, 2048, LANE, name="dt_proj")
        proj3 = proj2.reshape(b, seq, -1)

        y_ssd = _ssd_branch(proj3, dt2.reshape(b, seq, LANE), col, conv_w[i], conv_b[i], dt_bias[i],
                            a_log[i], d_skip[i], ssd_norm_g[i])
        y_pool = _pool_branch(proj3, col, pool_w[i].astype(BF16), pool_scale[i])
        y_attn = _attn_branch(proj3, col, q_norm_g[i], k_norm_g[i])

        merged = _merge(y_ssd.reshape(t, -1), y_pool.reshape(t, -1), y_attn.reshape(t, -1),
                        w_br_ssd[i].astype(BF16), w_br_pool[i].astype(BF16),
                        w_br_attn[i].astype(BF16), proj2, col)
        x2 = _out_proj(merged, w_out[i].astype(BF16), x2)
        h2 = _rms_norm_bf16(x2, ple_norm_g[i])
        x2 = _ple_update(h2, w_ple_gate[i].astype(BF16), p[i].reshape(t, -1),
                         w_ple_proj[i].astype(BF16), x2)
    return x2.reshape(b, seq, d)
```

```python
import functools

import jax
import jax.numpy as jnp
from jax import lax
from jax.experimental import pallas as pl
from jax.experimental.pallas import tpu as pltpu

F32 = jnp.float32
BF16 = jnp.bfloat16

RMS_EPS = 1e-6
SSD_HEAD_DIM = 64
SSD_HEADS = 32
SSD_GROUPS = 4
SSD_STATE = 128
SSD_CHUNK = 256
CONV_K = 4
POOL_WINDOWS = (2, 4, 8, 16)
POOL_HALO = 16
ATTN_PATTERNS = ((128, 1), (512, 4), (2048, 16))
ATTN_HEAD_DIM = 128
ATTN_HEADS_PER_GROUP = 4
ATTN_BLOCK = 128

LANE = 128
VMEM_LIMIT = 56 * 1024 * 1024
STAGE_ROWS = 256


def _params(*sem):
    return pltpu.CompilerParams(dimension_semantics=sem, vmem_limit_bytes=VMEM_LIMIT)


def _silu(v):
    return v * jax.nn.sigmoid(v)


def _dot(a, b):
    return jnp.dot(a, b, preferred_element_type=F32)


def _dot_nt(a, b):
    return lax.dot_general(a, b, (((1,), (1,)), ((), ())), preferred_element_type=F32)


def _dot_tn(a, b):
    return lax.dot_general(a, b, (((0,), (0,)), ((), ())), preferred_element_type=F32)


def _split3(v):
    hi = v.astype(BF16)
    r1 = v - hi.astype(F32)
    mid = r1.astype(BF16)
    lo = (r1 - mid.astype(F32)).astype(BF16)
    return hi, mid, lo


def _dot_f32_by_01(v, m01):
    hi, mid, lo = _split3(v)
    return _dot(hi, m01) + _dot(mid, m01) + _dot(lo, m01)


def _dot_01_by_f32(m01, v):
    hi, mid, lo = _split3(v)
    return _dot(m01, hi) + _dot(m01, mid) + _dot(m01, lo)


def _stage_weight(w_ref, wn_ref, dst, shift):
    k, bn = dst.shape

    def body(c, carry):
        rows = pl.ds(pl.multiple_of(c * STAGE_ROWS, STAGE_ROWS), STAGE_ROWS)
        if shift == 0:
            dst[rows, :] = w_ref[rows, :].astype(BF16)
        else:
            wide = jnp.concatenate([w_ref[rows, :], wn_ref[rows, :]], axis=1)
            dst[rows, :] = wide[:, shift:shift + bn].astype(BF16)
        return carry

    lax.fori_loop(0, k // STAGE_ROWS, body, 0)


def _weight_specs(layer, k, bn, col0):
    shift = col0 % LANE
    base = col0 - shift
    assert base % bn == 0
    specs = [pl.BlockSpec((None, k, bn), lambda n, m: (layer, 0, base // bn + n))]
    if shift:
        specs.append(pl.BlockSpec((None, k, LANE),
                                  lambda n, m: (layer, 0, (base + (n + 1) * bn) // LANE)))
    return shift, specs


def _rms_kernel(x_ref, g_ref, o_ref):
    x = x_ref[...]
    ms = jnp.mean(x * x, axis=-1, keepdims=True)
    o_ref[...] = ((x * lax.rsqrt(ms + RMS_EPS)) * g_ref[...]).astype(o_ref.dtype)


def _rms_norm_bf16(x2d, g, bm=512):
    t, d = x2d.shape
    return pl.pallas_call(
        _rms_kernel,
        grid=(t // bm,),
        in_specs=[pl.BlockSpec((bm, d), lambda i: (i, 0)),
                  pl.BlockSpec((1, d), lambda i: (0, 0))],
        out_specs=pl.BlockSpec((bm, d), lambda i: (i, 0)),
        out_shape=jax.ShapeDtypeStruct((t, d), BF16),
        compiler_params=_params("parallel"),
        name="rms_norm",
    )(x2d, g.reshape(1, d))


def _proj_kernel(*refs, shift):
    if shift:
        a_ref, w_ref, wn_ref, o_ref, wb = refs
    else:
        a_ref, w_ref, o_ref, wb = refs
        wn_ref = None

    @pl.when(pl.program_id(1) == 0)
    def _():
        _stage_weight(w_ref, wn_ref, wb, shift)

    o_ref[...] = _dot(a_ref[...], wb[...]).astype(o_ref.dtype)


def _in_proj(h, w_in, layer, col0, width, bm, bn, name):
    t, k = h.shape
    shift, wspecs = _weight_specs(layer, k, bn, col0)
    return pl.pallas_call(
        functools.partial(_proj_kernel, shift=shift),
        grid=(width // bn, t // bm),
        in_specs=[pl.BlockSpec((bm, k), lambda n, m: (m, 0))] + wspecs,
        out_specs=pl.BlockSpec((bm, bn), lambda n, m: (m, n)),
        out_shape=jax.ShapeDtypeStruct((t, width), F32),
        scratch_shapes=[pltpu.VMEM((k, bn), BF16)],
        compiler_params=_params("arbitrary", "arbitrary"),
        name=name,
    )(h, *([w_in] * len(wspecs)))


def _qkv_kernel(a_ref, w_ref, wn_ref, gq_ref, gk_ref, o_ref, wb, acc_scr, *, shift, seq):
    n = pl.program_id(0)
    m = pl.program_id(1)
    bm = a_ref.shape[0]
    hd = ATTN_HEAD_DIM
    ngroups = len(ATTN_PATTERNS)
    half = m % (seq // bm)

    @pl.when(m == 0)
    def _():
        _stage_weight(w_ref, wn_ref, wb, shift)

    acc = _dot(a_ref[...], wb[...])
    for hh in range(ATTN_HEADS_PER_GROUP):
        acc_scr[hh] = acc[:, hh * hd:(hh + 1) * hd]

    for kind in range(3):
        for g, (_, dil) in enumerate(ATTN_PATTERNS):
            @pl.when(n == kind * ngroups + g)
            def _(kind=kind, dil=dil):
                cnt = bm // dil
                ld = seq // dil
                for hh in range(ATTN_HEADS_PER_GROUP):
                    cs = slice(hh * hd, (hh + 1) * hd)
                    v = acc_scr[hh]
                    if kind < 2:
                        g_ref = gq_ref if kind == 0 else gk_ref
                        ms = jnp.mean(v * v, axis=-1, keepdims=True)
                        v = (v * lax.rsqrt(ms + RMS_EPS)) * g_ref[...]
                    if dil == 1:
                        o_ref[pl.ds(pl.multiple_of(half * bm, bm), bm), cs] = v.astype(BF16)
                    else:
                        if kind < 2:
                            acc_scr[hh] = v
                        for r in range(dil):
                            rows = acc_scr[hh, pl.ds(r, cnt, stride=dil), :]
                            dst = r * ld + pl.multiple_of(half * cnt, cnt)
                            o_ref[pl.ds(dst, cnt), cs] = rows.astype(BF16)


def _qkv_proj(h, w_in, layer, col0, b, seq, q_norm_g, k_norm_g, bm=1024):
    t, k = h.shape
    hd = ATTN_HEAD_DIM
    bn = ATTN_HEADS_PER_GROUP * hd
    n_tiles = 3 * len(ATTN_PATTERNS)
    tiles_per_seq = seq // bm
    shift, wspecs = _weight_specs(layer, k, bn, col0)
    assert shift
    return pl.pallas_call(
        functools.partial(_qkv_kernel, shift=shift, seq=seq),
        grid=(n_tiles, t // bm),
        in_specs=[pl.BlockSpec((bm, k), lambda n, m: (m, 0))] + wspecs
                 + [pl.BlockSpec((1, hd), lambda n, m: (0, 0))] * 2,
        out_specs=pl.BlockSpec((None, seq, bn), lambda n, m: (m // tiles_per_seq, 0, n)),
        out_shape=jax.ShapeDtypeStruct((b, seq, n_tiles * bn), BF16),
        scratch_shapes=[pltpu.VMEM((k, bn), BF16),
                        pltpu.VMEM((ATTN_HEADS_PER_GROUP, bm, hd), F32)],
        compiler_params=_params("arbitrary", "arbitrary"),
        name="qkv_proj",
    )(h, w_in, w_in, q_norm_g.reshape(1, hd), k_norm_g.reshape(1, hd))


def _ssd_kernel(h_ref, wdt_ref, z_ref, x_ref, b_ref, c_ref, cw_ref, cb_ref, dtb_ref, alog_ref,
                dsk_ref, ng_ref, e_ref, o_ref, xpad, xs_scr, bb_scr, cc_scr, s_scr, y_scr):
    q = SSD_CHUNK
    inner = x_ref.shape[-1]
    gw = inner // SSD_GROUPS
    n_state = SSD_STATE
    halo = 8

    @pl.when(pl.program_id(1) == 0)
    def _():
        xpad[0:halo, :] = jnp.zeros((halo, xpad.shape[1]), F32)
        s_scr[...] = jnp.zeros(s_scr.shape, F32)

    xpad[halo:halo + q, 0:inner] = x_ref[...]
    xpad[halo:halo + q, inner:inner + b_ref.shape[-1]] = b_ref[...]
    xpad[halo:halo + q, inner + b_ref.shape[-1]:] = c_ref[...]

    n_cc = xpad.shape[1] // gw
    for j in range(n_cc):
        cs = slice(j * gw, (j + 1) * gw)
        acc = cb_ref[:, cs] + cw_ref[0:1, cs] * xpad[halo - CONV_K + 1:halo - CONV_K + 1 + q, cs]
        for k in range(1, CONV_K):
            r0 = halo - CONV_K + 1 + k
            acc = acc + cw_ref[k:k + 1, cs] * xpad[r0:r0 + q, cs]
        xc = _silu(acc)
        if j < SSD_GROUPS:
            xs_scr[:, cs] = xc
        elif j == SSD_GROUPS:
            bb_scr[...] = xc.astype(BF16)
        else:
            cc_scr[...] = xc.astype(BF16)
    xpad[0:halo, :] = xpad[q:q + halo, :]

    dt_lanes = lax.broadcasted_iota(jnp.int32, (q, LANE), 1) < SSD_HEADS
    dt_raw = jnp.where(dt_lanes, _dot(h_ref[...], wdt_ref[...].astype(BF16)), 0.0)
    dtr = dt_raw + dtb_ref[...]
    dtv = jnp.maximum(dtr, 0.0) + jnp.log1p(jnp.exp(-jnp.abs(dtr)))
    da = dtv * (-jnp.exp(alog_ref[...]))
    ri = lax.broadcasted_iota(jnp.int32, (q, q), 0)
    ci = lax.broadcasted_iota(jnp.int32, (q, q), 1)
    causal = ri >= ci
    tri = jnp.where(causal, 1.0, 0.0).astype(BF16)
    tri_t = jnp.where(ri <= ci, 1.0, 0.0).astype(BF16)
    acum = _dot_01_by_f32(tri, da)
    acum_row = _dot_f32_by_01(da.T, tri_t)

    first_lane_half = lax.broadcasted_iota(jnp.int32, (q, LANE), 1) < SSD_HEAD_DIM
    heads_per_group = SSD_HEADS // SSD_GROUPS
    for g in range(SSD_GROUPS):
        gs = slice(g * gw, (g + 1) * gw)
        e_g = e_ref[:, gs]
        acum_e = _dot_f32_by_01(acum, e_g)
        dt_e = _dot_f32_by_01(dtv, e_g)
        xh = xs_scr[:, gs] * dt_e
        xh_b = xh.astype(BF16)
        last_e = acum_e[q - 1:q, :]
        xdec = (xh * jnp.exp(last_e - acum_e)).astype(BF16)
        bg = bb_scr[:, g * n_state:(g + 1) * n_state]
        cg = cc_scr[:, g * n_state:(g + 1) * n_state]
        s_old = s_scr[g]
        y_off = _dot(cg, s_old.astype(BF16)) * jnp.exp(acum_e)
        s_scr[g] = jnp.exp(last_e) * s_old + _dot_tn(bg, xdec)
        cb = _dot_nt(cg, bg)
        for hh in range(heads_per_group // 2):
            xp = xh_b[:, hh * LANE:(hh + 1) * LANE]
            ys = []
            for e in range(2):
                h = g * heads_per_group + 2 * hh + e
                seg = acum[:, h:h + 1] - acum_row[h:h + 1, :]
                lmat = jnp.exp(jnp.where(causal, seg, -jnp.inf))
                ys.append(_dot((cb * lmat).astype(BF16), xp))
            y_pair = jnp.where(first_lane_half, ys[0], ys[1])
            c0 = g * gw + hh * LANE
            y_scr[:, c0:c0 + LANE] = y_pair + y_off[:, hh * LANE:(hh + 1) * LANE]

    y = y_scr[...] + dsk_ref[...] * xs_scr[...]
    gated = y * _silu(z_ref[...])
    ms = jnp.mean(gated * gated, axis=-1, keepdims=True)
    o_ref[...] = ((gated * lax.rsqrt(ms + RMS_EPS)) * ng_ref[...]).astype(o_ref.dtype)


def _ssd_branch(h3, w_in, layer, dt_col, proj3, conv_w, conv_b, dt_bias, a_log, d_skip, norm_g):
    b, seq, d = h3.shape
    q = SSD_CHUNK
    inner = SSD_HEADS * SSD_HEAD_DIM
    bc = SSD_GROUPS * SSD_STATE
    conv_ch = inner + 2 * bc
    pad = LANE - SSD_HEADS
    assert dt_col % LANE == 0
    expand = (jnp.arange(inner)[None, :] // SSD_HEAD_DIM == jnp.arange(LANE)[:, None]).astype(BF16)
    row = lambda v: v.reshape(1, -1)
    const = lambda shape: pl.BlockSpec(shape, lambda i, c: (0,) * len(shape))
    return pl.pallas_call(
        _ssd_kernel,
        grid=(b, seq // q),
        in_specs=[
            pl.BlockSpec((None, q, d), lambda i, c: (i, c, 0)),
            pl.BlockSpec((None, d, LANE), lambda i, c: (layer, 0, dt_col // LANE)),
            pl.BlockSpec((None, q, inner), lambda i, c: (i, c, 0)),
            pl.BlockSpec((None, q, inner), lambda i, c: (i, c, 1)),
            pl.BlockSpec((None, q, bc), lambda i, c: (i, c, 2 * inner // bc)),
            pl.BlockSpec((None, q, bc), lambda i, c: (i, c, 2 * inner // bc + 1)),
            const((CONV_K, conv_ch)), const((1, conv_ch)), const((1, LANE)), const((1, LANE)),
            const((1, inner)), const((1, inner)), const((LANE, inner)),
        ],
        out_specs=pl.BlockSpec((None, q, inner), lambda i, c: (i, c, 0)),
        out_shape=jax.ShapeDtypeStruct((b, seq, inner), BF16),
        scratch_shapes=[
            pltpu.VMEM((q + 8, conv_ch), F32),
            pltpu.VMEM((q, inner), F32),
            pltpu.VMEM((q, bc), BF16),
            pltpu.VMEM((q, bc), BF16),
            pltpu.VMEM((SSD_GROUPS, SSD_STATE, inner // SSD_GROUPS), F32),
            pltpu.VMEM((q, inner), F32),
        ],
        compiler_params=_params("parallel", "arbitrary"),
        name="ssd_branch",
    )(h3, w_in, proj3, proj3, proj3, proj3, conv_w, row(conv_b),
      row(jnp.pad(dt_bias, (0, pad))), row(jnp.pad(a_log, (0, pad))),
      row(jnp.repeat(d_skip, SSD_HEAD_DIM)), row(norm_g), expand)


def _pool_kernel(u_ref, halo_ref, z_ref, w_ref, sc_ref, o_ref, upad):
    tm = u_ref.shape[0]
    gw = w_ref.shape[-1]
    i = pl.program_id(1)
    upad[POOL_HALO:POOL_HALO + tm, :] = u_ref[...]

    @pl.when(i == 0)
    def _():
        upad[0:POOL_HALO, :] = jnp.zeros((POOL_HALO, upad.shape[1]), F32)

    @pl.when(i > 0)
    def _():
        upad[0:POOL_HALO, :] = halo_ref[...]

    pos = i * tm + lax.broadcasted_iota(jnp.int32, (tm, gw), 0)
    for gi, w in enumerate(POOL_WINDOWS):
        cs = slice(gi * gw, (gi + 1) * gw)
        a = upad[:, cs]
        s = a
        k = 1
        while k < w:
            s = s + pltpu.roll(s, k, axis=0)
            k *= 2
        count = jnp.minimum(pos + 1, w).astype(F32)
        pooled = s[POOL_HALO:, :] / count - a[POOL_HALO:, :]
        mixed = _dot(pooled.astype(BF16), w_ref[gi].astype(BF16))
        o_ref[:, cs] = (mixed * sc_ref[:, cs] * _silu(z_ref[:, cs])).astype(o_ref.dtype)


def _pool_branch(proj3, pool_w, layer, pool_scale, tm=512):
    b, seq, _ = proj3.shape
    _, ng, gw, _ = pool_w.shape
    width = ng * gw
    hb = tm // POOL_HALO
    return pl.pallas_call(
        _pool_kernel,
        grid=(b, seq // tm),
        in_specs=[
            pl.BlockSpec((None, tm, width), lambda bi, i: (bi, i, 0)),
            pl.BlockSpec((None, POOL_HALO, width), lambda bi, i: (bi, jnp.maximum(i * hb - 1, 0), 0)),
            pl.BlockSpec((None, tm, width), lambda bi, i: (bi, i, 1)),
            pl.BlockSpec((None, ng, gw, gw), lambda bi, i: (layer, 0, 0, 0)),
            pl.BlockSpec((1, width), lambda bi, i: (0, 0)),
        ],
        out_specs=pl.BlockSpec((None, tm, width), lambda bi, i: (bi, i, 0)),
        out_shape=jax.ShapeDtypeStruct((b, seq, width), BF16),
        scratch_shapes=[pltpu.VMEM((tm + POOL_HALO, width), F32)],
        compiler_params=_params("parallel", "arbitrary"),
        name="pool_branch",
    )(proj3, proj3, proj3, pool_w, pool_scale.reshape(1, width))


def _attn_kernel(q0, q1, q2, k0, k1, k2, v0, v1, v2, z_ref, o_ref,
                 op_scr, lp_scr, o0, o1, o2, l0, l1, l2):
    seq, hd = q0.shape
    blk = ATTN_BLOCK
    scale = hd ** -0.5
    qs, ks, vs = (q0, q1, q2), (k0, k1, k2), (v0, v1, v2)
    o_scrs, l_scrs = (o0, o1, o2), (l0, l1, l2)
    qi = lax.broadcasted_iota(jnp.int32, (blk, 2 * blk), 0)
    ci = lax.broadcasted_iota(jnp.int32, (blk, 2 * blk), 1)
    band = (ci >= qi) & (ci <= qi + blk)
    diag = (lax.broadcasted_iota(jnp.int32, (blk, blk), 1)
            <= lax.broadcasted_iota(jnp.int32, (blk, blk), 0))

    for g, (_, dil) in enumerate(ATTN_PATTERNS):
        ld = seq // dil
        nbs = ld // blk
        o_dst = o_scrs[g] if dil == 1 else op_scr
        l_dst = l_scrs[g] if dil == 1 else lp_scr
        for n in range(seq // blk):
            rows = slice(n * blk, (n + 1) * blk)
            if n % nbs == 0:
                krows, mask = rows, diag
            else:
                krows, mask = slice((n - 1) * blk, (n + 1) * blk), band
            s = jnp.where(mask, _dot_nt(qs[g][rows, :], ks[g][krows, :]) * scale, -jnp.inf)
            m = jnp.max(s, axis=-1, keepdims=True)
            e = jnp.exp(s - m)
            den = jnp.sum(e, axis=-1, keepdims=True)
            o_dst[rows, :] = _dot((e / den).astype(BF16), vs[g][krows, :])
            l_dst[rows, :] = jnp.broadcast_to(m + jnp.log(den), (blk, hd))
        if dil > 1:
            for r in range(dil):
                o_scrs[g][pl.ds(r, ld, stride=dil), :] = op_scr[r * ld:(r + 1) * ld, :]
                l_scrs[g][pl.ds(r, ld, stride=dil), :] = lp_scr[r * ld:(r + 1) * ld, :]

    la, lb, lc = l0[...], l1[...], l2[...]
    mx = jnp.maximum(jnp.maximum(la, lb), lc)
    wa, wb, wc = jnp.exp(la - mx), jnp.exp(lb - mx), jnp.exp(lc - mx)
    tot = wa + wb + wc
    comb = (wa / tot) * o0[...] + (wb / tot) * o1[...] + (wc / tot) * o2[...]
    o_ref[...] = (comb * _silu(z_ref[...])).astype(o_ref.dtype)


def _attn_branch(qkv3, z3):
    b, seq, _ = qkv3.shape
    hd = ATTN_HEAD_DIM
    hpg = ATTN_HEADS_PER_GROUP
    ngroups = len(ATTN_PATTERNS)

    def head_spec(kind, g):
        base = (kind * ngroups + g) * hpg
        return pl.BlockSpec((None, seq, hd), lambda bi, j: (bi, 0, base + j))

    in_specs = [head_spec(kind, g) for kind in range(3) for g in range(ngroups)]
    in_specs.append(pl.BlockSpec((None, seq, hd), lambda bi, j: (bi, 0, j)))
    scratch = [pltpu.VMEM((seq, hd), F32)] * (2 + 2 * ngroups)
    return pl.pallas_call(
        _attn_kernel,
        grid=(b, hpg),
        in_specs=in_specs,
        out_specs=pl.BlockSpec((None, seq, hd), lambda bi, j: (bi, 0, j)),
        out_shape=jax.ShapeDtypeStruct((b, seq, hpg * hd), BF16),
        scratch_shapes=scratch,
        compiler_params=_params("parallel", "parallel"),
        name="attn_branch",
    )(*([qkv3] * 9), z3)


def _merge_kernel(ys_ref, yp_ref, ya_ref, w1_ref, w2_ref, w3_ref, g0_ref, g1_ref, g2_ref, o_ref,
                  wb1, wb2, wb3):
    @pl.when(pl.program_id(1) == 0)
    def _():
        _stage_weight(w1_ref, None, wb1, 0)
        _stage_weight(w2_ref, None, wb2, 0)
        _stage_weight(w3_ref, None, wb3, 0)

    m = jax.nn.sigmoid(g0_ref[...]) * _dot(ys_ref[...], wb1[...])
    m = m + jax.nn.sigmoid(g1_ref[...]) * _dot(yp_ref[...], wb2[...])
    m = m + jax.nn.sigmoid(g2_ref[...]) * _dot(ya_ref[...], wb3[...])
    o_ref[...] = m.astype(o_ref.dtype)


def _merge(y_ssd, y_pool, y_attn, w1, w2, w3, layer, gates2, bm=512, bn=1024):
    t, d = y_ssd.shape
    n = w1.shape[-1]
    gate_blocks = n // bn
    once = pl.Buffered(1)

    def gate_spec(i):
        return pl.BlockSpec((bm, bn), lambda ni, mi: (mi, i * gate_blocks + ni))

    def w_spec(w):
        return pl.BlockSpec((None, w.shape[1], bn), lambda ni, mi: (layer, 0, ni), pipeline_mode=once)

    return pl.pallas_call(
        _merge_kernel,
        grid=(n // bn, t // bm),
        in_specs=[pl.BlockSpec((bm, d), lambda ni, mi: (mi, 0)),
                  pl.BlockSpec((bm, y_pool.shape[1]), lambda ni, mi: (mi, 0)),
                  pl.BlockSpec((bm, y_attn.shape[1]), lambda ni, mi: (mi, 0)),
                  w_spec(w1), w_spec(w2), w_spec(w3),
                  gate_spec(0), gate_spec(1), gate_spec(2)],
        out_specs=pl.BlockSpec((bm, bn), lambda ni, mi: (mi, ni)),
        out_shape=jax.ShapeDtypeStruct((t, n), BF16),
        scratch_shapes=[pltpu.VMEM((w1.shape[1], bn), BF16), pltpu.VMEM((w2.shape[1], bn), BF16),
                        pltpu.VMEM((w3.shape[1], bn), BF16)],
        compiler_params=_params("arbitrary", "arbitrary"),
        name="gated_merge",
    )(y_ssd, y_pool, y_attn, w1, w2, w3, gates2, gates2, gates2)


def _out_kernel(a_ref, w_ref, r_ref, o_ref, wb):
    @pl.when(pl.program_id(1) == 0)
    def _():
        _stage_weight(w_ref, None, wb, 0)

    o_ref[...] = r_ref[...] + _dot(a_ref[...], wb[...])


def _out_proj(merged, w_out, layer, x2d, bm=1024, bn=1024):
    t, k = merged.shape
    n = w_out.shape[-1]
    return pl.pallas_call(
        _out_kernel,
        grid=(n // bn, t // bm),
        in_specs=[pl.BlockSpec((bm, k), lambda j, i: (i, 0)),
                  pl.BlockSpec((None, k, bn), lambda j, i: (layer, 0, j)),
                  pl.BlockSpec((bm, bn), lambda j, i: (i, j))],
        out_specs=pl.BlockSpec((bm, bn), lambda j, i: (i, j)),
        out_shape=jax.ShapeDtypeStruct((t, n), F32),
        scratch_shapes=[pltpu.VMEM((k, bn), BF16)],
        compiler_params=_params("arbitrary", "arbitrary"),
        name="out_proj",
    )(merged, w_out, x2d)


def _ple_kernel(h_ref, wg_ref, p_ref, wp_ref, x_ref, o_ref, wgb, wpb):
    @pl.when(pl.program_id(1) == 0)
    def _():
        _stage_weight(wg_ref, None, wgb, 0)
        wpb[...] = wp_ref[...].astype(BF16)

    gate = jax.nn.sigmoid(_dot(h_ref[...], wgb[...]))
    emb = _dot(p_ref[...].astype(BF16), wpb[...])
    o_ref[...] = x_ref[...] + gate * emb


def _ple_update(h, w_gate, p, w_proj, layer, x2d, bm=1024, bn=1024):
    t, k = h.shape
    n = w_gate.shape[-1]
    kp = p.shape[-1]
    return pl.pallas_call(
        _ple_kernel,
        grid=(n // bn, t // bm),
        in_specs=[pl.BlockSpec((bm, k), lambda j, i: (i, 0)),
                  pl.BlockSpec((None, k, bn), lambda j, i: (layer, 0, j)),
                  pl.BlockSpec((None, bm, kp), lambda j, i: (layer, i, 0)),
                  pl.BlockSpec((None, kp, bn), lambda j, i: (layer, 0, j)),
                  pl.BlockSpec((bm, bn), lambda j, i: (i, j))],
        out_specs=pl.BlockSpec((bm, bn), lambda j, i: (i, j)),
        out_shape=jax.ShapeDtypeStruct((t, n), F32),
        scratch_shapes=[pltpu.VMEM((k, bn), BF16), pltpu.VMEM((kp, bn), BF16)],
        compiler_params=_params("arbitrary", "arbitrary"),
        name="ple_update",
    )(h, w_gate, p, w_proj, x2d)


def _in_proj_columns(d_model):
    inner = SSD_HEADS * SSD_HEAD_DIM
    bc = SSD_GROUPS * SSD_STATE
    attn_w = len(ATTN_PATTERNS) * ATTN_HEADS_PER_GROUP * ATTN_HEAD_DIM
    attn_out = ATTN_HEADS_PER_GROUP * ATTN_HEAD_DIM
    sizes = [("z_ssd", inner), ("x", inner), ("B", bc), ("C", bc), ("dt", SSD_HEADS),
             ("u_pool", d_model), ("z_pool", d_model), ("q", attn_w), ("k", attn_w), ("v", attn_w),
             ("z_attn", attn_out), ("gates", 3 * d_model)]
    col, start = {}, 0
    for name, size in sizes:
        col[name] = start
        start += size
    col["end"] = start
    return col


def kernel(x, p, norm_g, w_in, conv_w, conv_b, dt_bias, a_log, d_skip, ssd_norm_g, w_br_ssd, pool_w,
           pool_scale, w_br_pool, q_norm_g, k_norm_g, w_br_attn, w_out, ple_norm_g, w_ple_gate,
           w_ple_proj):
    b, seq, d = x.shape
    depth = w_in.shape[0]
    t = b * seq
    col = _in_proj_columns(d)
    assert col["end"] == w_in.shape[-1]
    x2 = x.reshape(t, d)
    p2 = p.reshape(depth, t, -1)
    for i in range(depth):
        h = _rms_norm_bf16(x2, norm_g[i])
        ssd_in = _in_proj(h, w_in, i, col["z_ssd"], col["dt"] - col["z_ssd"], 1024, 1024, "in_proj_ssd")
        pool_in = _in_proj(h, w_in, i, col["u_pool"], col["q"] - col["u_pool"], 1024, 1024, "in_proj_pool")
        qkv = _qkv_proj(h, w_in, i, col["q"], b, seq, q_norm_g[i], k_norm_g[i])
        z_attn = _in_proj(h, w_in, i, col["z_attn"], col["gates"] - col["z_attn"], 1024, 512, "in_proj_zattn")
        gates = _in_proj(h, w_in, i, col["gates"], col["end"] - col["gates"], 1024, 1024, "in_proj_gates")

        y_ssd = _ssd_branch(h.reshape(b, seq, d), w_in, i, col["dt"], ssd_in.reshape(b, seq, -1),
                            conv_w[i], conv_b[i], dt_bias[i], a_log[i], d_skip[i], ssd_norm_g[i])
        y_pool = _pool_branch(pool_in.reshape(b, seq, -1), pool_w, i, pool_scale[i])
        y_attn = _attn_branch(qkv, z_attn.reshape(b, seq, -1))

        merged = _merge(y_ssd.reshape(t, -1), y_pool.reshape(t, -1), y_attn.reshape(t, -1),
                        w_br_ssd, w_br_pool, w_br_attn, i, gates)
        x2 = _out_proj(merged, w_out, i, x2)
        h2 = _rms_norm_bf16(x2, ple_norm_g[i])
        x2 = _ple_update(h2, w_ple_gate, p2, w_ple_proj, i, x2)
    return x2.reshape(b, seq, d)
```

```python
import functools

import jax
import jax.numpy as jnp
from jax import lax
from jax.experimental import pallas as pl
from jax.experimental.pallas import tpu as pltpu

F32 = jnp.float32
BF16 = jnp.bfloat16

RMS_EPS = 1e-6
SSD_HEAD_DIM = 64
SSD_HEADS = 32
SSD_GROUPS = 4
SSD_STATE = 128
SSD_CHUNK = 256
CONV_K = 4
POOL_WINDOWS = (2, 4, 8, 16)
POOL_HALO = 16
ATTN_PATTERNS = ((128, 1), (512, 4), (2048, 16))
ATTN_HEAD_DIM = 128
ATTN_HEADS_PER_GROUP = 4
ATTN_BLOCK = 128

LANE = 128
VMEM_LIMIT = 56 * 1024 * 1024
STAGE_ROWS = 256


def _params(*sem):
    return pltpu.CompilerParams(dimension_semantics=sem, vmem_limit_bytes=VMEM_LIMIT)


def _silu(v):
    return v * jax.nn.sigmoid(v)


def _dot(a, b):
    return jnp.dot(a, b, preferred_element_type=F32)


def _dot_nt(a, b):
    return lax.dot_general(a, b, (((1,), (1,)), ((), ())), preferred_element_type=F32)


def _dot_tn(a, b):
    return lax.dot_general(a, b, (((0,), (0,)), ((), ())), preferred_element_type=F32)


def _split3(v):
    hi = v.astype(BF16)
    r1 = v - hi.astype(F32)
    mid = r1.astype(BF16)
    lo = (r1 - mid.astype(F32)).astype(BF16)
    return hi, mid, lo


def _dot_f32_by_01(v, m01):
    hi, mid, lo = _split3(v)
    return _dot(hi, m01) + _dot(mid, m01) + _dot(lo, m01)


def _dot_01_by_f32(m01, v):
    hi, mid, lo = _split3(v)
    return _dot(m01, hi) + _dot(m01, mid) + _dot(m01, lo)


def _stage_weight(w_ref, dst):
    rows_total = dst.shape[0]
    step = min(STAGE_ROWS, rows_total)

    def body(c, carry):
        rows = pl.ds(pl.multiple_of(c * step, step), step)
        dst[rows, :] = w_ref[rows, :].astype(BF16)
        return carry

    lax.fori_loop(0, rows_total // step, body, 0)


def _fetch_weight_rows(wt_hbm, wf, wb, sem, layer, row0):
    n = pl.program_id(0)
    n_tiles = pl.num_programs(0)
    bn = wb.shape[0]

    def copy(tile, slot):
        rows = pl.ds(pl.multiple_of(row0 + tile * bn, 8), bn)
        return pltpu.make_async_copy(wt_hbm.at[layer, rows, :], wf.at[slot], sem.at[slot])

    @pl.when(pl.program_id(1) == 0)
    def _():
        slot = n % 2

        @pl.when(n == 0)
        def _():
            copy(0, 0).start()

        copy(n, slot).wait()

        @pl.when(n + 1 < n_tiles)
        def _():
            copy(n + 1, 1 - slot).start()

        _stage_weight(wf.at[slot], wb)


def _weight_scratch(bn, k):
    return [pltpu.VMEM((2, bn, k), F32), pltpu.VMEM((bn, k), BF16), pltpu.SemaphoreType.DMA((2,))]


def _rms_kernel(x_ref, g_ref, o_ref):
    x = x_ref[...]
    ms = jnp.mean(x * x, axis=-1, keepdims=True)
    o_ref[...] = ((x * lax.rsqrt(ms + RMS_EPS)) * g_ref[...]).astype(o_ref.dtype)


def _rms_norm_bf16(x2d, g, bm=512):
    t, d = x2d.shape
    return pl.pallas_call(
        _rms_kernel,
        grid=(t // bm,),
        in_specs=[pl.BlockSpec((bm, d), lambda i: (i, 0)),
                  pl.BlockSpec((1, d), lambda i: (0, 0))],
        out_specs=pl.BlockSpec((bm, d), lambda i: (i, 0)),
        out_shape=jax.ShapeDtypeStruct((t, d), BF16),
        compiler_params=_params("parallel"),
        name="rms_norm",
    )(x2d, g.reshape(1, d))


def _proj_kernel(a_ref, wt_hbm, o_ref, wf, wb, sem, *, layer, row0):
    _fetch_weight_rows(wt_hbm, wf, wb, sem, layer, row0)
    o_ref[...] = _dot_nt(a_ref[...], wb[...]).astype(o_ref.dtype)


def _in_proj(h, w_in_t, layer, col0, width, bm, bn, name):
    t, k = h.shape
    assert col0 % 8 == 0 and width % bn == 0
    return pl.pallas_call(
        functools.partial(_proj_kernel, layer=layer, row0=col0),
        grid=(width // bn, t // bm),
        in_specs=[pl.BlockSpec((bm, k), lambda n, m: (m, 0)),
                  pl.BlockSpec(memory_space=pl.ANY)],
        out_specs=pl.BlockSpec((bm, bn), lambda n, m: (m, n)),
        out_shape=jax.ShapeDtypeStruct((t, width), F32),
        scratch_shapes=_weight_scratch(bn, k),
        compiler_params=_params("arbitrary", "arbitrary"),
        name=name,
    )(h, w_in_t)


def _qkv_kernel(a_ref, wt_hbm, gq_ref, gk_ref, o_ref, wf, wb, sem, acc_scr, *, layer, row0, seq):
    n = pl.program_id(0)
    m = pl.program_id(1)
    bm = a_ref.shape[0]
    hd = ATTN_HEAD_DIM
    ngroups = len(ATTN_PATTERNS)
    half = m % (seq // bm)

    _fetch_weight_rows(wt_hbm, wf, wb, sem, layer, row0)
    acc = _dot_nt(a_ref[...], wb[...])
    for hh in range(ATTN_HEADS_PER_GROUP):
        acc_scr[hh] = acc[:, hh * hd:(hh + 1) * hd]

    for kind in range(3):
        for g, (_, dil) in enumerate(ATTN_PATTERNS):
            @pl.when(n == kind * ngroups + g)
            def _(kind=kind, dil=dil):
                cnt = bm // dil
                ld = seq // dil
                for hh in range(ATTN_HEADS_PER_GROUP):
                    cs = slice(hh * hd, (hh + 1) * hd)
                    v = acc_scr[hh]
                    if kind < 2:
                        g_ref = gq_ref if kind == 0 else gk_ref
                        ms = jnp.mean(v * v, axis=-1, keepdims=True)
                        v = (v * lax.rsqrt(ms + RMS_EPS)) * g_ref[...]
                    if dil == 1:
                        o_ref[pl.ds(pl.multiple_of(half * bm, bm), bm), cs] = v.astype(BF16)
                    else:
                        if kind < 2:
                            acc_scr[hh] = v
                        for r in range(dil):
                            rows = acc_scr[hh, pl.ds(r, cnt, stride=dil), :]
                            dst = r * ld + pl.multiple_of(half * cnt, cnt)
                            o_ref[pl.ds(dst, cnt), cs] = rows.astype(BF16)


def _qkv_proj(h, w_in_t, layer, col0, b, seq, q_norm_g, k_norm_g, bm=1024):
    t, k = h.shape
    hd = ATTN_HEAD_DIM
    bn = ATTN_HEADS_PER_GROUP * hd
    n_tiles = 3 * len(ATTN_PATTERNS)
    tiles_per_seq = seq // bm
    assert col0 % 8 == 0
    return pl.pallas_call(
        functools.partial(_qkv_kernel, layer=layer, row0=col0, seq=seq),
        grid=(n_tiles, t // bm),
        in_specs=[pl.BlockSpec((bm, k), lambda n, m: (m, 0)),
                  pl.BlockSpec(memory_space=pl.ANY)]
                 + [pl.BlockSpec((1, hd), lambda n, m: (0, 0))] * 2,
        out_specs=pl.BlockSpec((None, seq, bn), lambda n, m: (m // tiles_per_seq, 0, n)),
        out_shape=jax.ShapeDtypeStruct((b, seq, n_tiles * bn), BF16),
        scratch_shapes=_weight_scratch(bn, k) + [pltpu.VMEM((ATTN_HEADS_PER_GROUP, bm, hd), F32)],
        compiler_params=_params("arbitrary", "arbitrary"),
        name="qkv_proj",
    )(h, w_in_t, q_norm_g.reshape(1, hd), k_norm_g.reshape(1, hd))


def _ssd_kernel(h_ref, wdt_ref, z_ref, x_ref, b_ref, c_ref, cw_ref, cb_ref, dtb_ref, alog_ref,
                dsk_ref, ng_ref, e_ref, o_ref, xpad, xs_scr, bb_scr, cc_scr, s_scr, y_scr):
    q = SSD_CHUNK
    inner = x_ref.shape[-1]
    gw = inner // SSD_GROUPS
    n_state = SSD_STATE
    halo = 8

    @pl.when(pl.program_id(1) == 0)
    def _():
        xpad[:, 0:halo, :] = jnp.zeros((xpad.shape[0], halo, LANE), F32)
        s_scr[...] = jnp.zeros(s_scr.shape, F32)

    x_slabs = inner // LANE
    bc_slabs = b_ref.shape[-1] // LANE
    for j in range(xpad.shape[0]):
        cs = slice(j * LANE, (j + 1) * LANE)
        if j < x_slabs:
            src, sj = x_ref, j
        elif j < x_slabs + bc_slabs:
            src, sj = b_ref, j - x_slabs
        else:
            src, sj = c_ref, j - x_slabs - bc_slabs
        ss = slice(sj * LANE, (sj + 1) * LANE)
        xpad[j, halo:halo + q, :] = src[:, ss]
        acc = cb_ref[:, cs] + cw_ref[0:1, cs] * xpad[j, halo - CONV_K + 1:halo - CONV_K + 1 + q, :]
        for k in range(1, CONV_K):
            r0 = halo - CONV_K + 1 + k
            acc = acc + cw_ref[k:k + 1, cs] * xpad[j, r0:r0 + q, :]
        xc = _silu(acc)
        if src is x_ref:
            xs_scr[:, ss] = xc
        elif src is b_ref:
            bb_scr[:, ss] = xc.astype(BF16)
        else:
            cc_scr[:, ss] = xc.astype(BF16)
        xpad[j, 0:halo, :] = xpad[j, q:q + halo, :]

    dt_lanes = lax.broadcasted_iota(jnp.int32, (q, LANE), 1) < SSD_HEADS
    dt_raw = jnp.where(dt_lanes, _dot_nt(h_ref[...], wdt_ref[...].astype(BF16)), 0.0)
    dtr = dt_raw + dtb_ref[...]
    dtv = jnp.maximum(dtr, 0.0) + jnp.log1p(jnp.exp(-jnp.abs(dtr)))
    da = dtv * (-jnp.exp(alog_ref[...]))
    ri = lax.broadcasted_iota(jnp.int32, (q, q), 0)
    ci = lax.broadcasted_iota(jnp.int32, (q, q), 1)
    causal = ri >= ci
    tri = jnp.where(causal, 1.0, 0.0).astype(BF16)
    tri_t = jnp.where(ri <= ci, 1.0, 0.0).astype(BF16)
    acum = _dot_01_by_f32(tri, da)
    acum_row = _dot_f32_by_01(da.T, tri_t)

    first_lane_half = lax.broadcasted_iota(jnp.int32, (q, LANE), 1) < SSD_HEAD_DIM
    heads_per_group = SSD_HEADS // SSD_GROUPS
    for g in range(SSD_GROUPS):
        gs = slice(g * gw, (g + 1) * gw)
        e_g = e_ref[:, gs]
        acum_e = _dot_f32_by_01(acum, e_g)
        dt_e = _dot_f32_by_01(dtv, e_g)
        xh = xs_scr[:, gs] * dt_e
        xh_b = xh.astype(BF16)
        last_e = acum_e[q - 1:q, :]
        xdec = (xh * jnp.exp(last_e - acum_e)).astype(BF16)
        bg = bb_scr[:, g * n_state:(g + 1) * n_state]
        cg = cc_scr[:, g * n_state:(g + 1) * n_state]
        s_old = s_scr[g]
        y_off = _dot(cg, s_old.astype(BF16)) * jnp.exp(acum_e)
        s_scr[g] = jnp.exp(last_e) * s_old + _dot_tn(bg, xdec)
        cb = _dot_nt(cg, bg)
        for hh in range(heads_per_group // 2):
            xp = xh_b[:, hh * LANE:(hh + 1) * LANE]
            ys = []
            for e in range(2):
                h = g * heads_per_group + 2 * hh + e
                seg = acum[:, h:h + 1] - acum_row[h:h + 1, :]
                lmat = jnp.exp(jnp.where(causal, seg, -jnp.inf))
                ys.append(_dot((cb * lmat).astype(BF16), xp))
            y_pair = jnp.where(first_lane_half, ys[0], ys[1])
            c0 = g * gw + hh * LANE
            y_scr[:, c0:c0 + LANE] = y_pair + y_off[:, hh * LANE:(hh + 1) * LANE]

    y = y_scr[...] + dsk_ref[...] * xs_scr[...]
    gated = y * _silu(z_ref[...])
    ms = jnp.mean(gated * gated, axis=-1, keepdims=True)
    o_ref[...] = ((gated * lax.rsqrt(ms + RMS_EPS)) * ng_ref[...]).astype(o_ref.dtype)


def _ssd_branch(h3, w_in_t, layer, dt_col, proj3, conv_w, conv_b, dt_bias, a_log, d_skip, norm_g):
    b, seq, d = h3.shape
    q = SSD_CHUNK
    inner = SSD_HEADS * SSD_HEAD_DIM
    bc = SSD_GROUPS * SSD_STATE
    conv_ch = inner + 2 * bc
    pad = LANE - SSD_HEADS
    assert dt_col % LANE == 0
    expand = (jnp.arange(inner)[None, :] // SSD_HEAD_DIM == jnp.arange(LANE)[:, None]).astype(BF16)
    row = lambda v: v.reshape(1, -1)
    const = lambda shape: pl.BlockSpec(shape, lambda i, c: (0,) * len(shape))
    return pl.pallas_call(
        _ssd_kernel,
        grid=(b, seq // q),
        in_specs=[
            pl.BlockSpec((None, q, d), lambda i, c: (i, c, 0)),
            pl.BlockSpec((None, LANE, d), lambda i, c: (layer, dt_col // LANE, 0)),
            pl.BlockSpec((None, q, inner), lambda i, c: (i, c, 0)),
            pl.BlockSpec((None, q, inner), lambda i, c: (i, c, 1)),
            pl.BlockSpec((None, q, bc), lambda i, c: (i, c, 2 * inner // bc)),
            pl.BlockSpec((None, q, bc), lambda i, c: (i, c, 2 * inner // bc + 1)),
            const((CONV_K, conv_ch)), const((1, conv_ch)), const((1, LANE)), const((1, LANE)),
            const((1, inner)), const((1, inner)), const((LANE, inner)),
        ],
        out_specs=pl.BlockSpec((None, q, inner), lambda i, c: (i, c, 0)),
        out_shape=jax.ShapeDtypeStruct((b, seq, inner), BF16),
        scratch_shapes=[
            pltpu.VMEM((conv_ch // LANE, q + 8, LANE), F32),
            pltpu.VMEM((q, inner), F32),
            pltpu.VMEM((q, bc), BF16),
            pltpu.VMEM((q, bc), BF16),
            pltpu.VMEM((SSD_GROUPS, SSD_STATE, inner // SSD_GROUPS), F32),
            pltpu.VMEM((q, inner), F32),
        ],
        compiler_params=_params("parallel", "arbitrary"),
        name="ssd_branch",
    )(h3, w_in_t, proj3, proj3, proj3, proj3, conv_w, row(conv_b),
      row(jnp.pad(dt_bias, (0, pad))), row(jnp.pad(a_log, (0, pad))),
      row(jnp.repeat(d_skip, SSD_HEAD_DIM)), row(norm_g), expand)


def _pool_kernel(u_ref, halo_ref, z_ref, w_ref, sc_ref, o_ref, upad):
    tm = u_ref.shape[0]
    gw = w_ref.shape[-1]
    i = pl.program_id(1)
    upad[POOL_HALO:POOL_HALO + tm, :] = u_ref[...]

    @pl.when(i == 0)
    def _():
        upad[0:POOL_HALO, :] = jnp.zeros((POOL_HALO, upad.shape[1]), F32)

    @pl.when(i > 0)
    def _():
        upad[0:POOL_HALO, :] = halo_ref[...]

    pos = i * tm + lax.broadcasted_iota(jnp.int32, (tm, gw), 0)
    for gi, w in enumerate(POOL_WINDOWS):
        cs = slice(gi * gw, (gi + 1) * gw)
        a = upad[:, cs]
        s = a
        k = 1
        while k < w:
            s = s + pltpu.roll(s, k, axis=0)
            k *= 2
        count = jnp.minimum(pos + 1, w).astype(F32)
        pooled = s[POOL_HALO:, :] / count - a[POOL_HALO:, :]
        mixed = _dot(pooled.astype(BF16), w_ref[gi].astype(BF16))
        o_ref[:, cs] = (mixed * sc_ref[:, cs] * _silu(z_ref[:, cs])).astype(o_ref.dtype)


def _pool_branch(proj3, pool_w, layer, pool_scale, tm=512):
    b, seq, _ = proj3.shape
    _, ng, gw, _ = pool_w.shape
    width = ng * gw
    hb = tm // POOL_HALO
    return pl.pallas_call(
        _pool_kernel,
        grid=(b, seq // tm),
        in_specs=[
            pl.BlockSpec((None, tm, width), lambda bi, i: (bi, i, 0)),
            pl.BlockSpec((None, POOL_HALO, width), lambda bi, i: (bi, jnp.maximum(i * hb - 1, 0), 0)),
            pl.BlockSpec((None, tm, width), lambda bi, i: (bi, i, 1)),
            pl.BlockSpec((None, ng, gw, gw), lambda bi, i: (layer, 0, 0, 0)),
            pl.BlockSpec((1, width), lambda bi, i: (0, 0)),
        ],
        out_specs=pl.BlockSpec((None, tm, width), lambda bi, i: (bi, i, 0)),
        out_shape=jax.ShapeDtypeStruct((b, seq, width), BF16),
        scratch_shapes=[pltpu.VMEM((tm + POOL_HALO, width), F32)],
        compiler_params=_params("parallel", "arbitrary"),
        name="pool_branch",
    )(proj3, proj3, proj3, pool_w, pool_scale.reshape(1, width))


def _attn_kernel(q0, q1, q2, k0, k1, k2, v0, v1, v2, z_ref, o_ref,
                 op_scr, lp_scr, o0, o1, o2, l0, l1, l2):
    seq, hd = q0.shape
    blk = ATTN_BLOCK
    scale = hd ** -0.5
    qs, ks, vs = (q0, q1, q2), (k0, k1, k2), (v0, v1, v2)
    o_scrs, l_scrs = (o0, o1, o2), (l0, l1, l2)
    qi = lax.broadcasted_iota(jnp.int32, (blk, 2 * blk), 0)
    ci = lax.broadcasted_iota(jnp.int32, (blk, 2 * blk), 1)
    band = (ci >= qi) & (ci <= qi + blk)
    diag = (lax.broadcasted_iota(jnp.int32, (blk, blk), 1)
            <= lax.broadcasted_iota(jnp.int32, (blk, blk), 0))

    for g, (_, dil) in enumerate(ATTN_PATTERNS):
        ld = seq // dil
        nbs = ld // blk
        o_dst = o_scrs[g] if dil == 1 else op_scr
        l_dst = l_scrs[g] if dil == 1 else lp_scr
        for n in range(seq // blk):
            rows = slice(n * blk, (n + 1) * blk)
            if n % nbs == 0:
                krows, mask = rows, diag
            else:
                krows, mask = slice((n - 1) * blk, (n + 1) * blk), band
            s = jnp.where(mask, _dot_nt(qs[g][rows, :], ks[g][krows, :]) * scale, -jnp.inf)
            m = jnp.max(s, axis=-1, keepdims=True)
            e = jnp.exp(s - m)
            den = jnp.sum(e, axis=-1, keepdims=True)
            o_dst[rows, :] = _dot((e / den).astype(BF16), vs[g][krows, :])
            l_dst[rows, :] = jnp.broadcast_to(m + jnp.log(den), (blk, hd))
        if dil > 1:
            for r in range(dil):
                o_scrs[g][pl.ds(r, ld, stride=dil), :] = op_scr[r * ld:(r + 1) * ld, :]
                l_scrs[g][pl.ds(r, ld, stride=dil), :] = lp_scr[r * ld:(r + 1) * ld, :]

    la, lb, lc = l0[...], l1[...], l2[...]
    mx = jnp.maximum(jnp.maximum(la, lb), lc)
    wa, wb, wc = jnp.exp(la - mx), jnp.exp(lb - mx), jnp.exp(lc - mx)
    tot = wa + wb + wc
    comb = (wa / tot) * o0[...] + (wb / tot) * o1[...] + (wc / tot) * o2[...]
    o_ref[...] = (comb * _silu(z_ref[...])).astype(o_ref.dtype)


def _attn_branch(qkv3, z3):
    b, seq, _ = qkv3.shape
    hd = ATTN_HEAD_DIM
    hpg = ATTN_HEADS_PER_GROUP
    ngroups = len(ATTN_PATTERNS)

    def head_spec(kind, g):
        base = (kind * ngroups + g) * hpg
        return pl.BlockSpec((None, seq, hd), lambda bi, j: (bi, 0, base + j))

    in_specs = [head_spec(kind, g) for kind in range(3) for g in range(ngroups)]
    in_specs.append(pl.BlockSpec((None, seq, hd), lambda bi, j: (bi, 0, j)))
    scratch = [pltpu.VMEM((seq, hd), F32)] * (2 + 2 * ngroups)
    return pl.pallas_call(
        _attn_kernel,
        grid=(b, hpg),
        in_specs=in_specs,
        out_specs=pl.BlockSpec((None, seq, hd), lambda bi, j: (bi, 0, j)),
        out_shape=jax.ShapeDtypeStruct((b, seq, hpg * hd), BF16),
        scratch_shapes=scratch,
        compiler_params=_params("parallel", "parallel"),
        name="attn_branch",
    )(*([qkv3] * 9), z3)


def _merge_kernel(ys_ref, yp_ref, ya_ref, w1_ref, w2_ref, w3_ref, g0_ref, g1_ref, g2_ref, o_ref,
                  wb1, wb2, wb3):
    @pl.when(pl.program_id(1) == 0)
    def _():
        _stage_weight(w1_ref, wb1)
        _stage_weight(w2_ref, wb2)
        _stage_weight(w3_ref, wb3)

    m = jax.nn.sigmoid(g0_ref[...]) * _dot(ys_ref[...], wb1[...])
    m = m + jax.nn.sigmoid(g1_ref[...]) * _dot(yp_ref[...], wb2[...])
    m = m + jax.nn.sigmoid(g2_ref[...]) * _dot(ya_ref[...], wb3[...])
    o_ref[...] = m.astype(o_ref.dtype)


def _merge(y_ssd, y_pool, y_attn, w1, w2, w3, layer, gates2, bm=512, bn=1024):
    t, d = y_ssd.shape
    n = w1.shape[-1]
    gate_blocks = n // bn
    once = pl.Buffered(1)

    def gate_spec(i):
        return pl.BlockSpec((bm, bn), lambda ni, mi: (mi, i * gate_blocks + ni))

    def w_spec(w):
        return pl.BlockSpec((None, w.shape[1], bn), lambda ni, mi: (layer, 0, ni), pipeline_mode=once)

    return pl.pallas_call(
        _merge_kernel,
        grid=(n // bn, t // bm),
        in_specs=[pl.BlockSpec((bm, d), lambda ni, mi: (mi, 0)),
                  pl.BlockSpec((bm, y_pool.shape[1]), lambda ni, mi: (mi, 0)),
                  pl.BlockSpec((bm, y_attn.shape[1]), lambda ni, mi: (mi, 0)),
                  w_spec(w1), w_spec(w2), w_spec(w3),
                  gate_spec(0), gate_spec(1), gate_spec(2)],
        out_specs=pl.BlockSpec((bm, bn), lambda ni, mi: (mi, ni)),
        out_shape=jax.ShapeDtypeStruct((t, n), BF16),
        scratch_shapes=[pltpu.VMEM((w1.shape[1], bn), BF16), pltpu.VMEM((w2.shape[1], bn), BF16),
                        pltpu.VMEM((w3.shape[1], bn), BF16)],
        compiler_params=_params("arbitrary", "arbitrary"),
        name="gated_merge",
    )(y_ssd, y_pool, y_attn, w1, w2, w3, gates2, gates2, gates2)


def _out_kernel(a_ref, w_ref, r_ref, o_ref, wb):
    @pl.when(pl.program_id(1) == 0)
    def _():
        _stage_weight(w_ref, wb)

    o_ref[...] = r_ref[...] + _dot(a_ref[...], wb[...])


def _out_proj(merged, w_out, layer, x2d, bm=1024, bn=1024):
    t, k = merged.shape
    n = w_out.shape[-1]
    return pl.pallas_call(
        _out_kernel,
        grid=(n // bn, t // bm),
        in_specs=[pl.BlockSpec((bm, k), lambda j, i: (i, 0)),
                  pl.BlockSpec((None, k, bn), lambda j, i: (layer, 0, j)),
                  pl.BlockSpec((bm, bn), lambda j, i: (i, j))],
        out_specs=pl.BlockSpec((bm, bn), lambda j, i: (i, j)),
        out_shape=jax.ShapeDtypeStruct((t, n), F32),
        scratch_shapes=[pltpu.VMEM((k, bn), BF16)],
        compiler_params=_params("arbitrary", "arbitrary"),
        name="out_proj",
    )(merged, w_out, x2d)


def _ple_kernel(h_ref, wg_ref, p_ref, wp_ref, x_ref, o_ref, wgb, wpb):
    @pl.when(pl.program_id(1) == 0)
    def _():
        _stage_weight(wg_ref, wgb)
        wpb[...] = wp_ref[...].astype(BF16)

    gate = jax.nn.sigmoid(_dot(h_ref[...], wgb[...]))
    emb = _dot(p_ref[...].astype(BF16), wpb[...])
    o_ref[...] = x_ref[...] + gate * emb


def _ple_update(h, w_gate, p, w_proj, layer, x2d, bm=1024, bn=1024):
    t, k = h.shape
    n = w_gate.shape[-1]
    kp = p.shape[-1]
    return pl.pallas_call(
        _ple_kernel,
        grid=(n // bn, t // bm),
        in_specs=[pl.BlockSpec((bm, k), lambda j, i: (i, 0)),
                  pl.BlockSpec((None, k, bn), lambda j, i: (layer, 0, j)),
                  pl.BlockSpec((None, bm, kp), lambda j, i: (layer, i, 0)),
                  pl.BlockSpec((None, kp, bn), lambda j, i: (layer, 0, j)),
                  pl.BlockSpec((bm, bn), lambda j, i: (i, j))],
        out_specs=pl.BlockSpec((bm, bn), lambda j, i: (i, j)),
        out_shape=jax.ShapeDtypeStruct((t, n), F32),
        scratch_shapes=[pltpu.VMEM((k, bn), BF16), pltpu.VMEM((kp, bn), BF16)],
        compiler_params=_params("arbitrary", "arbitrary"),
        name="ple_update",
    )(h, w_gate, p, w_proj, x2d)


def _in_proj_columns(d_model):
    inner = SSD_HEADS * SSD_HEAD_DIM
    bc = SSD_GROUPS * SSD_STATE
    attn_w = len(ATTN_PATTERNS) * ATTN_HEADS_PER_GROUP * ATTN_HEAD_DIM
    attn_out = ATTN_HEADS_PER_GROUP * ATTN_HEAD_DIM
    sizes = [("z_ssd", inner), ("x", inner), ("B", bc), ("C", bc), ("dt", SSD_HEADS),
             ("u_pool", d_model), ("z_pool", d_model), ("q", attn_w), ("k", attn_w), ("v", attn_w),
             ("z_attn", attn_out), ("gates", 3 * d_model)]
    col, start = {}, 0
    for name, size in sizes:
        col[name] = start
        start += size
    col["end"] = start
    return col


def kernel(x, p, norm_g, w_in, conv_w, conv_b, dt_bias, a_log, d_skip, ssd_norm_g, w_br_ssd, pool_w,
           pool_scale, w_br_pool, q_norm_g, k_norm_g, w_br_attn, w_out, ple_norm_g, w_ple_gate,
           w_ple_proj):
    b, seq, d = x.shape
    depth = w_in.shape[0]
    t = b * seq
    col = _in_proj_columns(d)
    assert col["end"] == w_in.shape[-1]
    x2 = x.reshape(t, d)
    p2 = p.reshape(depth, t, -1)
    w_in_t = jnp.swapaxes(w_in, 1, 2)
    for i in range(depth):
        h = _rms_norm_bf16(x2, norm_g[i])
        ssd_in = _in_proj(h, w_in_t, i, col["z_ssd"], col["dt"] - col["z_ssd"], 1024, 1024, "in_proj_ssd")
        pool_in = _in_proj(h, w_in_t, i, col["u_pool"], col["q"] - col["u_pool"], 1024, 1024, "in_proj_pool")
        qkv = _qkv_proj(h, w_in_t, i, col["q"], b, seq, q_norm_g[i], k_norm_g[i])
        z_attn = _in_proj(h, w_in_t, i, col["z_attn"], col["gates"] - col["z_attn"], 1024, 512, "in_proj_zattn")
        gates = _in_proj(h, w_in_t, i, col["gates"], col["end"] - col["gates"], 1024, 1024, "in_proj_gates")

        y_ssd = _ssd_branch(h.reshape(b, seq, d), w_in_t, i, col["dt"], ssd_in.reshape(b, seq, -1),
                            conv_w[i], conv_b[i], dt_bias[i], a_log[i], d_skip[i], ssd_norm_g[i])
        y_pool = _pool_branch(pool_in.reshape(b, seq, -1), pool_w, i, pool_scale[i])
        y_attn = _attn_branch(qkv, z_attn.reshape(b, seq, -1))

        merged = _merge(y_ssd.reshape(t, -1), y_pool.reshape(t, -1), y_attn.reshape(t, -1),
                        w_br_ssd, w_br_pool, w_br_attn, i, gates)
        x2 = _out_proj(merged, w_out, i, x2)
        h2 = _rms_norm_bf16(x2, ple_norm_g[i])
        x2 = _ple_update(h2, w_ple_gate, p2, w_ple_proj, i, x2)
    return x2.reshape(b, seq, d)
```

```python
import functools

import jax
import jax.numpy as jnp
from jax import lax
from jax.experimental import pallas as pl
from jax.experimental.pallas import tpu as pltpu

F32 = jnp.float32
BF16 = jnp.bfloat16

RMS_EPS = 1e-6
SSD_HEAD_DIM = 64
SSD_HEADS = 32
SSD_GROUPS = 4
SSD_STATE = 128
SSD_CHUNK = 256
CONV_K = 4
POOL_WINDOWS = (2, 4, 8, 16)
POOL_HALO = 16
ATTN_PATTERNS = ((128, 1), (512, 4), (2048, 16))
ATTN_HEAD_DIM = 128
ATTN_HEADS_PER_GROUP = 4
ATTN_BLOCK = 128

LANE = 128
VMEM_LIMIT = 56 * 1024 * 1024
STAGE_ROWS = 256


def _params(*sem):
    return pltpu.CompilerParams(dimension_semantics=sem, vmem_limit_bytes=VMEM_LIMIT)


def _silu(v):
    return v * jax.nn.sigmoid(v)


def _dot(a, b):
    return jnp.dot(a, b, preferred_element_type=F32)


def _dot_nt(a, b):
    return lax.dot_general(a, b, (((1,), (1,)), ((), ())), preferred_element_type=F32)


def _dot_tn(a, b):
    return lax.dot_general(a, b, (((0,), (0,)), ((), ())), preferred_element_type=F32)


def _split3(v):
    hi = v.astype(BF16)
    r1 = v - hi.astype(F32)
    mid = r1.astype(BF16)
    lo = (r1 - mid.astype(F32)).astype(BF16)
    return hi, mid, lo


def _dot_f32_by_01(v, m01):
    hi, mid, lo = _split3(v)
    return _dot(hi, m01) + _dot(mid, m01) + _dot(lo, m01)


def _dot_01_by_f32(m01, v):
    hi, mid, lo = _split3(v)
    return _dot(m01, hi) + _dot(m01, mid) + _dot(m01, lo)


def _stage_weight(w_ref, dst):
    rows_total = dst.shape[0]
    step = min(STAGE_ROWS, rows_total)

    def body(c, carry):
        rows = pl.ds(pl.multiple_of(c * step, step), step)
        dst[rows, :] = w_ref[rows, :].astype(BF16)
        return carry

    lax.fori_loop(0, rows_total // step, body, 0)


def _fetch_weight_rows(wt_hbm, wf, wb, sem, layer, row0):
    n = pl.program_id(0)
    n_tiles = pl.num_programs(0)
    bn = wb.shape[0]

    def copy(tile, slot):
        rows = pl.ds(pl.multiple_of(row0 + tile * bn, 8), bn)
        return pltpu.make_async_copy(wt_hbm.at[layer, rows, :], wf.at[slot], sem.at[slot])

    @pl.when(pl.program_id(1) == 0)
    def _():
        slot = n % 2

        @pl.when(n == 0)
        def _():
            copy(0, 0).start()

        copy(n, slot).wait()

        @pl.when(n + 1 < n_tiles)
        def _():
            copy(n + 1, 1 - slot).start()

        _stage_weight(wf.at[slot], wb)


def _weight_scratch(bn, k):
    return [pltpu.VMEM((2, bn, k), F32), pltpu.VMEM((bn, k), BF16), pltpu.SemaphoreType.DMA((2,))]


def _stage_resident(hbm_rows, wb, stage, sem):
    n_chunks = wb.shape[0] // STAGE_ROWS

    def copy(c):
        return pltpu.make_async_copy(hbm_rows(c * STAGE_ROWS), stage.at[c % 2], sem.at[c % 2])

    copy(0).start()
    for c in range(n_chunks):
        if c + 1 < n_chunks:
            copy(c + 1).start()
        copy(c).wait()
        wb[c * STAGE_ROWS:(c + 1) * STAGE_ROWS, :] = stage[c % 2].astype(BF16)


def _resident_scratch(rows, cols):
    return [pltpu.VMEM((rows, cols), BF16)]


def _stage_buffers(cols):
    return [pltpu.VMEM((2, STAGE_ROWS, cols), F32), pltpu.SemaphoreType.DMA((2,))]


def _rms_kernel(x_ref, g_ref, o_ref):
    x = x_ref[...]
    ms = jnp.mean(x * x, axis=-1, keepdims=True)
    o_ref[...] = ((x * lax.rsqrt(ms + RMS_EPS)) * g_ref[...]).astype(o_ref.dtype)


def _rms_norm_bf16(x2d, g, bm=512):
    t, d = x2d.shape
    return pl.pallas_call(
        _rms_kernel,
        grid=(t // bm,),
        in_specs=[pl.BlockSpec((bm, d), lambda i: (i, 0)),
                  pl.BlockSpec((1, d), lambda i: (0, 0))],
        out_specs=pl.BlockSpec((bm, d), lambda i: (i, 0)),
        out_shape=jax.ShapeDtypeStruct((t, d), BF16),
        compiler_params=_params("parallel"),
        name="rms_norm",
    )(x2d, g.reshape(1, d))


def _proj_kernel(a_ref, wt_hbm, o_ref, wf, wb, sem, *, layer, row0):
    _fetch_weight_rows(wt_hbm, wf, wb, sem, layer, row0)
    o_ref[...] = _dot_nt(a_ref[...], wb[...]).astype(o_ref.dtype)


def _in_proj(h, w_in_t, layer, col0, width, bm, bn, name):
    t, k = h.shape
    assert col0 % 8 == 0 and width % bn == 0
    return pl.pallas_call(
        functools.partial(_proj_kernel, layer=layer, row0=col0),
        grid=(width // bn, t // bm),
        in_specs=[pl.BlockSpec((bm, k), lambda n, m: (m, 0)),
                  pl.BlockSpec(memory_space=pl.ANY)],
        out_specs=pl.BlockSpec((bm, bn), lambda n, m: (m, n)),
        out_shape=jax.ShapeDtypeStruct((t, width), F32),
        scratch_shapes=_weight_scratch(bn, k),
        compiler_params=_params("arbitrary", "arbitrary"),
        name=name,
    )(h, w_in_t)


def _head_proj_kernel(*refs, layer, row0, seq, normalize):
    if normalize:
        a_ref, wt_hbm, g_ref, o_ref, wb, stage, sem, acc_scr = refs
    else:
        a_ref, wt_hbm, o_ref, wb, stage, sem, acc_scr = refs
    m = pl.program_id(0)
    bm = a_ref.shape[0]
    hd = ATTN_HEAD_DIM
    gw = ATTN_HEADS_PER_GROUP * hd
    half = m % (seq // bm)

    @pl.when(m == 0)
    def _():
        _stage_resident(lambda r0: wt_hbm.at[layer, pl.ds(row0 + r0, STAGE_ROWS), :], wb, stage, sem)

    a = a_ref[...]
    slab = 0
    for g, (_, dil) in enumerate(ATTN_PATTERNS):
        acc = _dot_nt(a, wb[g * gw:(g + 1) * gw, :])
        cnt = bm // dil
        ld = seq // dil
        for hh in range(ATTN_HEADS_PER_GROUP):
            cs = slice(g * gw + hh * hd, g * gw + (hh + 1) * hd)
            v = acc[:, hh * hd:(hh + 1) * hd]
            if normalize:
                ms = jnp.mean(v * v, axis=-1, keepdims=True)
                v = (v * lax.rsqrt(ms + RMS_EPS)) * g_ref[...]
            if dil == 1:
                o_ref[pl.ds(pl.multiple_of(half * bm, bm), bm), cs] = v.astype(BF16)
            else:
                acc_scr[slab] = v
                for r in range(dil):
                    rows = acc_scr[slab, pl.ds(r, cnt, stride=dil), :]
                    dst = r * ld + pl.multiple_of(half * cnt, cnt)
                    o_ref[pl.ds(dst, cnt), cs] = rows.astype(BF16)
                slab += 1


def _head_proj(h, w_in_t, layer, col0, b, seq, norm_g, name, bm=1024):
    t, k = h.shape
    hd = ATTN_HEAD_DIM
    width = len(ATTN_PATTERNS) * ATTN_HEADS_PER_GROUP * hd
    n_dilated = sum(ATTN_HEADS_PER_GROUP for _, dil in ATTN_PATTERNS if dil > 1)
    tiles_per_seq = seq // bm
    assert col0 % 8 == 0 and width % STAGE_ROWS == 0
    normalize = norm_g is not None
    in_specs = [pl.BlockSpec((bm, k), lambda m: (m, 0)), pl.BlockSpec(memory_space=pl.ANY)]
    args = [h, w_in_t]
    if normalize:
        in_specs.append(pl.BlockSpec((1, hd), lambda m: (0, 0)))
        args.append(norm_g.reshape(1, hd))
    return pl.pallas_call(
        functools.partial(_head_proj_kernel, layer=layer, row0=col0, seq=seq, normalize=normalize),
        grid=(t // bm,),
        in_specs=in_specs,
        out_specs=pl.BlockSpec((None, seq, width), lambda m: (m // tiles_per_seq, 0, 0)),
        out_shape=jax.ShapeDtypeStruct((b, seq, width), BF16),
        scratch_shapes=_resident_scratch(width, k) + _stage_buffers(k)
                       + [pltpu.VMEM((n_dilated, bm, hd), F32)],
        compiler_params=_params("arbitrary"),
        name=name,
    )(*args)


def _ssd_kernel(h_ref, wdt_ref, z_ref, x_ref, b_ref, c_ref, cw_ref, cb_ref, dtb_ref, alog_ref,
                dsk_ref, ng_ref, e_ref, o_ref, xpad, xs_scr, bb_scr, cc_scr, s_scr, y_scr):
    q = SSD_CHUNK
    inner = x_ref.shape[-1]
    gw = inner // SSD_GROUPS
    n_state = SSD_STATE
    halo = 8

    @pl.when(pl.program_id(1) == 0)
    def _():
        xpad[:, 0:halo, :] = jnp.zeros((xpad.shape[0], halo, LANE), F32)
        s_scr[...] = jnp.zeros(s_scr.shape, F32)

    x_slabs = inner // LANE
    bc_slabs = b_ref.shape[-1] // LANE
    for j in range(xpad.shape[0]):
        cs = slice(j * LANE, (j + 1) * LANE)
        if j < x_slabs:
            src, sj = x_ref, j
        elif j < x_slabs + bc_slabs:
            src, sj = b_ref, j - x_slabs
        else:
            src, sj = c_ref, j - x_slabs - bc_slabs
        ss = slice(sj * LANE, (sj + 1) * LANE)
        xpad[j, halo:halo + q, :] = src[:, ss]
        acc = cb_ref[:, cs] + cw_ref[0:1, cs] * xpad[j, halo - CONV_K + 1:halo - CONV_K + 1 + q, :]
        for k in range(1, CONV_K):
            r0 = halo - CONV_K + 1 + k
            acc = acc + cw_ref[k:k + 1, cs] * xpad[j, r0:r0 + q, :]
        xc = _silu(acc)
        if src is x_ref:
            xs_scr[:, ss] = xc
        elif src is b_ref:
            bb_scr[:, ss] = xc.astype(BF16)
        else:
            cc_scr[:, ss] = xc.astype(BF16)
        xpad[j, 0:halo, :] = xpad[j, q:q + halo, :]

    dt_lanes = lax.broadcasted_iota(jnp.int32, (q, LANE), 1) < SSD_HEADS
    dt_raw = jnp.where(dt_lanes, _dot_nt(h_ref[...], wdt_ref[...].astype(BF16)), 0.0)
    dtr = dt_raw + dtb_ref[...]
    dtv = jnp.maximum(dtr, 0.0) + jnp.log1p(jnp.exp(-jnp.abs(dtr)))
    da = dtv * (-jnp.exp(alog_ref[...]))
    ri = lax.broadcasted_iota(jnp.int32, (q, q), 0)
    ci = lax.broadcasted_iota(jnp.int32, (q, q), 1)
    causal = ri >= ci
    tri = jnp.where(causal, 1.0, 0.0).astype(BF16)
    tri_t = jnp.where(ri <= ci, 1.0, 0.0).astype(BF16)
    acum = _dot_01_by_f32(tri, da)
    acum_row = _dot_f32_by_01(da.T, tri_t)

    first_lane_half = lax.broadcasted_iota(jnp.int32, (q, LANE), 1) < SSD_HEAD_DIM
    heads_per_group = SSD_HEADS // SSD_GROUPS
    for g in range(SSD_GROUPS):
        gs = slice(g * gw, (g + 1) * gw)
        e_g = e_ref[:, gs]
        acum_e = _dot_f32_by_01(acum, e_g)
        dt_e = _dot_f32_by_01(dtv, e_g)
        xh = xs_scr[:, gs] * dt_e
        xh_b = xh.astype(BF16)
        last_e = acum_e[q - 1:q, :]
        xdec = (xh * jnp.exp(last_e - acum_e)).astype(BF16)
        bg = bb_scr[:, g * n_state:(g + 1) * n_state]
        cg = cc_scr[:, g * n_state:(g + 1) * n_state]
        s_old = s_scr[g]
        y_off = _dot(cg, s_old.astype(BF16)) * jnp.exp(acum_e)
        s_scr[g] = jnp.exp(last_e) * s_old + _dot_tn(bg, xdec)
        cb = _dot_nt(cg, bg)
        for hh in range(heads_per_group // 2):
            xp = xh_b[:, hh * LANE:(hh + 1) * LANE]
            ys = []
            for e in range(2):
                h = g * heads_per_group + 2 * hh + e
                seg = acum[:, h:h + 1] - acum_row[h:h + 1, :]
                lmat = jnp.exp(jnp.where(causal, seg, -jnp.inf))
                ys.append(_dot((cb * lmat).astype(BF16), xp))
            y_pair = jnp.where(first_lane_half, ys[0], ys[1])
            c0 = g * gw + hh * LANE
            y_scr[:, c0:c0 + LANE] = y_pair + y_off[:, hh * LANE:(hh + 1) * LANE]

    y = y_scr[...] + dsk_ref[...] * xs_scr[...]
    gated = y * _silu(z_ref[...])
    ms = jnp.mean(gated * gated, axis=-1, keepdims=True)
    o_ref[...] = ((gated * lax.rsqrt(ms + RMS_EPS)) * ng_ref[...]).astype(o_ref.dtype)


def _ssd_branch(h3, w_in_t, layer, dt_col, proj3, conv_w, conv_b, dt_bias, a_log, d_skip, norm_g):
    b, seq, d = h3.shape
    q = SSD_CHUNK
    inner = SSD_HEADS * SSD_HEAD_DIM
    bc = SSD_GROUPS * SSD_STATE
    conv_ch = inner + 2 * bc
    pad = LANE - SSD_HEADS
    assert dt_col % LANE == 0
    expand = (jnp.arange(inner)[None, :] // SSD_HEAD_DIM == jnp.arange(LANE)[:, None]).astype(BF16)
    row = lambda v: v.reshape(1, -1)
    const = lambda shape: pl.BlockSpec(shape, lambda i, c: (0,) * len(shape))
    return pl.pallas_call(
        _ssd_kernel,
        grid=(b, seq // q),
        in_specs=[
            pl.BlockSpec((None, q, d), lambda i, c: (i, c, 0)),
            pl.BlockSpec((None, LANE, d), lambda i, c: (layer, dt_col // LANE, 0)),
            pl.BlockSpec((None, q, inner), lambda i, c: (i, c, 0)),
            pl.BlockSpec((None, q, inner), lambda i, c: (i, c, 1)),
            pl.BlockSpec((None, q, bc), lambda i, c: (i, c, 2 * inner // bc)),
            pl.BlockSpec((None, q, bc), lambda i, c: (i, c, 2 * inner // bc + 1)),
            const((CONV_K, conv_ch)), const((1, conv_ch)), const((1, LANE)), const((1, LANE)),
            const((1, inner)), const((1, inner)), const((LANE, inner)),
        ],
        out_specs=pl.BlockSpec((None, q, inner), lambda i, c: (i, c, 0)),
        out_shape=jax.ShapeDtypeStruct((b, seq, inner), BF16),
        scratch_shapes=[
            pltpu.VMEM((conv_ch // LANE, q + 8, LANE), F32),
            pltpu.VMEM((q, inner), F32),
            pltpu.VMEM((q, bc), BF16),
            pltpu.VMEM((q, bc), BF16),
            pltpu.VMEM((SSD_GROUPS, SSD_STATE, inner // SSD_GROUPS), F32),
            pltpu.VMEM((q, inner), F32),
        ],
        compiler_params=_params("parallel", "arbitrary"),
        name="ssd_branch",
    )(h3, w_in_t, proj3, proj3, proj3, proj3, conv_w, row(conv_b),
      row(jnp.pad(dt_bias, (0, pad))), row(jnp.pad(a_log, (0, pad))),
      row(jnp.repeat(d_skip, SSD_HEAD_DIM)), row(norm_g), expand)


def _pool_kernel(u_ref, halo_ref, z_ref, w_ref, sc_ref, o_ref, upad):
    tm = u_ref.shape[0]
    gw = w_ref.shape[-1]
    i = pl.program_id(1)
    upad[POOL_HALO:POOL_HALO + tm, :] = u_ref[...]

    @pl.when(i == 0)
    def _():
        upad[0:POOL_HALO, :] = jnp.zeros((POOL_HALO, upad.shape[1]), F32)

    @pl.when(i > 0)
    def _():
        upad[0:POOL_HALO, :] = halo_ref[...]

    pos = i * tm + lax.broadcasted_iota(jnp.int32, (tm, gw), 0)
    for gi, w in enumerate(POOL_WINDOWS):
        cs = slice(gi * gw, (gi + 1) * gw)
        a = upad[:, cs]
        s = a
        k = 1
        while k < w:
            s = s + pltpu.roll(s, k, axis=0)
            k *= 2
        count = jnp.minimum(pos + 1, w).astype(F32)
        pooled = s[POOL_HALO:, :] / count - a[POOL_HALO:, :]
        mixed = _dot(pooled.astype(BF16), w_ref[gi].astype(BF16))
        o_ref[:, cs] = (mixed * sc_ref[:, cs] * _silu(z_ref[:, cs])).astype(o_ref.dtype)


def _pool_branch(proj3, pool_w, layer, pool_scale, tm=512):
    b, seq, _ = proj3.shape
    _, ng, gw, _ = pool_w.shape
    width = ng * gw
    hb = tm // POOL_HALO
    return pl.pallas_call(
        _pool_kernel,
        grid=(b, seq // tm),
        in_specs=[
            pl.BlockSpec((None, tm, width), lambda bi, i: (bi, i, 0)),
            pl.BlockSpec((None, POOL_HALO, width), lambda bi, i: (bi, jnp.maximum(i * hb - 1, 0), 0)),
            pl.BlockSpec((None, tm, width), lambda bi, i: (bi, i, 1)),
            pl.BlockSpec((None, ng, gw, gw), lambda bi, i: (layer, 0, 0, 0)),
            pl.BlockSpec((1, width), lambda bi, i: (0, 0)),
        ],
        out_specs=pl.BlockSpec((None, tm, width), lambda bi, i: (bi, i, 0)),
        out_shape=jax.ShapeDtypeStruct((b, seq, width), BF16),
        scratch_shapes=[pltpu.VMEM((tm + POOL_HALO, width), F32)],
        compiler_params=_params("parallel", "arbitrary"),
        name="pool_branch",
    )(proj3, proj3, proj3, pool_w, pool_scale.reshape(1, width))


def _attn_kernel(q0, q1, q2, k0, k1, k2, v0, v1, v2, z_ref, o_ref,
                 op_scr, lp_scr, o0, o1, o2, l0, l1, l2):
    seq, hd = q0.shape
    blk = ATTN_BLOCK
    scale = hd ** -0.5
    qs, ks, vs = (q0, q1, q2), (k0, k1, k2), (v0, v1, v2)
    o_scrs, l_scrs = (o0, o1, o2), (l0, l1, l2)
    qi = lax.broadcasted_iota(jnp.int32, (blk, 2 * blk), 0)
    ci = lax.broadcasted_iota(jnp.int32, (blk, 2 * blk), 1)
    band = (ci >= qi) & (ci <= qi + blk)
    diag = (lax.broadcasted_iota(jnp.int32, (blk, blk), 1)
            <= lax.broadcasted_iota(jnp.int32, (blk, blk), 0))

    for g, (_, dil) in enumerate(ATTN_PATTERNS):
        ld = seq // dil
        nbs = ld // blk
        o_dst = o_scrs[g] if dil == 1 else op_scr
        l_dst = l_scrs[g] if dil == 1 else lp_scr
        for n in range(seq // blk):
            rows = slice(n * blk, (n + 1) * blk)
            if n % nbs == 0:
                krows, mask = rows, diag
            else:
                krows, mask = slice((n - 1) * blk, (n + 1) * blk), band
            s = jnp.where(mask, _dot_nt(qs[g][rows, :], ks[g][krows, :]) * scale, -jnp.inf)
            m = jnp.max(s, axis=-1, keepdims=True)
            e = jnp.exp(s - m)
            den = jnp.sum(e, axis=-1, keepdims=True)
            o_dst[rows, :] = _dot((e / den).astype(BF16), vs[g][krows, :])
            l_dst[rows, :] = jnp.broadcast_to(m + jnp.log(den), (blk, hd))
        if dil > 1:
            for r in range(dil):
                o_scrs[g][pl.ds(r, ld, stride=dil), :] = op_scr[r * ld:(r + 1) * ld, :]
                l_scrs[g][pl.ds(r, ld, stride=dil), :] = lp_scr[r * ld:(r + 1) * ld, :]

    la, lb, lc = l0[...], l1[...], l2[...]
    mx = jnp.maximum(jnp.maximum(la, lb), lc)
    wa, wb, wc = jnp.exp(la - mx), jnp.exp(lb - mx), jnp.exp(lc - mx)
    tot = wa + wb + wc
    comb = (wa / tot) * o0[...] + (wb / tot) * o1[...] + (wc / tot) * o2[...]
    o_ref[...] = (comb * _silu(z_ref[...])).astype(o_ref.dtype)


def _attn_branch(q3, k3, v3, z3):
    b, seq, _ = q3.shape
    hd = ATTN_HEAD_DIM
    hpg = ATTN_HEADS_PER_GROUP
    ngroups = len(ATTN_PATTERNS)

    def head_spec(g):
        return pl.BlockSpec((None, seq, hd), lambda bi, j: (bi, 0, g * hpg + j))

    in_specs = [head_spec(g) for _ in range(3) for g in range(ngroups)]
    in_specs.append(pl.BlockSpec((None, seq, hd), lambda bi, j: (bi, 0, j)))
    scratch = [pltpu.VMEM((seq, hd), F32)] * (2 + 2 * ngroups)
    return pl.pallas_call(
        _attn_kernel,
        grid=(b, hpg),
        in_specs=in_specs,
        out_specs=pl.BlockSpec((None, seq, hd), lambda bi, j: (bi, 0, j)),
        out_shape=jax.ShapeDtypeStruct((b, seq, hpg * hd), BF16),
        scratch_shapes=scratch,
        compiler_params=_params("parallel", "parallel"),
        name="attn_branch",
    )(*([q3] * ngroups + [k3] * ngroups + [v3] * ngroups), z3)


def _merge_kernel(ys_ref, yp_ref, ya_ref, w1_ref, w2_ref, w3_ref, g0_ref, g1_ref, g2_ref, o_ref,
                  wb1, wb2, wb3):
    @pl.when(pl.program_id(1) == 0)
    def _():
        _stage_weight(w1_ref, wb1)
        _stage_weight(w2_ref, wb2)
        _stage_weight(w3_ref, wb3)

    m = jax.nn.sigmoid(g0_ref[...]) * _dot(ys_ref[...], wb1[...])
    m = m + jax.nn.sigmoid(g1_ref[...]) * _dot(yp_ref[...], wb2[...])
    m = m + jax.nn.sigmoid(g2_ref[...]) * _dot(ya_ref[...], wb3[...])
    o_ref[...] = m.astype(o_ref.dtype)


def _merge(y_ssd, y_pool, y_attn, w1, w2, w3, layer, gates2, bm=512, bn=1024):
    t, d = y_ssd.shape
    n = w1.shape[-1]
    gate_blocks = n // bn
    once = pl.Buffered(1)

    def gate_spec(i):
        return pl.BlockSpec((bm, bn), lambda ni, mi: (mi, i * gate_blocks + ni))

    def w_spec(w):
        return pl.BlockSpec((None, w.shape[1], bn), lambda ni, mi: (layer, 0, ni), pipeline_mode=once)

    return pl.pallas_call(
        _merge_kernel,
        grid=(n // bn, t // bm),
        in_specs=[pl.BlockSpec((bm, d), lambda ni, mi: (mi, 0)),
                  pl.BlockSpec((bm, y_pool.shape[1]), lambda ni, mi: (mi, 0)),
                  pl.BlockSpec((bm, y_attn.shape[1]), lambda ni, mi: (mi, 0)),
                  w_spec(w1), w_spec(w2), w_spec(w3),
                  gate_spec(0), gate_spec(1), gate_spec(2)],
        out_specs=pl.BlockSpec((bm, bn), lambda ni, mi: (mi, ni)),
        out_shape=jax.ShapeDtypeStruct((t, n), BF16),
        scratch_shapes=[pltpu.VMEM((w1.shape[1], bn), BF16), pltpu.VMEM((w2.shape[1], bn), BF16),
                        pltpu.VMEM((w3.shape[1], bn), BF16)],
        compiler_params=_params("arbitrary", "arbitrary"),
        name="gated_merge",
    )(y_ssd, y_pool, y_attn, w1, w2, w3, gates2, gates2, gates2)


def _tail_kernel(*refs, layer, emit_h):
    if emit_h:
        (mg_ref, x_ref, p_ref, wo_hbm, wg_hbm, wp_ref, gple_ref, gnext_ref,
         xo_ref, ho_ref, wob, wgb, wpb, stage, sem) = refs
    else:
        (mg_ref, x_ref, p_ref, wo_hbm, wg_hbm, wp_ref, gple_ref,
         xo_ref, wob, wgb, wpb, stage, sem) = refs

    @pl.when(pl.program_id(0) == 0)
    def _():
        _stage_resident(lambda r0: wo_hbm.at[layer, pl.ds(r0, STAGE_ROWS), :], wob, stage, sem)
        _stage_resident(lambda r0: wg_hbm.at[layer, pl.ds(r0, STAGE_ROWS), :], wgb, stage, sem)
        wpb[...] = wp_ref[...].astype(BF16)

    def rms(v, g_ref):
        ms = jnp.mean(v * v, axis=-1, keepdims=True)
        return ((v * lax.rsqrt(ms + RMS_EPS)) * g_ref[...]).astype(BF16)

    x1 = x_ref[...] + _dot(mg_ref[...], wob[...])
    gate = jax.nn.sigmoid(_dot(rms(x1, gple_ref), wgb[...]))
    x2 = x1 + gate * _dot(p_ref[...].astype(BF16), wpb[...])
    xo_ref[...] = x2
    if emit_h:
        ho_ref[...] = rms(x2, gnext_ref)


def _layer_tail(merged, x2d, p, w_out, w_gate, w_proj, layer, ple_g, next_g, bm=256):
    t, d = x2d.shape
    kp = p.shape[-1]
    emit_h = next_g is not None
    once = pl.Buffered(1)
    row = lambda i: (i, 0)
    in_specs = [pl.BlockSpec((bm, d), row), pl.BlockSpec((bm, d), row),
                pl.BlockSpec((None, bm, kp), lambda i: (layer, i, 0)),
                pl.BlockSpec(memory_space=pl.ANY), pl.BlockSpec(memory_space=pl.ANY),
                pl.BlockSpec((None, kp, d), lambda i: (layer, 0, 0), pipeline_mode=once),
                pl.BlockSpec((1, d), lambda i: (0, 0))]
    args = [merged, x2d, p, w_out, w_gate, w_proj, ple_g.reshape(1, d)]
    out_specs = [pl.BlockSpec((bm, d), row)]
    out_shape = [jax.ShapeDtypeStruct((t, d), F32)]
    if emit_h:
        in_specs.append(pl.BlockSpec((1, d), lambda i: (0, 0)))
        args.append(next_g.reshape(1, d))
        out_specs.append(pl.BlockSpec((bm, d), row))
        out_shape.append(jax.ShapeDtypeStruct((t, d), BF16))
    outs = pl.pallas_call(
        functools.partial(_tail_kernel, layer=layer, emit_h=emit_h),
        grid=(t // bm,),
        in_specs=in_specs,
        out_specs=out_specs,
        out_shape=out_shape,
        scratch_shapes=_resident_scratch(d, d) + _resident_scratch(d, d) + _resident_scratch(kp, d)
                       + _stage_buffers(d),
        compiler_params=_params("arbitrary"),
        name="layer_tail",
    )(*args)
    return (outs[0], outs[1]) if emit_h else (outs[0], None)


def _in_proj_columns(d_model):
    inner = SSD_HEADS * SSD_HEAD_DIM
    bc = SSD_GROUPS * SSD_STATE
    attn_w = len(ATTN_PATTERNS) * ATTN_HEADS_PER_GROUP * ATTN_HEAD_DIM
    attn_out = ATTN_HEADS_PER_GROUP * ATTN_HEAD_DIM
    sizes = [("z_ssd", inner), ("x", inner), ("B", bc), ("C", bc), ("dt", SSD_HEADS),
             ("u_pool", d_model), ("z_pool", d_model), ("q", attn_w), ("k", attn_w), ("v", attn_w),
             ("z_attn", attn_out), ("gates", 3 * d_model)]
    col, start = {}, 0
    for name, size in sizes:
        col[name] = start
        start += size
    col["end"] = start
    return col


def kernel(x, p, norm_g, w_in, conv_w, conv_b, dt_bias, a_log, d_skip, ssd_norm_g, w_br_ssd, pool_w,
           pool_scale, w_br_pool, q_norm_g, k_norm_g, w_br_attn, w_out, ple_norm_g, w_ple_gate,
           w_ple_proj):
    b, seq, d = x.shape
    depth = w_in.shape[0]
    t = b * seq
    col = _in_proj_columns(d)
    assert col["end"] == w_in.shape[-1]
    x2 = x.reshape(t, d)
    p2 = p.reshape(depth, t, -1)
    w_in_t = jnp.swapaxes(w_in, 1, 2)
    h = _rms_norm_bf16(x2, norm_g[0])
    for i in range(depth):
        ssd_in = _in_proj(h, w_in_t, i, col["z_ssd"], col["dt"] - col["z_ssd"], 1024, 1024, "in_proj_ssd")
        pool_in = _in_proj(h, w_in_t, i, col["u_pool"], col["q"] - col["u_pool"], 1024, 1024, "in_proj_pool")
        q3 = _head_proj(h, w_in_t, i, col["q"], b, seq, q_norm_g[i], "q_proj")
        k3 = _head_proj(h, w_in_t, i, col["k"], b, seq, k_norm_g[i], "k_proj")
        v3 = _head_proj(h, w_in_t, i, col["v"], b, seq, None, "v_proj")
        z_attn = _in_proj(h, w_in_t, i, col["z_attn"], col["gates"] - col["z_attn"], 1024, 512, "in_proj_zattn")
        gates = _in_proj(h, w_in_t, i, col["gates"], col["end"] - col["gates"], 1024, 1024, "in_proj_gates")

        y_ssd = _ssd_branch(h.reshape(b, seq, d), w_in_t, i, col["dt"], ssd_in.reshape(b, seq, -1),
                            conv_w[i], conv_b[i], dt_bias[i], a_log[i], d_skip[i], ssd_norm_g[i])
        y_pool = _pool_branch(pool_in.reshape(b, seq, -1), pool_w, i, pool_scale[i])
        y_attn = _attn_branch(q3, k3, v3, z_attn.reshape(b, seq, -1))

        merged = _merge(y_ssd.reshape(t, -1), y_pool.reshape(t, -1), y_attn.reshape(t, -1),
                        w_br_ssd, w_br_pool, w_br_attn, i, gates)
        next_g = norm_g[i + 1] if i + 1 < depth else None
        x2, h = _layer_tail(merged, x2, p2, w_out, w_ple_gate, w_ple_proj, i, ple_norm_g[i], next_g)
    return x2.reshape(b, seq, d)
```

```python
import functools

import jax
import jax.numpy as jnp
from jax import lax
from jax.experimental import pallas as pl
from jax.experimental.pallas import tpu as pltpu

F32 = jnp.float32
BF16 = jnp.bfloat16

RMS_EPS = 1e-6
LOG2E = 1.4426950408889634
SSD_HEAD_DIM = 64
SSD_HEADS = 32
SSD_GROUPS = 4
SSD_STATE = 128
SSD_CHUNK = 256
CONV_K = 4
POOL_WINDOWS = (2, 4, 8, 16)
POOL_HALO = 16
ATTN_PATTERNS = ((128, 1), (512, 4), (2048, 16))
ATTN_HEAD_DIM = 128
ATTN_HEADS_PER_GROUP = 4
ATTN_BLOCK = 128

LANE = 128
VMEM_LIMIT = 56 * 1024 * 1024
STAGE_ROWS = 256


def _params(*sem):
    return pltpu.CompilerParams(dimension_semantics=sem, vmem_limit_bytes=VMEM_LIMIT)


def _sigmoid(v):
    return 0.5 * jnp.tanh(0.5 * v) + 0.5


def _silu(v):
    half = 0.5 * v
    return half * jnp.tanh(half) + half


def _dot(a, b):
    return jnp.dot(a, b, preferred_element_type=F32)


def _dot_nt(a, b):
    return lax.dot_general(a, b, (((1,), (1,)), ((), ())), preferred_element_type=F32)


def _dot_tn(a, b):
    return lax.dot_general(a, b, (((0,), (0,)), ((), ())), preferred_element_type=F32)


def _split3(v):
    hi = v.astype(BF16)
    r1 = v - hi.astype(F32)
    mid = r1.astype(BF16)
    lo = (r1 - mid.astype(F32)).astype(BF16)
    return hi, mid, lo


def _dot_f32_by_01(v, m01):
    hi, mid, lo = _split3(v)
    return _dot(hi, m01) + _dot(mid, m01) + _dot(lo, m01)


def _dot_f32_by_01_coarse(v, m01):
    hi = v.astype(BF16)
    mid = (v - hi.astype(F32)).astype(BF16)
    return _dot(hi, m01) + _dot(mid, m01)


def _dot_01_by_f32(m01, v):
    hi, mid, lo = _split3(v)
    return _dot(m01, hi) + _dot(m01, mid) + _dot(m01, lo)


def _stage_weight(w_ref, dst):
    rows_total = dst.shape[0]
    step = min(STAGE_ROWS, rows_total)

    def body(c, carry):
        rows = pl.ds(pl.multiple_of(c * step, step), step)
        dst[rows, :] = w_ref[rows, :].astype(BF16)
        return carry

    lax.fori_loop(0, rows_total // step, body, 0)


def _fetch_weight_rows(wt_hbm, wf, wb, sem, layer, row0):
    n = pl.program_id(0)
    n_tiles = pl.num_programs(0)
    bn = wb.shape[0]

    def copy(tile, slot):
        rows = pl.ds(pl.multiple_of(row0 + tile * bn, 8), bn)
        return pltpu.make_async_copy(wt_hbm.at[layer, rows, :], wf.at[slot], sem.at[slot])

    @pl.when(pl.program_id(1) == 0)
    def _():
        slot = n % 2

        @pl.when(n == 0)
        def _():
            copy(0, 0).start()

        copy(n, slot).wait()

        @pl.when(n + 1 < n_tiles)
        def _():
            copy(n + 1, 1 - slot).start()

        _stage_weight(wf.at[slot], wb)


def _weight_scratch(bn, k):
    return [pltpu.VMEM((2, bn, k), F32), pltpu.VMEM((bn, k), BF16), pltpu.SemaphoreType.DMA((2,))]


def _stage_resident(hbm_rows, wb, stage, sem):
    n_chunks = wb.shape[0] // STAGE_ROWS

    def copy(c):
        return pltpu.make_async_copy(hbm_rows(c * STAGE_ROWS), stage.at[c % 2], sem.at[c % 2])

    copy(0).start()
    for c in range(n_chunks):
        if c + 1 < n_chunks:
            copy(c + 1).start()
        copy(c).wait()
        wb[c * STAGE_ROWS:(c + 1) * STAGE_ROWS, :] = stage[c % 2].astype(BF16)


def _resident_scratch(rows, cols):
    return [pltpu.VMEM((rows, cols), BF16)]


def _stage_buffers(cols):
    return [pltpu.VMEM((2, STAGE_ROWS, cols), F32), pltpu.SemaphoreType.DMA((2,))]


def _rms_kernel(x_ref, g_ref, o_ref):
    x = x_ref[...]
    ms = jnp.mean(x * x, axis=-1, keepdims=True)
    o_ref[...] = ((x * lax.rsqrt(ms + RMS_EPS)) * g_ref[...]).astype(o_ref.dtype)


def _rms_norm_bf16(x2d, g, bm=512):
    t, d = x2d.shape
    return pl.pallas_call(
        _rms_kernel,
        grid=(t // bm,),
        in_specs=[pl.BlockSpec((bm, d), lambda i: (i, 0)),
                  pl.BlockSpec((1, d), lambda i: (0, 0))],
        out_specs=pl.BlockSpec((bm, d), lambda i: (i, 0)),
        out_shape=jax.ShapeDtypeStruct((t, d), BF16),
        compiler_params=_params("parallel"),
        name="rms_norm",
    )(x2d, g.reshape(1, d))


def _proj_kernel(a_ref, wt_hbm, o_ref, wf, wb, sem, *, layer, row0):
    _fetch_weight_rows(wt_hbm, wf, wb, sem, layer, row0)
    o_ref[...] = _dot_nt(a_ref[...], wb[...]).astype(o_ref.dtype)


def _in_proj(h, w_in_t, layer, col0, width, bm, bn, name):
    t, k = h.shape
    assert col0 % 8 == 0 and width % bn == 0
    return pl.pallas_call(
        functools.partial(_proj_kernel, layer=layer, row0=col0),
        grid=(width // bn, t // bm),
        in_specs=[pl.BlockSpec((bm, k), lambda n, m: (m, 0)),
                  pl.BlockSpec(memory_space=pl.ANY)],
        out_specs=pl.BlockSpec((bm, bn), lambda n, m: (m, n)),
        out_shape=jax.ShapeDtypeStruct((t, width), F32),
        scratch_shapes=_weight_scratch(bn, k),
        compiler_params=_params("arbitrary", "arbitrary"),
        name=name,
    )(h, w_in_t)


def _head_proj_kernel(*refs, layer, row0, seq, normalize):
    if normalize:
        a_ref, wt_hbm, g_ref, o_ref, wb, stage, sem, acc_scr = refs
    else:
        a_ref, wt_hbm, o_ref, wb, stage, sem, acc_scr = refs
    m = pl.program_id(0)
    bm = a_ref.shape[0]
    hd = ATTN_HEAD_DIM
    gw = ATTN_HEADS_PER_GROUP * hd
    half = m % (seq // bm)

    @pl.when(m == 0)
    def _():
        _stage_resident(lambda r0: wt_hbm.at[layer, pl.ds(row0 + r0, STAGE_ROWS), :], wb, stage, sem)

    a = a_ref[...]
    slab = 0
    for g, (_, dil) in enumerate(ATTN_PATTERNS):
        acc = _dot_nt(a, wb[g * gw:(g + 1) * gw, :])
        cnt = bm // dil
        ld = seq // dil
        for hh in range(ATTN_HEADS_PER_GROUP):
            cs = slice(g * gw + hh * hd, g * gw + (hh + 1) * hd)
            v = acc[:, hh * hd:(hh + 1) * hd]
            if normalize:
                ms = jnp.mean(v * v, axis=-1, keepdims=True)
                v = (v * lax.rsqrt(ms + RMS_EPS)) * g_ref[...]
            if dil == 1:
                o_ref[pl.ds(pl.multiple_of(half * bm, bm), bm), cs] = v.astype(BF16)
            else:
                acc_scr[slab] = v
                for r in range(dil):
                    rows = acc_scr[slab, pl.ds(r, cnt, stride=dil), :]
                    dst = r * ld + pl.multiple_of(half * cnt, cnt)
                    o_ref[pl.ds(dst, cnt), cs] = rows.astype(BF16)
                slab += 1


def _head_proj(h, w_in_t, layer, col0, b, seq, norm_g, name, bm=1024):
    t, k = h.shape
    hd = ATTN_HEAD_DIM
    width = len(ATTN_PATTERNS) * ATTN_HEADS_PER_GROUP * hd
    n_dilated = sum(ATTN_HEADS_PER_GROUP for _, dil in ATTN_PATTERNS if dil > 1)
    tiles_per_seq = seq // bm
    assert col0 % 8 == 0 and width % STAGE_ROWS == 0
    normalize = norm_g is not None
    in_specs = [pl.BlockSpec((bm, k), lambda m: (m, 0)), pl.BlockSpec(memory_space=pl.ANY)]
    args = [h, w_in_t]
    if normalize:
        in_specs.append(pl.BlockSpec((1, hd), lambda m: (0, 0)))
        args.append(norm_g.reshape(1, hd))
    return pl.pallas_call(
        functools.partial(_head_proj_kernel, layer=layer, row0=col0, seq=seq, normalize=normalize),
        grid=(t // bm,),
        in_specs=in_specs,
        out_specs=pl.BlockSpec((None, seq, width), lambda m: (m // tiles_per_seq, 0, 0)),
        out_shape=jax.ShapeDtypeStruct((b, seq, width), BF16),
        scratch_shapes=_resident_scratch(width, k) + _stage_buffers(k)
                       + [pltpu.VMEM((n_dilated, bm, hd), F32)],
        compiler_params=_params("arbitrary"),
        name=name,
    )(*args)


def _ssd_kernel(h_ref, wdt_ref, z_ref, x_ref, b_ref, c_ref, cw_ref, cb_ref, dtb_ref, alog_ref,
                dsk_ref, ng_ref, e_ref, o_ref, xpad, xs_scr, bb_scr, cc_scr, s_scr, y_scr):
    q = SSD_CHUNK
    inner = x_ref.shape[-1]
    gw = inner // SSD_GROUPS
    n_state = SSD_STATE
    halo = 8

    @pl.when(pl.program_id(1) == 0)
    def _():
        xpad[:, 0:halo, :] = jnp.zeros((xpad.shape[0], halo, LANE), F32)
        s_scr[...] = jnp.zeros(s_scr.shape, F32)

    x_slabs = inner // LANE
    bc_slabs = b_ref.shape[-1] // LANE
    for j in range(xpad.shape[0]):
        cs = slice(j * LANE, (j + 1) * LANE)
        if j < x_slabs:
            src, sj = x_ref, j
        elif j < x_slabs + bc_slabs:
            src, sj = b_ref, j - x_slabs
        else:
            src, sj = c_ref, j - x_slabs - bc_slabs
        ss = slice(sj * LANE, (sj + 1) * LANE)
        xpad[j, halo:halo + q, :] = src[:, ss]
        acc = cb_ref[:, cs] + cw_ref[0:1, cs] * xpad[j, halo - CONV_K + 1:halo - CONV_K + 1 + q, :]
        for k in range(1, CONV_K):
            r0 = halo - CONV_K + 1 + k
            acc = acc + cw_ref[k:k + 1, cs] * xpad[j, r0:r0 + q, :]
        xc = _silu(acc)
        if src is x_ref:
            xs_scr[:, ss] = xc
        elif src is b_ref:
            bb_scr[:, ss] = xc.astype(BF16)
        else:
            cc_scr[:, ss] = xc.astype(BF16)
        xpad[j, 0:halo, :] = xpad[j, q:q + halo, :]

    dt_lanes = lax.broadcasted_iota(jnp.int32, (q, LANE), 1) < SSD_HEADS
    dt_raw = jnp.where(dt_lanes, _dot_nt(h_ref[...], wdt_ref[...].astype(BF16)), 0.0)
    dtr = dt_raw + dtb_ref[...]
    dtv = jnp.maximum(dtr, 0.0) + jnp.log1p(jnp.exp(-jnp.abs(dtr)))
    da = dtv * (-jnp.exp(alog_ref[...]))
    ri = lax.broadcasted_iota(jnp.int32, (q, q), 0)
    ci = lax.broadcasted_iota(jnp.int32, (q, q), 1)
    causal = ri >= ci
    tri = jnp.where(causal, 1.0, 0.0).astype(BF16)
    tri_t = jnp.where(ri <= ci, 1.0, 0.0).astype(BF16)
    acum = _dot_01_by_f32(tri, da)
    acum2 = acum * LOG2E
    acum2_row = _dot_f32_by_01(da.T, tri_t) * LOG2E
    hq = q // 2
    causal_h = (lax.broadcasted_iota(jnp.int32, (hq, hq), 0)
                >= lax.broadcasted_iota(jnp.int32, (hq, hq), 1))

    first_lane_half = lax.broadcasted_iota(jnp.int32, (q, LANE), 1) < SSD_HEAD_DIM
    heads_per_group = SSD_HEADS // SSD_GROUPS
    for g in range(SSD_GROUPS):
        gs = slice(g * gw, (g + 1) * gw)
        e_g = e_ref[:, gs]
        acum_e = _dot_f32_by_01(acum, e_g)
        dt_e = _dot_f32_by_01_coarse(dtv, e_g)
        xh = xs_scr[:, gs] * dt_e
        xh_b = xh.astype(BF16)
        last_e = acum_e[q - 1:q, :]
        xdec = (xh * jnp.exp(last_e - acum_e)).astype(BF16)
        bg = bb_scr[:, g * n_state:(g + 1) * n_state]
        cg = cc_scr[:, g * n_state:(g + 1) * n_state]
        s_old = s_scr[g]
        y_off = _dot(cg, s_old.astype(BF16)) * jnp.exp(acum_e)
        s_scr[g] = jnp.exp(last_e) * s_old + _dot_tn(bg, xdec)
        cb = _dot_nt(cg, bg)
        cb00, cb10, cb11 = cb[:hq, :hq], cb[hq:, :hq], cb[hq:, hq:]
        for hh in range(heads_per_group // 2):
            xp = xh_b[:, hh * LANE:(hh + 1) * LANE]
            ys = []
            for e in range(2):
                h = g * heads_per_group + 2 * hh + e
                col = acum2[:, h:h + 1]
                row = acum2_row[h:h + 1, :]
                l00 = jnp.exp2(jnp.where(causal_h, col[:hq] - row[:, :hq], -jnp.inf))
                l10 = jnp.exp2(col[hq:] - row[:, :hq])
                l11 = jnp.exp2(jnp.where(causal_h, col[hq:] - row[:, hq:], -jnp.inf))
                top = _dot((cb00 * l00).astype(BF16), xp[:hq])
                bot = _dot(jnp.concatenate([cb10 * l10, cb11 * l11], axis=1).astype(BF16), xp)
                ys.append(jnp.concatenate([top, bot], axis=0))
            y_pair = jnp.where(first_lane_half, ys[0], ys[1])
            c0 = g * gw + hh * LANE
            y_scr[:, c0:c0 + LANE] = y_pair + y_off[:, hh * LANE:(hh + 1) * LANE]

    rb = 32
    for r0 in range(0, q, rb):
        rows = slice(r0, r0 + rb)
        y = y_scr[rows, :] + dsk_ref[...] * xs_scr[rows, :]
        gated = y * _silu(z_ref[rows, :])
        ms = jnp.mean(gated * gated, axis=-1, keepdims=True)
        o_ref[rows, :] = ((gated * lax.rsqrt(ms + RMS_EPS)) * ng_ref[...]).astype(o_ref.dtype)


def _ssd_branch(h3, w_in_t, layer, dt_col, proj3, conv_w, conv_b, dt_bias, a_log, d_skip, norm_g):
    b, seq, d = h3.shape
    q = SSD_CHUNK
    inner = SSD_HEADS * SSD_HEAD_DIM
    bc = SSD_GROUPS * SSD_STATE
    conv_ch = inner + 2 * bc
    pad = LANE - SSD_HEADS
    assert dt_col % LANE == 0
    expand = (jnp.arange(inner)[None, :] // SSD_HEAD_DIM == jnp.arange(LANE)[:, None]).astype(BF16)
    row = lambda v: v.reshape(1, -1)
    const = lambda shape: pl.BlockSpec(shape, lambda i, c: (0,) * len(shape))
    return pl.pallas_call(
        _ssd_kernel,
        grid=(b, seq // q),
        in_specs=[
            pl.BlockSpec((None, q, d), lambda i, c: (i, c, 0)),
            pl.BlockSpec((None, LANE, d), lambda i, c: (layer, dt_col // LANE, 0)),
            pl.BlockSpec((None, q, inner), lambda i, c: (i, c, 0)),
            pl.BlockSpec((None, q, inner), lambda i, c: (i, c, 1)),
            pl.BlockSpec((None, q, bc), lambda i, c: (i, c, 2 * inner // bc)),
            pl.BlockSpec((None, q, bc), lambda i, c: (i, c, 2 * inner // bc + 1)),
            const((CONV_K, conv_ch)), const((1, conv_ch)), const((1, LANE)), const((1, LANE)),
            const((1, inner)), const((1, inner)), const((LANE, inner)),
        ],
        out_specs=pl.BlockSpec((None, q, inner), lambda i, c: (i, c, 0)),
        out_shape=jax.ShapeDtypeStruct((b, seq, inner), BF16),
        scratch_shapes=[
            pltpu.VMEM((conv_ch // LANE, q + 8, LANE), F32),
            pltpu.VMEM((q, inner), F32),
            pltpu.VMEM((q, bc), BF16),
            pltpu.VMEM((q, bc), BF16),
            pltpu.VMEM((SSD_GROUPS, SSD_STATE, inner // SSD_GROUPS), F32),
            pltpu.VMEM((q, inner), F32),
        ],
        compiler_params=_params("parallel", "arbitrary"),
        name="ssd_branch",
    )(h3, w_in_t, proj3, proj3, proj3, proj3, conv_w, row(conv_b),
      row(jnp.pad(dt_bias, (0, pad))), row(jnp.pad(a_log, (0, pad))),
      row(jnp.repeat(d_skip, SSD_HEAD_DIM)), row(norm_g), expand)


def _pool_kernel(a_ref, wt_hbm, pw_ref, sc_ref, o_ref, wb, stage, sem, pwb, tail,
                 *, layer, u_row0, z_row0, seq):
    m = pl.program_id(0)
    bm = a_ref.shape[0]
    ng, gw, _ = pwb.shape
    tile_in_seq = m % (seq // bm)

    @pl.when(m == 0)
    def _():
        def hbm_rows(r0):
            g, part, off = r0 // (2 * gw), (r0 // gw) % 2, r0 % gw
            src = (z_row0 if part else u_row0) + g * gw + off
            return wt_hbm.at[layer, pl.ds(src, STAGE_ROWS), :]

        _stage_resident(hbm_rows, wb, stage, sem)
        for g in range(ng):
            pwb[g] = pw_ref[g].astype(BF16)

    a = a_ref[...]
    pos = tile_in_seq * bm + lax.broadcasted_iota(jnp.int32, (bm, gw), 0)
    for g, w in enumerate(POOL_WINDOWS):
        cs = slice(g * gw, (g + 1) * gw)
        acc = _dot_nt(a, wb[2 * g * gw:2 * (g + 1) * gw, :])
        u = acc[:, :gw]
        hist = jnp.where(tile_in_seq == 0, 0.0, tail[:, cs])
        s = jnp.concatenate([hist, u], axis=0)
        k = 1
        while k < w:
            s = s + pltpu.roll(s, k, axis=0)
            k *= 2
        count = jnp.minimum(pos + 1, w).astype(F32)
        pooled = s[POOL_HALO:, :] / count - u
        mixed = _dot(pooled.astype(BF16), pwb[g])
        o_ref[:, cs] = (mixed * sc_ref[:, cs] * _silu(acc[:, gw:])).astype(o_ref.dtype)
        tail[:, cs] = u[bm - POOL_HALO:, :]


def _pool_branch(h, w_in_t, layer, u_col, z_col, seq, pool_w, pool_scale, bm=1024):
    t, k = h.shape
    _, ng, gw, _ = pool_w.shape
    width = ng * gw
    assert u_col % 8 == 0 and z_col % 8 == 0 and gw % STAGE_ROWS == 0 and seq % bm == 0
    return pl.pallas_call(
        functools.partial(_pool_kernel, layer=layer, u_row0=u_col, z_row0=z_col, seq=seq),
        grid=(t // bm,),
        in_specs=[pl.BlockSpec((bm, k), lambda m: (m, 0)),
                  pl.BlockSpec(memory_space=pl.ANY),
                  pl.BlockSpec((None, ng, gw, gw), lambda m: (layer, 0, 0, 0),
                               pipeline_mode=pl.Buffered(1)),
                  pl.BlockSpec((1, width), lambda m: (0, 0))],
        out_specs=pl.BlockSpec((bm, width), lambda m: (m, 0)),
        out_shape=jax.ShapeDtypeStruct((t, width), BF16),
        scratch_shapes=_resident_scratch(2 * width, k) + _stage_buffers(k)
                       + [pltpu.VMEM((ng, gw, gw), BF16), pltpu.VMEM((POOL_HALO, width), F32)],
        compiler_params=_params("arbitrary"),
        name="pool_branch",
    )(h, w_in_t, pool_w, pool_scale.reshape(1, width))


def _attn_kernel(q0, q1, q2, k0, k1, k2, v0, v1, v2, z_ref, o_ref,
                 op_scr, lp_scr, o0, o1, o2, l0, l1, l2):
    seq, hd = q0.shape
    blk = ATTN_BLOCK
    scale = hd ** -0.5
    qs, ks, vs = (q0, q1, q2), (k0, k1, k2), (v0, v1, v2)
    o_scrs, l_scrs = (o0, o1, o2), (l0, l1, l2)
    qi = lax.broadcasted_iota(jnp.int32, (blk, 2 * blk), 0)
    ci = lax.broadcasted_iota(jnp.int32, (blk, 2 * blk), 1)
    band = (ci >= qi) & (ci <= qi + blk)
    diag = (lax.broadcasted_iota(jnp.int32, (blk, blk), 1)
            <= lax.broadcasted_iota(jnp.int32, (blk, blk), 0))

    for g, (_, dil) in enumerate(ATTN_PATTERNS):
        ld = seq // dil
        nbs = ld // blk
        o_dst = o_scrs[g] if dil == 1 else op_scr
        l_dst = l_scrs[g] if dil == 1 else lp_scr
        for n in range(seq // blk):
            rows = slice(n * blk, (n + 1) * blk)
            if n % nbs == 0:
                krows, mask = rows, diag
            else:
                krows, mask = slice((n - 1) * blk, (n + 1) * blk), band
            s = jnp.where(mask, _dot_nt(qs[g][rows, :], ks[g][krows, :]) * scale, -jnp.inf)
            m = jnp.max(s, axis=-1, keepdims=True)
            e = jnp.exp(s - m)
            den = jnp.sum(e, axis=-1, keepdims=True)
            o_dst[rows, :] = _dot((e / den).astype(BF16), vs[g][krows, :])
            l_dst[rows, :] = jnp.broadcast_to(m + jnp.log(den), (blk, hd))
        if dil > 1:
            for r in range(dil):
                o_scrs[g][pl.ds(r, ld, stride=dil), :] = op_scr[r * ld:(r + 1) * ld, :]
                l_scrs[g][pl.ds(r, ld, stride=dil), :] = lp_scr[r * ld:(r + 1) * ld, :]

    la, lb, lc = l0[...], l1[...], l2[...]
    mx = jnp.maximum(jnp.maximum(la, lb), lc)
    wa, wb, wc = jnp.exp(la - mx), jnp.exp(lb - mx), jnp.exp(lc - mx)
    tot = wa + wb + wc
    comb = (wa / tot) * o0[...] + (wb / tot) * o1[...] + (wc / tot) * o2[...]
    o_ref[...] = (comb * _silu(z_ref[...])).astype(o_ref.dtype)


def _attn_branch(q3, k3, v3, z3):
    b, seq, _ = q3.shape
    hd = ATTN_HEAD_DIM
    hpg = ATTN_HEADS_PER_GROUP
    ngroups = len(ATTN_PATTERNS)

    def head_spec(g):
        return pl.BlockSpec((None, seq, hd), lambda bi, j: (bi, 0, g * hpg + j))

    in_specs = [head_spec(g) for _ in range(3) for g in range(ngroups)]
    in_specs.append(pl.BlockSpec((None, seq, hd), lambda bi, j: (bi, 0, j)))
    scratch = [pltpu.VMEM((seq, hd), F32)] * (2 + 2 * ngroups)
    return pl.pallas_call(
        _attn_kernel,
        grid=(b, hpg),
        in_specs=in_specs,
        out_specs=pl.BlockSpec((None, seq, hd), lambda bi, j: (bi, 0, j)),
        out_shape=jax.ShapeDtypeStruct((b, seq, hpg * hd), BF16),
        scratch_shapes=scratch,
        compiler_params=_params("parallel", "parallel"),
        name="attn_branch",
    )(*([q3] * ngroups + [k3] * ngroups + [v3] * ngroups), z3)


def _merge_kernel(ys_ref, yp_ref, ya_ref, w1_ref, w2_ref, w3_ref, g0_ref, g1_ref, g2_ref, o_ref,
                  wb1, wb2, wb3):
    @pl.when(pl.program_id(1) == 0)
    def _():
        _stage_weight(w1_ref, wb1)
        _stage_weight(w2_ref, wb2)
        _stage_weight(w3_ref, wb3)

    m = _sigmoid(g0_ref[...]) * _dot(ys_ref[...], wb1[...])
    m = m + _sigmoid(g1_ref[...]) * _dot(yp_ref[...], wb2[...])
    m = m + _sigmoid(g2_ref[...]) * _dot(ya_ref[...], wb3[...])
    o_ref[...] = m.astype(o_ref.dtype)


def _merge(y_ssd, y_pool, y_attn, w1, w2, w3, layer, gates2, bm=512, bn=1024):
    t, d = y_ssd.shape
    n = w1.shape[-1]
    gate_blocks = n // bn
    once = pl.Buffered(1)

    def gate_spec(i):
        return pl.BlockSpec((bm, bn), lambda ni, mi: (mi, i * gate_blocks + ni))

    def w_spec(w):
        return pl.BlockSpec((None, w.shape[1], bn), lambda ni, mi: (layer, 0, ni), pipeline_mode=once)

    return pl.pallas_call(
        _merge_kernel,
        grid=(n // bn, t // bm),
        in_specs=[pl.BlockSpec((bm, d), lambda ni, mi: (mi, 0)),
                  pl.BlockSpec((bm, y_pool.shape[1]), lambda ni, mi: (mi, 0)),
                  pl.BlockSpec((bm, y_attn.shape[1]), lambda ni, mi: (mi, 0)),
                  w_spec(w1), w_spec(w2), w_spec(w3),
                  gate_spec(0), gate_spec(1), gate_spec(2)],
        out_specs=pl.BlockSpec((bm, bn), lambda ni, mi: (mi, ni)),
        out_shape=jax.ShapeDtypeStruct((t, n), BF16),
        scratch_shapes=[pltpu.VMEM((w1.shape[1], bn), BF16), pltpu.VMEM((w2.shape[1], bn), BF16),
                        pltpu.VMEM((w3.shape[1], bn), BF16)],
        compiler_params=_params("arbitrary", "arbitrary"),
        name="gated_merge",
    )(y_ssd, y_pool, y_attn, w1, w2, w3, gates2, gates2, gates2)


def _tail_kernel(*refs, layer, emit_h):
    if emit_h:
        (mg_ref, x_ref, p_ref, wo_hbm, wg_hbm, wp_ref, gple_ref, gnext_ref,
         xo_ref, ho_ref, wob, wgb, wpb, stage, sem) = refs
    else:
        (mg_ref, x_ref, p_ref, wo_hbm, wg_hbm, wp_ref, gple_ref,
         xo_ref, wob, wgb, wpb, stage, sem) = refs

    @pl.when(pl.program_id(0) == 0)
    def _():
        _stage_resident(lambda r0: wo_hbm.at[layer, pl.ds(r0, STAGE_ROWS), :], wob, stage, sem)
        _stage_resident(lambda r0: wg_hbm.at[layer, pl.ds(r0, STAGE_ROWS), :], wgb, stage, sem)
        wpb[...] = wp_ref[...].astype(BF16)

    def rms(v, g_ref):
        ms = jnp.mean(v * v, axis=-1, keepdims=True)
        return ((v * lax.rsqrt(ms + RMS_EPS)) * g_ref[...]).astype(BF16)

    x1 = x_ref[...] + _dot(mg_ref[...], wob[...])
    gate = _sigmoid(_dot(rms(x1, gple_ref), wgb[...]))
    x2 = x1 + gate * _dot(p_ref[...].astype(BF16), wpb[...])
    xo_ref[...] = x2
    if emit_h:
        ho_ref[...] = rms(x2, gnext_ref)


def _layer_tail(merged, x2d, p, w_out, w_gate, w_proj, layer, ple_g, next_g, bm=256):
    t, d = x2d.shape
    kp = p.shape[-1]
    emit_h = next_g is not None
    once = pl.Buffered(1)
    row = lambda i: (i, 0)
    in_specs = [pl.BlockSpec((bm, d), row), pl.BlockSpec((bm, d), row),
                pl.BlockSpec((None, bm, kp), lambda i: (layer, i, 0)),
                pl.BlockSpec(memory_space=pl.ANY), pl.BlockSpec(memory_space=pl.ANY),
                pl.BlockSpec((None, kp, d), lambda i: (layer, 0, 0), pipeline_mode=once),
                pl.BlockSpec((1, d), lambda i: (0, 0))]
    args = [merged, x2d, p, w_out, w_gate, w_proj, ple_g.reshape(1, d)]
    out_specs = [pl.BlockSpec((bm, d), row)]
    out_shape = [jax.ShapeDtypeStruct((t, d), F32)]
    if emit_h:
        in_specs.append(pl.BlockSpec((1, d), lambda i: (0, 0)))
        args.append(next_g.reshape(1, d))
        out_specs.append(pl.BlockSpec((bm, d), row))
        out_shape.append(jax.ShapeDtypeStruct((t, d), BF16))
    outs = pl.pallas_call(
        functools.partial(_tail_kernel, layer=layer, emit_h=emit_h),
        grid=(t // bm,),
        in_specs=in_specs,
        out_specs=out_specs,
        out_shape=out_shape,
        scratch_shapes=_resident_scratch(d, d) + _resident_scratch(d, d) + _resident_scratch(kp, d)
                       + _stage_buffers(d),
        compiler_params=_params("arbitrary"),
        name="layer_tail",
    )(*args)
    return (outs[0], outs[1]) if emit_h else (outs[0], None)


def _in_proj_columns(d_model):
    inner = SSD_HEADS * SSD_HEAD_DIM
    bc = SSD_GROUPS * SSD_STATE
    attn_w = len(ATTN_PATTERNS) * ATTN_HEADS_PER_GROUP * ATTN_HEAD_DIM
    attn_out = ATTN_HEADS_PER_GROUP * ATTN_HEAD_DIM
    sizes = [("z_ssd", inner), ("x", inner), ("B", bc), ("C", bc), ("dt", SSD_HEADS),
             ("u_pool", d_model), ("z_pool", d_model), ("q", attn_w), ("k", attn_w), ("v", attn_w),
             ("z_attn", attn_out), ("gates", 3 * d_model)]
    col, start = {}, 0
    for name, size in sizes:
        col[name] = start
        start += size
    col["end"] = start
    return col


def kernel(x, p, norm_g, w_in, conv_w, conv_b, dt_bias, a_log, d_skip, ssd_norm_g, w_br_ssd, pool_w,
           pool_scale, w_br_pool, q_norm_g, k_norm_g, w_br_attn, w_out, ple_norm_g, w_ple_gate,
           w_ple_proj):
    b, seq, d = x.shape
    depth = w_in.shape[0]
    t = b * seq
    col = _in_proj_columns(d)
    assert col["end"] == w_in.shape[-1]
    x2 = x.reshape(t, d)
    p2 = p.reshape(depth, t, -1)
    w_in_t = jnp.swapaxes(w_in, 1, 2)
    h = _rms_norm_bf16(x2, norm_g[0])
    for i in range(depth):
        ssd_in = _in_proj(h, w_in_t, i, col["z_ssd"], col["dt"] - col["z_ssd"], 1024, 1024, "in_proj_ssd")
        q3 = _head_proj(h, w_in_t, i, col["q"], b, seq, q_norm_g[i], "q_proj")
        k3 = _head_proj(h, w_in_t, i, col["k"], b, seq, k_norm_g[i], "k_proj")
        v3 = _head_proj(h, w_in_t, i, col["v"], b, seq, None, "v_proj")
        z_attn = _in_proj(h, w_in_t, i, col["z_attn"], col["gates"] - col["z_attn"], 1024, 512, "in_proj_zattn")
        gates = _in_proj(h, w_in_t, i, col["gates"], col["end"] - col["gates"], 1024, 1024, "in_proj_gates")

        y_ssd = _ssd_branch(h.reshape(b, seq, d), w_in_t, i, col["dt"], ssd_in.reshape(b, seq, -1),
                            conv_w[i], conv_b[i], dt_bias[i], a_log[i], d_skip[i], ssd_norm_g[i])
        y_pool = _pool_branch(h, w_in_t, i, col["u_pool"], col["z_pool"], seq, pool_w, pool_scale[i])
        y_attn = _attn_branch(q3, k3, v3, z_attn.reshape(b, seq, -1))

        merged = _merge(y_ssd.reshape(t, -1), y_pool.reshape(t, -1), y_attn.reshape(t, -1),
                        w_br_ssd, w_br_pool, w_br_attn, i, gates)
        next_g = norm_g[i + 1] if i + 1 < depth else None
        x2, h = _layer_tail(merged, x2, p2, w_out, w_ple_gate, w_ple_proj, i, ple_norm_g[i], next_g)
    return x2.reshape(b, seq, d)
```

```python
import functools

import jax
import jax.numpy as jnp
from jax import lax
from jax.experimental import pallas as pl
from jax.experimental.pallas import tpu as pltpu

F32 = jnp.float32
BF16 = jnp.bfloat16

RMS_EPS = 1e-6
LOG2E = 1.4426950408889634
SSD_HEAD_DIM = 64
SSD_HEADS = 32
SSD_GROUPS = 4
SSD_STATE = 128
SSD_CHUNK = 256
CONV_K = 4
POOL_WINDOWS = (2, 4, 8, 16)
POOL_HALO = 16
ATTN_PATTERNS = ((128, 1), (512, 4), (2048, 16))
ATTN_HEAD_DIM = 128
ATTN_HEADS_PER_GROUP = 4
ATTN_BLOCK = 128

LANE = 128
VMEM_LIMIT = 56 * 1024 * 1024
STAGE_ROWS = 256


def _params(*sem):
    return pltpu.CompilerParams(dimension_semantics=sem, vmem_limit_bytes=VMEM_LIMIT)


def _sigmoid(v):
    return 0.5 * jnp.tanh(0.5 * v) + 0.5


def _silu(v):
    half = 0.5 * v
    return half * jnp.tanh(half) + half


def _dot(a, b):
    return jnp.dot(a, b, preferred_element_type=F32)


def _dot_nt(a, b):
    return lax.dot_general(a, b, (((1,), (1,)), ((), ())), preferred_element_type=F32)


def _dot_tn(a, b):
    return lax.dot_general(a, b, (((0,), (0,)), ((), ())), preferred_element_type=F32)


def _split3(v):
    hi = v.astype(BF16)
    r1 = v - hi.astype(F32)
    mid = r1.astype(BF16)
    lo = (r1 - mid.astype(F32)).astype(BF16)
    return hi, mid, lo


def _dot_f32_by_01(v, m01):
    hi, mid, lo = _split3(v)
    return _dot(hi, m01) + _dot(mid, m01) + _dot(lo, m01)


def _dot_f32_by_01_coarse(v, m01):
    hi = v.astype(BF16)
    mid = (v - hi.astype(F32)).astype(BF16)
    return _dot(hi, m01) + _dot(mid, m01)


def _dot_01_by_f32(m01, v):
    hi, mid, lo = _split3(v)
    return _dot(m01, hi) + _dot(m01, mid) + _dot(m01, lo)


def _stage_weight(w_ref, dst):
    rows_total = dst.shape[0]
    step = min(STAGE_ROWS, rows_total)

    def body(c, carry):
        rows = pl.ds(pl.multiple_of(c * step, step), step)
        dst[rows, :] = w_ref[rows, :].astype(BF16)
        return carry

    lax.fori_loop(0, rows_total // step, body, 0)


def _fetch_weight_rows(wt_hbm, wf, wb, sem, layer, row0):
    n = pl.program_id(0)
    n_tiles = pl.num_programs(0)
    bn = wb.shape[0]

    def copy(tile, slot):
        rows = pl.ds(pl.multiple_of(row0 + tile * bn, 8), bn)
        return pltpu.make_async_copy(wt_hbm.at[layer, rows, :], wf.at[slot], sem.at[slot])

    @pl.when(pl.program_id(1) == 0)
    def _():
        slot = n % 2

        @pl.when(n == 0)
        def _():
            copy(0, 0).start()

        copy(n, slot).wait()

        @pl.when(n + 1 < n_tiles)
        def _():
            copy(n + 1, 1 - slot).start()

        _stage_weight(wf.at[slot], wb)


def _weight_scratch(bn, k):
    return [pltpu.VMEM((2, bn, k), F32), pltpu.VMEM((bn, k), BF16), pltpu.SemaphoreType.DMA((2,))]


def _stage_resident(hbm_rows, wb, stage, sem):
    chunk = stage.shape[1]
    n_chunks = wb.shape[0] // chunk

    def copy(c):
        return pltpu.make_async_copy(hbm_rows(c * chunk), stage.at[c % 2], sem.at[c % 2])

    copy(0).start()
    for c in range(n_chunks):
        if c + 1 < n_chunks:
            copy(c + 1).start()
        copy(c).wait()
        wb[c * chunk:(c + 1) * chunk, :] = stage[c % 2].astype(BF16)


def _resident_scratch(rows, cols):
    return [pltpu.VMEM((rows, cols), BF16)]


def _stage_buffers(cols, rows=STAGE_ROWS):
    return [pltpu.VMEM((2, rows, cols), F32), pltpu.SemaphoreType.DMA((2,))]


def _rms_kernel(x_ref, g_ref, o_ref):
    x = x_ref[...]
    ms = jnp.mean(x * x, axis=-1, keepdims=True)
    o_ref[...] = ((x * lax.rsqrt(ms + RMS_EPS)) * g_ref[...]).astype(o_ref.dtype)


def _rms_norm_bf16(x2d, g, bm=512):
    t, d = x2d.shape
    return pl.pallas_call(
        _rms_kernel,
        grid=(t // bm,),
        in_specs=[pl.BlockSpec((bm, d), lambda i: (i, 0)),
                  pl.BlockSpec((1, d), lambda i: (0, 0))],
        out_specs=pl.BlockSpec((bm, d), lambda i: (i, 0)),
        out_shape=jax.ShapeDtypeStruct((t, d), BF16),
        compiler_params=_params("parallel"),
        name="rms_norm",
    )(x2d, g.reshape(1, d))


def _proj_kernel(a_ref, wt_hbm, o_ref, wf, wb, sem, *, layer, row0):
    _fetch_weight_rows(wt_hbm, wf, wb, sem, layer, row0)
    o_ref[...] = _dot_nt(a_ref[...], wb[...]).astype(o_ref.dtype)


def _in_proj(h, w_in_t, layer, col0, width, bm, bn, name):
    t, k = h.shape
    assert col0 % 8 == 0 and width % bn == 0
    return pl.pallas_call(
        functools.partial(_proj_kernel, layer=layer, row0=col0),
        grid=(width // bn, t // bm),
        in_specs=[pl.BlockSpec((bm, k), lambda n, m: (m, 0)),
                  pl.BlockSpec(memory_space=pl.ANY)],
        out_specs=pl.BlockSpec((bm, bn), lambda n, m: (m, n)),
        out_shape=jax.ShapeDtypeStruct((t, width), F32),
        scratch_shapes=_weight_scratch(bn, k),
        compiler_params=_params("arbitrary", "arbitrary"),
        name=name,
    )(h, w_in_t)


def _head_proj_kernel(*refs, layer, row0, seq, normalize):
    if normalize:
        a_ref, wt_hbm, g_ref, o_ref, wb, stage, sem, acc_scr = refs
    else:
        a_ref, wt_hbm, o_ref, wb, stage, sem, acc_scr = refs
    m = pl.program_id(0)
    bm = a_ref.shape[0]
    hd = ATTN_HEAD_DIM
    gw = ATTN_HEADS_PER_GROUP * hd
    half = m % (seq // bm)

    @pl.when(m == 0)
    def _():
        _stage_resident(lambda r0: wt_hbm.at[layer, pl.ds(row0 + r0, STAGE_ROWS), :], wb, stage, sem)

    a = a_ref[...]
    slab = 0
    for g, (_, dil) in enumerate(ATTN_PATTERNS):
        acc = _dot_nt(a, wb[g * gw:(g + 1) * gw, :])
        cnt = bm // dil
        ld = seq // dil
        for hh in range(ATTN_HEADS_PER_GROUP):
            cs = slice(g * gw + hh * hd, g * gw + (hh + 1) * hd)
            v = acc[:, hh * hd:(hh + 1) * hd]
            if normalize:
                ms = jnp.mean(v * v, axis=-1, keepdims=True)
                v = (v * lax.rsqrt(ms + RMS_EPS)) * g_ref[...]
            if dil == 1:
                o_ref[pl.ds(pl.multiple_of(half * bm, bm), bm), cs] = v.astype(BF16)
            else:
                acc_scr[slab] = v
                for r in range(dil):
                    rows = acc_scr[slab, pl.ds(r, cnt, stride=dil), :]
                    dst = r * ld + pl.multiple_of(half * cnt, cnt)
                    o_ref[pl.ds(dst, cnt), cs] = rows.astype(BF16)
                slab += 1


def _head_proj(h, w_in_t, layer, col0, b, seq, norm_g, name, bm=1024):
    t, k = h.shape
    hd = ATTN_HEAD_DIM
    width = len(ATTN_PATTERNS) * ATTN_HEADS_PER_GROUP * hd
    n_dilated = sum(ATTN_HEADS_PER_GROUP for _, dil in ATTN_PATTERNS if dil > 1)
    tiles_per_seq = seq // bm
    assert col0 % 8 == 0 and width % STAGE_ROWS == 0
    normalize = norm_g is not None
    in_specs = [pl.BlockSpec((bm, k), lambda m: (m, 0)), pl.BlockSpec(memory_space=pl.ANY)]
    args = [h, w_in_t]
    if normalize:
        in_specs.append(pl.BlockSpec((1, hd), lambda m: (0, 0)))
        args.append(norm_g.reshape(1, hd))
    return pl.pallas_call(
        functools.partial(_head_proj_kernel, layer=layer, row0=col0, seq=seq, normalize=normalize),
        grid=(t // bm,),
        in_specs=in_specs,
        out_specs=pl.BlockSpec((None, seq, width), lambda m: (m // tiles_per_seq, 0, 0)),
        out_shape=jax.ShapeDtypeStruct((b, seq, width), BF16),
        scratch_shapes=_resident_scratch(width, k) + _stage_buffers(k)
                       + [pltpu.VMEM((n_dilated, bm, hd), F32)],
        compiler_params=_params("arbitrary"),
        name=name,
    )(*args)


def _ssd_kernel(h0_ref, hn_ref, wt_hbm, cw_ref, cb_ref, dtb_ref, alog_ref, dsk_ref, ng_ref, e_ref,
                o_ref, wb, stage, sem, proj, z_scr, xpad, xs_scr, bb_scr, cc_scr, s_scr, y_scr,
                *, layer, row0, chunks_per_seq):
    q = SSD_CHUNK
    inner = xs_scr.shape[-1]
    bc = bb_scr.shape[-1]
    gw = inner // SSD_GROUPS
    n_state = SSD_STATE
    halo = 8
    step = pl.program_id(0)

    @pl.when(step == 0)
    def _():
        chunk = stage.shape[1]
        _stage_resident(lambda r0: wt_hbm.at[layer, pl.ds(row0 + r0, chunk), :], wb, stage, sem)
        proj[...] = _dot_nt(h0_ref[...], wb[...])

    @pl.when(step % chunks_per_seq == 0)
    def _():
        xpad[:, 0:halo, :] = jnp.zeros((xpad.shape[0], halo, LANE), F32)
        s_scr[...] = jnp.zeros(s_scr.shape, F32)

    x_slabs = inner // LANE
    bc_slabs = bc // LANE
    for j in range(xpad.shape[0]):
        cs = slice(j * LANE, (j + 1) * LANE)
        xpad[j, halo:halo + q, :] = proj[:, inner + j * LANE:inner + (j + 1) * LANE]
        acc = cb_ref[:, cs] + cw_ref[0:1, cs] * xpad[j, halo - CONV_K + 1:halo - CONV_K + 1 + q, :]
        for k in range(1, CONV_K):
            r0 = halo - CONV_K + 1 + k
            acc = acc + cw_ref[k:k + 1, cs] * xpad[j, r0:r0 + q, :]
        xc = _silu(acc)
        if j < x_slabs:
            xs_scr[:, cs] = xc
        elif j < x_slabs + bc_slabs:
            bb_scr[:, (j - x_slabs) * LANE:(j - x_slabs + 1) * LANE] = xc.astype(BF16)
        else:
            jc = j - x_slabs - bc_slabs
            cc_scr[:, jc * LANE:(jc + 1) * LANE] = xc.astype(BF16)
        xpad[j, 0:halo, :] = xpad[j, q:q + halo, :]
    z_scr[...] = proj[:, 0:inner]

    dt_lanes = lax.broadcasted_iota(jnp.int32, (q, LANE), 1) < SSD_HEADS
    dt_col = 2 * inner + 2 * bc
    dt_raw = jnp.where(dt_lanes, proj[:, dt_col:dt_col + LANE], 0.0)

    proj[...] = _dot_nt(hn_ref[...], wb[...])

    dtr = dt_raw + dtb_ref[...]
    dtv = jnp.maximum(dtr, 0.0) + jnp.log1p(jnp.exp(-jnp.abs(dtr)))
    da = dtv * (-jnp.exp(alog_ref[...]))
    ri = lax.broadcasted_iota(jnp.int32, (q, q), 0)
    ci = lax.broadcasted_iota(jnp.int32, (q, q), 1)
    causal = ri >= ci
    tri = jnp.where(causal, 1.0, 0.0).astype(BF16)
    tri_t = jnp.where(ri <= ci, 1.0, 0.0).astype(BF16)
    acum = _dot_01_by_f32(tri, da)
    acum2 = acum * LOG2E
    acum2_row = _dot_f32_by_01(da.T, tri_t) * LOG2E
    hq = q // 2
    causal_h = (lax.broadcasted_iota(jnp.int32, (hq, hq), 0)
                >= lax.broadcasted_iota(jnp.int32, (hq, hq), 1))

    first_lane_half = lax.broadcasted_iota(jnp.int32, (q, LANE), 1) < SSD_HEAD_DIM
    heads_per_group = SSD_HEADS // SSD_GROUPS
    for g in range(SSD_GROUPS):
        gs = slice(g * gw, (g + 1) * gw)
        e_g = e_ref[:, gs]
        acum_e = _dot_f32_by_01(acum, e_g)
        dt_e = _dot_f32_by_01_coarse(dtv, e_g)
        xh = xs_scr[:, gs] * dt_e
        xh_b = xh.astype(BF16)
        last_e = acum_e[q - 1:q, :]
        xdec = (xh * jnp.exp(last_e - acum_e)).astype(BF16)
        bg = bb_scr[:, g * n_state:(g + 1) * n_state]
        cg = cc_scr[:, g * n_state:(g + 1) * n_state]
        s_old = s_scr[g]
        y_off = _dot(cg, s_old.astype(BF16)) * jnp.exp(acum_e)
        s_scr[g] = jnp.exp(last_e) * s_old + _dot_tn(bg, xdec)
        cb = _dot_nt(cg, bg)
        cb00, cb10, cb11 = cb[:hq, :hq], cb[hq:, :hq], cb[hq:, hq:]
        for hh in range(heads_per_group // 2):
            xp = xh_b[:, hh * LANE:(hh + 1) * LANE]
            ys = []
            for e in range(2):
                h = g * heads_per_group + 2 * hh + e
                col = acum2[:, h:h + 1]
                row = acum2_row[h:h + 1, :]
                l00 = jnp.exp2(jnp.where(causal_h, col[:hq] - row[:, :hq], -jnp.inf))
                l10 = jnp.exp2(col[hq:] - row[:, :hq])
                l11 = jnp.exp2(jnp.where(causal_h, col[hq:] - row[:, hq:], -jnp.inf))
                top = _dot((cb00 * l00).astype(BF16), xp[:hq])
                bot = _dot(jnp.concatenate([cb10 * l10, cb11 * l11], axis=1).astype(BF16), xp)
                ys.append(jnp.concatenate([top, bot], axis=0))
            y_pair = jnp.where(first_lane_half, ys[0], ys[1])
            c0 = g * gw + hh * LANE
            y_scr[:, c0:c0 + LANE] = y_pair + y_off[:, hh * LANE:(hh + 1) * LANE]

    rb = 32
    for r0 in range(0, q, rb):
        rows = slice(r0, r0 + rb)
        y = y_scr[rows, :] + dsk_ref[...] * xs_scr[rows, :]
        gated = y * _silu(z_scr[rows, :])
        ms = jnp.mean(gated * gated, axis=-1, keepdims=True)
        o_ref[rows, :] = ((gated * lax.rsqrt(ms + RMS_EPS)) * ng_ref[...]).astype(o_ref.dtype)


def _ssd_branch(h, w_in_t, layer, col0, seq, conv_w, conv_b, dt_bias, a_log, d_skip, norm_g):
    t, d = h.shape
    q = SSD_CHUNK
    inner = SSD_HEADS * SSD_HEAD_DIM
    bc = SSD_GROUPS * SSD_STATE
    conv_ch = inner + 2 * bc
    width = 2 * inner + 2 * bc + LANE
    pad = LANE - SSD_HEADS
    n_steps = t // q
    assert col0 % 8 == 0 and seq % q == 0
    expand = (jnp.arange(inner)[None, :] // SSD_HEAD_DIM == jnp.arange(LANE)[:, None]).astype(BF16)
    row = lambda v: v.reshape(1, -1)
    const = lambda shape: pl.BlockSpec(shape, lambda s: (0,) * len(shape))
    return pl.pallas_call(
        functools.partial(_ssd_kernel, layer=layer, row0=col0, chunks_per_seq=seq // q),
        grid=(n_steps,),
        in_specs=[
            pl.BlockSpec((q, d), lambda s: (0, 0)),
            pl.BlockSpec((q, d), lambda s: (jnp.minimum(s + 1, n_steps - 1), 0)),
            pl.BlockSpec(memory_space=pl.ANY),
            const((CONV_K, conv_ch)), const((1, conv_ch)), const((1, LANE)), const((1, LANE)),
            const((1, inner)), const((1, inner)), const((LANE, inner)),
        ],
        out_specs=pl.BlockSpec((q, inner), lambda s: (s, 0)),
        out_shape=jax.ShapeDtypeStruct((t, inner), BF16),
        scratch_shapes=_resident_scratch(width, d) + _stage_buffers(d, LANE) + [
            pltpu.VMEM((q, width), F32),
            pltpu.VMEM((q, inner), F32),
            pltpu.VMEM((conv_ch // LANE, q + 8, LANE), F32),
            pltpu.VMEM((q, inner), F32),
            pltpu.VMEM((q, bc), BF16),
            pltpu.VMEM((q, bc), BF16),
            pltpu.VMEM((SSD_GROUPS, SSD_STATE, inner // SSD_GROUPS), F32),
            pltpu.VMEM((q, inner), F32),
        ],
        compiler_params=_params("arbitrary"),
        name="ssd_branch",
    )(h, h, w_in_t, conv_w, row(conv_b),
      row(jnp.pad(dt_bias, (0, pad))), row(jnp.pad(a_log, (0, pad))),
      row(jnp.repeat(d_skip, SSD_HEAD_DIM)), row(norm_g), expand)


def _pool_kernel(a_ref, wt_hbm, pw_ref, sc_ref, o_ref, wb, stage, sem, pwb, tail,
                 *, layer, u_row0, z_row0, seq):
    m = pl.program_id(0)
    bm = a_ref.shape[0]
    ng, gw, _ = pwb.shape
    tile_in_seq = m % (seq // bm)

    @pl.when(m == 0)
    def _():
        def hbm_rows(r0):
            g, part, off = r0 // (2 * gw), (r0 // gw) % 2, r0 % gw
            src = (z_row0 if part else u_row0) + g * gw + off
            return wt_hbm.at[layer, pl.ds(src, STAGE_ROWS), :]

        _stage_resident(hbm_rows, wb, stage, sem)
        for g in range(ng):
            pwb[g] = pw_ref[g].astype(BF16)

    a = a_ref[...]
    pos = tile_in_seq * bm + lax.broadcasted_iota(jnp.int32, (bm, gw), 0)
    for g, w in enumerate(POOL_WINDOWS):
        cs = slice(g * gw, (g + 1) * gw)
        acc = _dot_nt(a, wb[2 * g * gw:2 * (g + 1) * gw, :])
        u = acc[:, :gw]
        hist = jnp.where(tile_in_seq == 0, 0.0, tail[:, cs])
        s = jnp.concatenate([hist, u], axis=0)
        k = 1
        while k < w:
            s = s + pltpu.roll(s, k, axis=0)
            k *= 2
        count = jnp.minimum(pos + 1, w).astype(F32)
        pooled = s[POOL_HALO:, :] / count - u
        mixed = _dot(pooled.astype(BF16), pwb[g])
        o_ref[:, cs] = (mixed * sc_ref[:, cs] * _silu(acc[:, gw:])).astype(o_ref.dtype)
        tail[:, cs] = u[bm - POOL_HALO:, :]


def _pool_branch(h, w_in_t, layer, u_col, z_col, seq, pool_w, pool_scale, bm=1024):
    t, k = h.shape
    _, ng, gw, _ = pool_w.shape
    width = ng * gw
    assert u_col % 8 == 0 and z_col % 8 == 0 and gw % STAGE_ROWS == 0 and seq % bm == 0
    return pl.pallas_call(
        functools.partial(_pool_kernel, layer=layer, u_row0=u_col, z_row0=z_col, seq=seq),
        grid=(t // bm,),
        in_specs=[pl.BlockSpec((bm, k), lambda m: (m, 0)),
                  pl.BlockSpec(memory_space=pl.ANY),
                  pl.BlockSpec((None, ng, gw, gw), lambda m: (layer, 0, 0, 0),
                               pipeline_mode=pl.Buffered(1)),
                  pl.BlockSpec((1, width), lambda m: (0, 0))],
        out_specs=pl.BlockSpec((bm, width), lambda m: (m, 0)),
        out_shape=jax.ShapeDtypeStruct((t, width), BF16),
        scratch_shapes=_resident_scratch(2 * width, k) + _stage_buffers(k)
                       + [pltpu.VMEM((ng, gw, gw), BF16), pltpu.VMEM((POOL_HALO, width), F32)],
        compiler_params=_params("arbitrary"),
        name="pool_branch",
    )(h, w_in_t, pool_w, pool_scale.reshape(1, width))


def _attn_kernel(q0, q1, q2, k0, k1, k2, v0, v1, v2, z_ref, o_ref,
                 op_scr, lp_scr, o0, o1, o2, l0, l1, l2):
    seq, hd = q0.shape
    blk = ATTN_BLOCK
    scale = hd ** -0.5
    qs, ks, vs = (q0, q1, q2), (k0, k1, k2), (v0, v1, v2)
    o_scrs, l_scrs = (o0, o1, o2), (l0, l1, l2)
    qi = lax.broadcasted_iota(jnp.int32, (blk, 2 * blk), 0)
    ci = lax.broadcasted_iota(jnp.int32, (blk, 2 * blk), 1)
    band = (ci >= qi) & (ci <= qi + blk)
    diag = (lax.broadcasted_iota(jnp.int32, (blk, blk), 1)
            <= lax.broadcasted_iota(jnp.int32, (blk, blk), 0))

    for g, (_, dil) in enumerate(ATTN_PATTERNS):
        ld = seq // dil
        nbs = ld // blk
        o_dst = o_scrs[g] if dil == 1 else op_scr
        l_dst = l_scrs[g] if dil == 1 else lp_scr
        for n in range(seq // blk):
            rows = slice(n * blk, (n + 1) * blk)
            if n % nbs == 0:
                krows, mask = rows, diag
            else:
                krows, mask = slice((n - 1) * blk, (n + 1) * blk), band
            s = jnp.where(mask, _dot_nt(qs[g][rows, :], ks[g][krows, :]) * scale, -jnp.inf)
            m = jnp.max(s, axis=-1, keepdims=True)
            e = jnp.exp(s - m)
            den = jnp.sum(e, axis=-1, keepdims=True)
            o_dst[rows, :] = _dot((e / den).astype(BF16), vs[g][krows, :])
            l_dst[rows, :] = jnp.broadcast_to(m + jnp.log(den), (blk, hd))
        if dil > 1:
            for r in range(dil):
                o_scrs[g][pl.ds(r, ld, stride=dil), :] = op_scr[r * ld:(r + 1) * ld, :]
                l_scrs[g][pl.ds(r, ld, stride=dil), :] = lp_scr[r * ld:(r + 1) * ld, :]

    la, lb, lc = l0[...], l1[...], l2[...]
    mx = jnp.maximum(jnp.maximum(la, lb), lc)
    wa, wb, wc = jnp.exp(la - mx), jnp.exp(lb - mx), jnp.exp(lc - mx)
    tot = wa + wb + wc
    comb = (wa / tot) * o0[...] + (wb / tot) * o1[...] + (wc / tot) * o2[...]
    o_ref[...] = (comb * _silu(z_ref[...])).astype(o_ref.dtype)


def _attn_branch(q3, k3, v3, z3):
    b, seq, _ = q3.shape
    hd = ATTN_HEAD_DIM
    hpg = ATTN_HEADS_PER_GROUP
    ngroups = len(ATTN_PATTERNS)

    def head_spec(g):
        return pl.BlockSpec((None, seq, hd), lambda bi, j: (bi, 0, g * hpg + j))

    in_specs = [head_spec(g) for _ in range(3) for g in range(ngroups)]
    in_specs.append(pl.BlockSpec((None, seq, hd), lambda bi, j: (bi, 0, j)))
    scratch = [pltpu.VMEM((seq, hd), F32)] * (2 + 2 * ngroups)
    return pl.pallas_call(
        _attn_kernel,
        grid=(b, hpg),
        in_specs=in_specs,
        out_specs=pl.BlockSpec((None, seq, hd), lambda bi, j: (bi, 0, j)),
        out_shape=jax.ShapeDtypeStruct((b, seq, hpg * hd), BF16),
        scratch_shapes=scratch,
        compiler_params=_params("parallel", "parallel"),
        name="attn_branch",
    )(*([q3] * ngroups + [k3] * ngroups + [v3] * ngroups), z3)


def _merge_kernel(ys_ref, yp_ref, ya_ref, w1_ref, w2_ref, w3_ref, g0_ref, g1_ref, g2_ref, o_ref,
                  wb1, wb2, wb3):
    @pl.when(pl.program_id(1) == 0)
    def _():
        _stage_weight(w1_ref, wb1)
        _stage_weight(w2_ref, wb2)
        _stage_weight(w3_ref, wb3)

    m = _sigmoid(g0_ref[...]) * _dot(ys_ref[...], wb1[...])
    m = m + _sigmoid(g1_ref[...]) * _dot(yp_ref[...], wb2[...])
    m = m + _sigmoid(g2_ref[...]) * _dot(ya_ref[...], wb3[...])
    o_ref[...] = m.astype(o_ref.dtype)


def _merge(y_ssd, y_pool, y_attn, w1, w2, w3, layer, gates2, bm=512, bn=1024):
    t, d = y_ssd.shape
    n = w1.shape[-1]
    gate_blocks = n // bn
    once = pl.Buffered(1)

    def gate_spec(i):
        return pl.BlockSpec((bm, bn), lambda ni, mi: (mi, i * gate_blocks + ni))

    def w_spec(w):
        return pl.BlockSpec((None, w.shape[1], bn), lambda ni, mi: (layer, 0, ni), pipeline_mode=once)

    return pl.pallas_call(
        _merge_kernel,
        grid=(n // bn, t // bm),
        in_specs=[pl.BlockSpec((bm, d), lambda ni, mi: (mi, 0)),
                  pl.BlockSpec((bm, y_pool.shape[1]), lambda ni, mi: (mi, 0)),
                  pl.BlockSpec((bm, y_attn.shape[1]), lambda ni, mi: (mi, 0)),
                  w_spec(w1), w_spec(w2), w_spec(w3),
                  gate_spec(0), gate_spec(1), gate_spec(2)],
        out_specs=pl.BlockSpec((bm, bn), lambda ni, mi: (mi, ni)),
        out_shape=jax.ShapeDtypeStruct((t, n), BF16),
        scratch_shapes=[pltpu.VMEM((w1.shape[1], bn), BF16), pltpu.VMEM((w2.shape[1], bn), BF16),
                        pltpu.VMEM((w3.shape[1], bn), BF16)],
        compiler_params=_params("arbitrary", "arbitrary"),
        name="gated_merge",
    )(y_ssd, y_pool, y_attn, w1, w2, w3, gates2, gates2, gates2)


def _tail_kernel(*refs, layer, emit_h):
    if emit_h:
        (mg_ref, x_ref, p_ref, wo_hbm, wg_hbm, wp_ref, gple_ref, gnext_ref,
         xo_ref, ho_ref, wob, wgb, wpb, stage, sem) = refs
    else:
        (mg_ref, x_ref, p_ref, wo_hbm, wg_hbm, wp_ref, gple_ref,
         xo_ref, wob, wgb, wpb, stage, sem) = refs

    @pl.when(pl.program_id(0) == 0)
    def _():
        _stage_resident(lambda r0: wo_hbm.at[layer, pl.ds(r0, STAGE_ROWS), :], wob, stage, sem)
        _stage_resident(lambda r0: wg_hbm.at[layer, pl.ds(r0, STAGE_ROWS), :], wgb, stage, sem)
        wpb[...] = wp_ref[...].astype(BF16)

    def rms(v, g_ref):
        ms = jnp.mean(v * v, axis=-1, keepdims=True)
        return ((v * lax.rsqrt(ms + RMS_EPS)) * g_ref[...]).astype(BF16)

    x1 = x_ref[...] + _dot(mg_ref[...], wob[...])
    gate = _sigmoid(_dot(rms(x1, gple_ref), wgb[...]))
    x2 = x1 + gate * _dot(p_ref[...].astype(BF16), wpb[...])
    xo_ref[...] = x2
    if emit_h:
        ho_ref[...] = rms(x2, gnext_ref)


def _layer_tail(merged, x2d, p, w_out, w_gate, w_proj, layer, ple_g, next_g, bm=256):
    t, d = x2d.shape
    kp = p.shape[-1]
    emit_h = next_g is not None
    once = pl.Buffered(1)
    row = lambda i: (i, 0)
    in_specs = [pl.BlockSpec((bm, d), row), pl.BlockSpec((bm, d), row),
                pl.BlockSpec((None, bm, kp), lambda i: (layer, i, 0)),
                pl.BlockSpec(memory_space=pl.ANY), pl.BlockSpec(memory_space=pl.ANY),
                pl.BlockSpec((None, kp, d), lambda i: (layer, 0, 0), pipeline_mode=once),
                pl.BlockSpec((1, d), lambda i: (0, 0))]
    args = [merged, x2d, p, w_out, w_gate, w_proj, ple_g.reshape(1, d)]
    out_specs = [pl.BlockSpec((bm, d), row)]
    out_shape = [jax.ShapeDtypeStruct((t, d), F32)]
    if emit_h:
        in_specs.append(pl.BlockSpec((1, d), lambda i: (0, 0)))
        args.append(next_g.reshape(1, d))
        out_specs.append(pl.BlockSpec((bm, d), row))
        out_shape.append(jax.ShapeDtypeStruct((t, d), BF16))
    outs = pl.pallas_call(
        functools.partial(_tail_kernel, layer=layer, emit_h=emit_h),
        grid=(t // bm,),
        in_specs=in_specs,
        out_specs=out_specs,
        out_shape=out_shape,
        scratch_shapes=_resident_scratch(d, d) + _resident_scratch(d, d) + _resident_scratch(kp, d)
                       + _stage_buffers(d),
        compiler_params=_params("arbitrary"),
        name="layer_tail",
    )(*args)
    return (outs[0], outs[1]) if emit_h else (outs[0], None)


def _in_proj_columns(d_model):
    inner = SSD_HEADS * SSD_HEAD_DIM
    bc = SSD_GROUPS * SSD_STATE
    attn_w = len(ATTN_PATTERNS) * ATTN_HEADS_PER_GROUP * ATTN_HEAD_DIM
    attn_out = ATTN_HEADS_PER_GROUP * ATTN_HEAD_DIM
    sizes = [("z_ssd", inner), ("x", inner), ("B", bc), ("C", bc), ("dt", SSD_HEADS),
             ("u_pool", d_model), ("z_pool", d_model), ("q", attn_w), ("k", attn_w), ("v", attn_w),
             ("z_attn", attn_out), ("gates", 3 * d_model)]
    col, start = {}, 0
    for name, size in sizes:
        col[name] = start
        start += size
    col["end"] = start
    return col


def kernel(x, p, norm_g, w_in, conv_w, conv_b, dt_bias, a_log, d_skip, ssd_norm_g, w_br_ssd, pool_w,
           pool_scale, w_br_pool, q_norm_g, k_norm_g, w_br_attn, w_out, ple_norm_g, w_ple_gate,
           w_ple_proj):
    b, seq, d = x.shape
    depth = w_in.shape[0]
    t = b * seq
    col = _in_proj_columns(d)
    assert col["end"] == w_in.shape[-1]
    x2 = x.reshape(t, d)
    p2 = p.reshape(depth, t, -1)
    w_in_t = jnp.swapaxes(w_in, 1, 2)
    h = _rms_norm_bf16(x2, norm_g[0])
    for i in range(depth):
        q3 =_head_proj(h, w_in_t, i, col["q"], b, seq, q_norm_g[i], "q_proj")
        k3 = _head_proj(h, w_in_t, i, col["k"], b, seq, k_norm_g[i], "k_proj")
        v3 = _head_proj(h, w_in_t, i, col["v"], b, seq, None, "v_proj")
        z_attn = _in_proj(h, w_in_t, i, col["z_attn"], col["gates"] - col["z_attn"], 1024, 512, "in_proj_zattn")
        gates = _in_proj(h, w_in_t, i, col["gates"], col["end"] - col["gates"], 1024, 1024, "in_proj_gates")

        assert col["dt"] - col["z_ssd"] == 2 * SSD_HEADS * SSD_HEAD_DIM + 2 * SSD_GROUPS * SSD_STATE
        y_ssd = _ssd_branch(h, w_in_t, i, col["z_ssd"], seq, conv_w[i], conv_b[i], dt_bias[i],
                            a_log[i], d_skip[i], ssd_norm_g[i])
        y_pool = _pool_branch(h, w_in_t, i, col["u_pool"], col["z_pool"], seq, pool_w, pool_scale[i])
        y_attn = _attn_branch(q3, k3, v3, z_attn.reshape(b, seq, -1))

        merged = _merge(y_ssd.reshape(t, -1), y_pool.reshape(t, -1), y_attn.reshape(t, -1),
                        w_br_ssd, w_br_pool, w_br_attn, i, gates)
        next_g = norm_g[i + 1] if i + 1 < depth else None
        x2, h = _layer_tail(merged, x2, p2, w_out, w_ple_gate, w_ple_proj, i, ple_norm_g[i], next_g)
    return x2.reshape(b, seq, d)
```

```python
import functools

import jax
import jax.numpy as jnp
from jax import lax
from jax.experimental import pallas as pl
from jax.experimental.pallas import tpu as pltpu

F32 = jnp.float32
BF16 = jnp.bfloat16

RMS_EPS = 1e-6
LOG2E = 1.4426950408889634
SSD_HEAD_DIM = 64
SSD_HEADS = 32
SSD_GROUPS = 4
SSD_STATE = 128
SSD_CHUNK = 256
CONV_K = 4
POOL_WINDOWS = (2, 4, 8, 16)
POOL_HALO = 16
ATTN_PATTERNS = ((128, 1), (512, 4), (2048, 16))
ATTN_HEAD_DIM = 128
ATTN_HEADS_PER_GROUP = 4
ATTN_BLOCK = 128

LANE = 128
VMEM_LIMIT = 56 * 1024 * 1024
VMEM_LIMIT_HIGH = 58 * 1024 * 1024
STAGE_ROWS = 256


def _params(*sem, vmem=VMEM_LIMIT):
    return pltpu.CompilerParams(dimension_semantics=sem, vmem_limit_bytes=vmem)


def _sigmoid(v):
    return 0.5 * jnp.tanh(0.5 * v) + 0.5


def _silu(v):
    half = 0.5 * v
    return half * jnp.tanh(half) + half


def _dot(a, b):
    return jnp.dot(a, b, preferred_element_type=F32)


def _dot_nt(a, b):
    return lax.dot_general(a, b, (((1,), (1,)), ((), ())), preferred_element_type=F32)


def _dot_tn(a, b):
    return lax.dot_general(a, b, (((0,), (0,)), ((), ())), preferred_element_type=F32)


def _pack_pieces(v, width, n_pieces):
    packed = None
    rest = v
    for k in range(n_pieces):
        piece = rest.astype(BF16).astype(F32)
        rest = rest - piece
        shifted = piece if k == 0 else pltpu.roll(piece, k * width, axis=1)
        packed = shifted if packed is None else packed + shifted
    return packed


def _stage_weight(w_ref, dst):
    rows_total = dst.shape[0]
    step = min(STAGE_ROWS, rows_total)

    def body(c, carry):
        rows = pl.ds(pl.multiple_of(c * step, step), step)
        dst[rows, :] = w_ref[rows, :].astype(BF16)
        return carry

    lax.fori_loop(0, rows_total // step, body, 0)


def _fetch_weight_rows(wt_hbm, wf, wb, sem, layer, row0):
    n = pl.program_id(0)
    n_tiles = pl.num_programs(0)
    bn = wb.shape[0]

    def copy(tile, slot):
        rows = pl.ds(pl.multiple_of(row0 + tile * bn, 8), bn)
        return pltpu.make_async_copy(wt_hbm.at[layer, rows, :], wf.at[slot], sem.at[slot])

    @pl.when(pl.program_id(1) == 0)
    def _():
        slot = n % 2

        @pl.when(n == 0)
        def _():
            copy(0, 0).start()

        copy(n, slot).wait()

        @pl.when(n + 1 < n_tiles)
        def _():
            copy(n + 1, 1 - slot).start()

        _stage_weight(wf.at[slot], wb)


def _weight_scratch(bn, k):
    return [pltpu.VMEM((2, bn, k), F32), pltpu.VMEM((bn, k), BF16), pltpu.SemaphoreType.DMA((2,))]


def _stage_resident(hbm_rows, wb, stage, sem):
    chunk = stage.shape[1]
    n_chunks = wb.shape[0] // chunk

    def copy(c):
        return pltpu.make_async_copy(hbm_rows(c * chunk), stage.at[c % 2], sem.at[c % 2])

    copy(0).start()
    for c in range(n_chunks):
        if c + 1 < n_chunks:
            copy(c + 1).start()
        copy(c).wait()
        wb[c * chunk:(c + 1) * chunk, :] = stage[c % 2].astype(BF16)


def _resident_scratch(rows, cols):
    return [pltpu.VMEM((rows, cols), BF16)]


def _stage_buffers(cols, rows=STAGE_ROWS):
    return [pltpu.VMEM((2, rows, cols), F32), pltpu.SemaphoreType.DMA((2,))]


def _rms_kernel(x_ref, g_ref, o_ref):
    x = x_ref[...]
    ms = jnp.mean(x * x, axis=-1, keepdims=True)
    o_ref[...] = ((x * lax.rsqrt(ms + RMS_EPS)) * g_ref[...]).astype(o_ref.dtype)


def _rms_norm_bf16(x2d, g, bm=512):
    t, d = x2d.shape
    return pl.pallas_call(
        _rms_kernel,
        grid=(t // bm,),
        in_specs=[pl.BlockSpec((bm, d), lambda i: (i, 0)),
                  pl.BlockSpec((1, d), lambda i: (0, 0))],
        out_specs=pl.BlockSpec((bm, d), lambda i: (i, 0)),
        out_shape=jax.ShapeDtypeStruct((t, d), BF16),
        compiler_params=_params("parallel"),
        name="rms_norm",
    )(x2d, g.reshape(1, d))


def _proj_kernel(a_ref, wt_hbm, o_ref, wf, wb, sem, *, layer, row0):
    _fetch_weight_rows(wt_hbm, wf, wb, sem, layer, row0)
    o_ref[...] = _dot_nt(a_ref[...], wb[...]).astype(o_ref.dtype)


def _in_proj(h, w_in_t, layer, col0, width, bm, bn, name):
    t, k = h.shape
    assert col0 % 8 == 0 and width % bn == 0
    return pl.pallas_call(
        functools.partial(_proj_kernel, layer=layer, row0=col0),
        grid=(width // bn, t // bm),
        in_specs=[pl.BlockSpec((bm, k), lambda n, m: (m, 0)),
                  pl.BlockSpec(memory_space=pl.ANY)],
        out_specs=pl.BlockSpec((bm, bn), lambda n, m: (m, n)),
        out_shape=jax.ShapeDtypeStruct((t, width), F32),
        scratch_shapes=_weight_scratch(bn, k),
        compiler_params=_params("arbitrary", "arbitrary"),
        name=name,
    )(h, w_in_t)


def _head_proj_kernel(*refs, layer, row0, seq, normalize):
    if normalize:
        a_ref, wt_hbm, g_ref, o_ref, wb, stage, sem, acc_scr = refs
    else:
        a_ref, wt_hbm, o_ref, wb, stage, sem, acc_scr = refs
    m = pl.program_id(0)
    bm = a_ref.shape[0]
    hd = ATTN_HEAD_DIM
    gw = ATTN_HEADS_PER_GROUP * hd
    half = m % (seq // bm)

    @pl.when(m == 0)
    def _():
        _stage_resident(lambda r0: wt_hbm.at[layer, pl.ds(row0 + r0, STAGE_ROWS), :], wb, stage, sem)

    a = a_ref[...]
    slab = 0
    for g, (_, dil) in enumerate(ATTN_PATTERNS):
        acc = _dot_nt(a, wb[g * gw:(g + 1) * gw, :])
        cnt = bm // dil
        ld = seq // dil
        for hh in range(ATTN_HEADS_PER_GROUP):
            cs = slice(g * gw + hh * hd, g * gw + (hh + 1) * hd)
            v = acc[:, hh * hd:(hh + 1) * hd]
            if normalize:
                ms = jnp.mean(v * v, axis=-1, keepdims=True)
                v = (v * lax.rsqrt(ms + RMS_EPS)) * g_ref[...]
            if dil == 1:
                o_ref[pl.ds(pl.multiple_of(half * bm, bm), bm), cs] = v.astype(BF16)
            else:
                acc_scr[slab] = v
                for r in range(dil):
                    rows = acc_scr[slab, pl.ds(r, cnt, stride=dil), :]
                    dst = r * ld + pl.multiple_of(half * cnt, cnt)
                    o_ref[pl.ds(dst, cnt), cs] = rows.astype(BF16)
                slab += 1


def _head_proj(h, w_in_t, layer, col0, b, seq, norm_g, name, bm=1024):
    t, k = h.shape
    hd = ATTN_HEAD_DIM
    width = len(ATTN_PATTERNS) * ATTN_HEADS_PER_GROUP * hd
    n_dilated = sum(ATTN_HEADS_PER_GROUP for _, dil in ATTN_PATTERNS if dil > 1)
    tiles_per_seq = seq // bm
    assert col0 % 8 == 0 and width % STAGE_ROWS == 0
    normalize = norm_g is not None
    in_specs = [pl.BlockSpec((bm, k), lambda m: (m, 0)), pl.BlockSpec(memory_space=pl.ANY)]
    args = [h, w_in_t]
    if normalize:
        in_specs.append(pl.BlockSpec((1, hd), lambda m: (0, 0)))
        args.append(norm_g.reshape(1, hd))
    return pl.pallas_call(
        functools.partial(_head_proj_kernel, layer=layer, row0=col0, seq=seq, normalize=normalize),
        grid=(t // bm,),
        in_specs=in_specs,
        out_specs=pl.BlockSpec((None, seq, width), lambda m: (m // tiles_per_seq, 0, 0)),
        out_shape=jax.ShapeDtypeStruct((b, seq, width), BF16),
        scratch_shapes=_resident_scratch(width, k) + _stage_buffers(k)
                       + [pltpu.VMEM((n_dilated, bm, hd), F32)],
        compiler_params=_params("arbitrary"),
        name=name,
    )(*args)


def _ssd_kernel(h0_ref, hn_ref, wt_hbm, cw_ref, cb_ref, dtb_ref, alog_ref, dsk_ref, ng_ref, e_ref,
                o_ref, wb, stage, sem, proj, z_scr, xpad, xs_scr, bb_scr, cc_scr, s_scr, y_scr,
                *, layer, row0, chunks_per_seq):
    q = SSD_CHUNK
    inner = xs_scr.shape[-1]
    bc = bb_scr.shape[-1]
    gw = inner // SSD_GROUPS
    n_state = SSD_STATE
    halo = 8
    step = pl.program_id(0)

    @pl.when(step == 0)
    def _():
        chunk = stage.shape[1]
        _stage_resident(lambda r0: wt_hbm.at[layer, pl.ds(row0 + r0, chunk), :], wb, stage, sem)
        proj[...] = _dot_nt(h0_ref[...], wb[...])

    @pl.when(step % chunks_per_seq == 0)
    def _():
        xpad[:, 0:halo, :] = jnp.zeros((xpad.shape[0], halo, LANE), F32)
        s_scr[...] = jnp.zeros(s_scr.shape, F32)

    x_slabs = inner // LANE
    bc_slabs = bc // LANE
    for j in range(xpad.shape[0]):
        cs = slice(j * LANE, (j + 1) * LANE)
        xpad[j, halo:halo + q, :] = proj[:, inner + j * LANE:inner + (j + 1) * LANE]
        acc = cb_ref[:, cs] + cw_ref[0:1, cs] * xpad[j, halo - CONV_K + 1:halo - CONV_K + 1 + q, :]
        for k in range(1, CONV_K):
            r0 = halo - CONV_K + 1 + k
            acc = acc + cw_ref[k:k + 1, cs] * xpad[j, r0:r0 + q, :]
        xc = _silu(acc)
        if j < x_slabs:
            xs_scr[:, cs] = xc
        elif j < x_slabs + bc_slabs:
            bb_scr[:, (j - x_slabs) * LANE:(j - x_slabs + 1) * LANE] = xc.astype(BF16)
        else:
            jc = j - x_slabs - bc_slabs
            cc_scr[:, jc * LANE:(jc + 1) * LANE] = xc.astype(BF16)
        xpad[j, 0:halo, :] = xpad[j, q:q + halo, :]
    z_scr[...] = proj[:, 0:inner]

    dt_lanes = lax.broadcasted_iota(jnp.int32, (q, LANE), 1) < SSD_HEADS
    dt_col = 2 * inner + 2 * bc
    dt_raw = jnp.where(dt_lanes, proj[:, dt_col:dt_col + LANE], 0.0)

    h_next = hn_ref[...]
    piece = 2 * LANE
    pieces = [(c0, min(c0 + piece, proj.shape[1])) for c0 in range(0, proj.shape[1], piece)]

    def project_next(count):
        for _ in range(min(count, len(pieces))):
            c0, c1 = pieces.pop(0)
            proj[:, c0:c1] = _dot_nt(h_next, wb[c0:c1, :])

    dtr = dt_raw + dtb_ref[...]
    dtv = jnp.where(dt_lanes, jnp.maximum(dtr, 0.0) + jnp.log1p(jnp.exp(-jnp.abs(dtr))), 0.0)
    da = dtv * (-jnp.exp(alog_ref[...]))
    ri = lax.broadcasted_iota(jnp.int32, (q, q), 0)
    ci = lax.broadcasted_iota(jnp.int32, (q, q), 1)
    tri = jnp.where(ri >= ci, 1.0, 0.0).astype(BF16)
    tri_t = jnp.where(ri <= ci, 1.0, 0.0).astype(BF16)
    nh = SSD_HEADS
    da3 = _pack_pieces(da, nh, 3)
    cum3 = _dot(tri, da3.astype(BF16))
    acum = jnp.where(dt_lanes, cum3 + pltpu.roll(cum3, LANE - nh, axis=1)
                     + pltpu.roll(cum3, LANE - 2 * nh, axis=1), 0.0)
    acum2 = acum * LOG2E
    cum3_row = _dot(da3.T.astype(BF16), tri_t)
    acum2_row = (cum3_row[0:nh] + cum3_row[nh:2 * nh] + cum3_row[2 * nh:3 * nh]) * LOG2E
    acum_p = _pack_pieces(acum, nh, 3).astype(BF16)
    dtv_p = _pack_pieces(dtv, nh, 2).astype(BF16)
    hq = q // 2
    causal_h = (lax.broadcasted_iota(jnp.int32, (hq, hq), 0)
                >= lax.broadcasted_iota(jnp.int32, (hq, hq), 1))

    first_lane_half = lax.broadcasted_iota(jnp.int32, (q, LANE), 1) < SSD_HEAD_DIM
    heads_per_group = SSD_HEADS // SSD_GROUPS
    project_next(1)
    for g in range(SSD_GROUPS):
        project_next(1)
        gs = slice(g * gw, (g + 1) * gw)
        e_g = e_ref[:, gs]
        acum_e = _dot(acum_p, e_g)
        dt_e = _dot(dtv_p, e_g)
        xh = xs_scr[:, gs] * dt_e
        xh_b = xh.astype(BF16)
        last_e = acum_e[q - 1:q, :]
        xdec = (xh * jnp.exp(last_e - acum_e)).astype(BF16)
        bg = bb_scr[:, g * n_state:(g + 1) * n_state]
        cg = cc_scr[:, g * n_state:(g + 1) * n_state]
        s_old = s_scr[g]
        y_off = _dot(cg, s_old.astype(BF16)) * jnp.exp(acum_e)
        s_scr[g] = jnp.exp(last_e) * s_old + _dot_tn(bg, xdec)
        cb = _dot_nt(cg, bg)
        cb00, cb10, cb11 = cb[:hq, :hq], cb[hq:, :hq], cb[hq:, hq:]
        for hh in range(heads_per_group // 2):
            xp = xh_b[:, hh * LANE:(hh + 1) * LANE]
            ys = []
            for e in range(2):
                h = g * heads_per_group + 2 * hh + e
                col = acum2[:, h:h + 1]
                row = acum2_row[h:h + 1, :]
                l00 = jnp.exp2(jnp.where(causal_h, col[:hq] - row[:, :hq], -jnp.inf))
                l10 = jnp.exp2(col[hq:] - row[:, :hq])
                l11 = jnp.exp2(jnp.where(causal_h, col[hq:] - row[:, hq:], -jnp.inf))
                top = _dot((cb00 * l00).astype(BF16), xp[:hq])
                bot = _dot(jnp.concatenate([cb10 * l10, cb11 * l11], axis=1).astype(BF16), xp)
                ys.append(jnp.concatenate([top, bot], axis=0))
            y_pair = jnp.where(first_lane_half, ys[0], ys[1])
            c0 = g * gw + hh * LANE
            y_scr[:, c0:c0 + LANE] = y_pair + y_off[:, hh * LANE:(hh + 1) * LANE]
            project_next(1)
    project_next(len(pieces))

    rb = 32
    for r0 in range(0, q, rb):
        rows = slice(r0, r0 + rb)
        y = y_scr[rows, :] + dsk_ref[...] * xs_scr[rows, :]
        gated = y * _silu(z_scr[rows, :])
        ms = jnp.mean(gated * gated, axis=-1, keepdims=True)
        o_ref[rows, :] = ((gated * lax.rsqrt(ms + RMS_EPS)) * ng_ref[...]).astype(o_ref.dtype)


def _ssd_branch(h, w_in_t, layer, col0, seq, conv_w, conv_b, dt_bias, a_log, d_skip, norm_g):
    t, d = h.shape
    q = SSD_CHUNK
    inner = SSD_HEADS * SSD_HEAD_DIM
    bc = SSD_GROUPS * SSD_STATE
    conv_ch = inner + 2 * bc
    width = 2 * inner + 2 * bc + LANE
    pad = LANE - SSD_HEADS
    n_steps = t // q
    assert col0 % 8 == 0 and seq % q == 0
    piece_rows = jnp.arange(LANE)[:, None]
    expand = ((jnp.arange(inner)[None, :] // SSD_HEAD_DIM == piece_rows % SSD_HEADS)
              & (piece_rows < 3 * SSD_HEADS)).astype(BF16)
    row = lambda v: v.reshape(1, -1)
    const = lambda shape: pl.BlockSpec(shape, lambda s: (0,) * len(shape))
    return pl.pallas_call(
        functools.partial(_ssd_kernel, layer=layer, row0=col0, chunks_per_seq=seq // q),
        grid=(n_steps,),
        in_specs=[
            pl.BlockSpec((q, d), lambda s: (0, 0)),
            pl.BlockSpec((q, d), lambda s: (jnp.minimum(s + 1, n_steps - 1), 0)),
            pl.BlockSpec(memory_space=pl.ANY),
            const((CONV_K, conv_ch)), const((1, conv_ch)), const((1, LANE)), const((1, LANE)),
            const((1, inner)), const((1, inner)), const((LANE, inner)),
        ],
        out_specs=pl.BlockSpec((q, inner), lambda s: (s, 0)),
        out_shape=jax.ShapeDtypeStruct((t, inner), BF16),
        scratch_shapes=_resident_scratch(width, d) + _stage_buffers(d, LANE) + [
            pltpu.VMEM((q, width), F32),
            pltpu.VMEM((q, inner), F32),
            pltpu.VMEM((conv_ch // LANE, q + 8, LANE), F32),
            pltpu.VMEM((q, inner), F32),
            pltpu.VMEM((q, bc), BF16),
            pltpu.VMEM((q, bc), BF16),
            pltpu.VMEM((SSD_GROUPS, SSD_STATE, inner // SSD_GROUPS), F32),
            pltpu.VMEM((q, inner), F32),
        ],
        compiler_params=_params("arbitrary"),
        name="ssd_branch",
    )(h, h, w_in_t, conv_w, row(conv_b),
      row(jnp.pad(dt_bias, (0, pad))), row(jnp.pad(a_log, (0, pad))),
      row(jnp.repeat(d_skip, SSD_HEAD_DIM)), row(norm_g), expand)


def _pool_kernel(a_ref, wt_hbm, pw_ref, sc_ref, o_ref, wb, stage, sem, pwb, tail,
                 *, layer, u_row0, z_row0, seq):
    m = pl.program_id(0)
    bm = a_ref.shape[0]
    ng, gw, _ = pwb.shape
    tile_in_seq = m % (seq // bm)

    @pl.when(m == 0)
    def _():
        def hbm_rows(r0):
            g, part, off = r0 // (2 * gw), (r0 // gw) % 2, r0 % gw
            src = (z_row0 if part else u_row0) + g * gw + off
            return wt_hbm.at[layer, pl.ds(src, STAGE_ROWS), :]

        _stage_resident(hbm_rows, wb, stage, sem)
        for g in range(ng):
            pwb[g] = pw_ref[g].astype(BF16)

    a = a_ref[...]
    pos = tile_in_seq * bm + lax.broadcasted_iota(jnp.int32, (bm, gw), 0)
    for g, w in enumerate(POOL_WINDOWS):
        cs = slice(g * gw, (g + 1) * gw)
        acc = _dot_nt(a, wb[2 * g * gw:2 * (g + 1) * gw, :])
        u = acc[:, :gw]
        hist = jnp.where(tile_in_seq == 0, 0.0, tail[:, cs])
        s = jnp.concatenate([hist, u], axis=0)
        k = 1
        while k < w:
            s = s + pltpu.roll(s, k, axis=0)
            k *= 2
        count = jnp.minimum(pos + 1, w).astype(F32)
        pooled = s[POOL_HALO:, :] / count - u
        mixed = _dot(pooled.astype(BF16), pwb[g])
        o_ref[:, cs] = (mixed * sc_ref[:, cs] * _silu(acc[:, gw:])).astype(o_ref.dtype)
        tail[:, cs] = u[bm - POOL_HALO:, :]


def _pool_branch(h, w_in_t, layer, u_col, z_col, seq, pool_w, pool_scale, bm=1024):
    t, k = h.shape
    _, ng, gw, _ = pool_w.shape
    width = ng * gw
    assert u_col % 8 == 0 and z_col % 8 == 0 and gw % STAGE_ROWS == 0 and seq % bm == 0
    return pl.pallas_call(
        functools.partial(_pool_kernel, layer=layer, u_row0=u_col, z_row0=z_col, seq=seq),
        grid=(t // bm,),
        in_specs=[pl.BlockSpec((bm, k), lambda m: (m, 0)),
                  pl.BlockSpec(memory_space=pl.ANY),
                  pl.BlockSpec((None, ng, gw, gw), lambda m: (layer, 0, 0, 0),
                               pipeline_mode=pl.Buffered(1)),
                  pl.BlockSpec((1, width), lambda m: (0, 0))],
        out_specs=pl.BlockSpec((bm, width), lambda m: (m, 0)),
        out_shape=jax.ShapeDtypeStruct((t, width), BF16),
        scratch_shapes=_resident_scratch(2 * width, k) + _stage_buffers(k)
                       + [pltpu.VMEM((ng, gw, gw), BF16), pltpu.VMEM((POOL_HALO, width), F32)],
        compiler_params=_params("arbitrary"),
        name="pool_branch",
    )(h, w_in_t, pool_w, pool_scale.reshape(1, width))


def _attn_kernel(q0, q1, q2, k0, k1, k2, v0, v1, v2, z_ref, o_ref,
                 op_scr, lp_scr, o0, o1, o2, l0, l1, l2):
    seq, hd = q0.shape
    blk = ATTN_BLOCK
    scale = hd ** -0.5
    qs, ks, vs = (q0, q1, q2), (k0, k1, k2), (v0, v1, v2)
    o_scrs, l_scrs = (o0, o1, o2), (l0, l1, l2)
    qi = lax.broadcasted_iota(jnp.int32, (blk, 2 * blk), 0)
    ci = lax.broadcasted_iota(jnp.int32, (blk, 2 * blk), 1)
    band = (ci >= qi) & (ci <= qi + blk)
    diag = (lax.broadcasted_iota(jnp.int32, (blk, blk), 1)
            <= lax.broadcasted_iota(jnp.int32, (blk, blk), 0))

    for g, (_, dil) in enumerate(ATTN_PATTERNS):
        ld = seq // dil
        nbs = ld // blk
        o_dst = o_scrs[g] if dil == 1 else op_scr
        l_dst = l_scrs[g] if dil == 1 else lp_scr
        for n in range(seq // blk):
            rows = slice(n * blk, (n + 1) * blk)
            if n % nbs == 0:
                krows, mask = rows, diag
            else:
                krows, mask = slice((n - 1) * blk, (n + 1) * blk), band
            s = jnp.where(mask, _dot_nt(qs[g][rows, :], ks[g][krows, :]) * scale, -jnp.inf)
            m = jnp.max(s, axis=-1, keepdims=True)
            e = jnp.exp(s - m)
            den = jnp.sum(e, axis=-1, keepdims=True)
            o_dst[rows, :] = _dot((e / den).astype(BF16), vs[g][krows, :])
            l_dst[rows, :] = jnp.broadcast_to(m + jnp.log(den), (blk, hd))
        if dil > 1:
            for r in range(dil):
                o_scrs[g][pl.ds(r, ld, stride=dil), :] = op_scr[r * ld:(r + 1) * ld, :]
                l_scrs[g][pl.ds(r, ld, stride=dil), :] = lp_scr[r * ld:(r + 1) * ld, :]

    la, lb, lc = l0[...], l1[...], l2[...]
    mx = jnp.maximum(jnp.maximum(la, lb), lc)
    wa, wb, wc = jnp.exp(la - mx), jnp.exp(lb - mx), jnp.exp(lc - mx)
    tot = wa + wb + wc
    comb = (wa / tot) * o0[...] + (wb / tot) * o1[...] + (wc / tot) * o2[...]
    o_ref[...] = (comb * _silu(z_ref[...])).astype(o_ref.dtype)


def _attn_branch(q3, k3, v3, z3):
    b, seq, _ = q3.shape
    hd = ATTN_HEAD_DIM
    hpg = ATTN_HEADS_PER_GROUP
    ngroups = len(ATTN_PATTERNS)

    def head_spec(g):
        return pl.BlockSpec((None, seq, hd), lambda bi, j: (bi, 0, g * hpg + j))

    in_specs = [head_spec(g) for _ in range(3) for g in range(ngroups)]
    in_specs.append(pl.BlockSpec((None, seq, hd), lambda bi, j: (bi, 0, j)))
    scratch = [pltpu.VMEM((seq, hd), F32)] * (2 + 2 * ngroups)
    return pl.pallas_call(
        _attn_kernel,
        grid=(b, hpg),
        in_specs=in_specs,
        out_specs=pl.BlockSpec((None, seq, hd), lambda bi, j: (bi, 0, j)),
        out_shape=jax.ShapeDtypeStruct((b, seq, hpg * hd), BF16),
        scratch_shapes=scratch,
        compiler_params=_params("parallel", "parallel"),
        name="attn_branch",
    )(*([q3] * ngroups + [k3] * ngroups + [v3] * ngroups), z3)


def _merge_kernel(ys_ref, yp_ref, ya_ref, w1_ref, w2_ref, w3_ref, g0_ref, g1_ref, g2_ref, o_ref,
                  wb1, wb2, wb3):
    @pl.when(pl.program_id(1) == 0)
    def _():
        _stage_weight(w1_ref, wb1)
        _stage_weight(w2_ref, wb2)
        _stage_weight(w3_ref, wb3)

    m = _sigmoid(g0_ref[...]) * _dot(ys_ref[...], wb1[...])
    m = m + _sigmoid(g1_ref[...]) * _dot(yp_ref[...], wb2[...])
    m = m + _sigmoid(g2_ref[...]) * _dot(ya_ref[...], wb3[...])
    o_ref[...] = m.astype(o_ref.dtype)


def _merge(y_ssd, y_pool, y_attn, w1, w2, w3, layer, gates2, bm=512, bn=1024):
    t, d = y_ssd.shape
    n = w1.shape[-1]
    gate_blocks = n // bn
    once = pl.Buffered(1)

    def gate_spec(i):
        return pl.BlockSpec((bm, bn), lambda ni, mi: (mi, i * gate_blocks + ni))

    def w_spec(w):
        return pl.BlockSpec((None, w.shape[1], bn), lambda ni, mi: (layer, 0, ni), pipeline_mode=once)

    return pl.pallas_call(
        _merge_kernel,
        grid=(n // bn, t // bm),
        in_specs=[pl.BlockSpec((bm, d), lambda ni, mi: (mi, 0)),
                  pl.BlockSpec((bm, y_pool.shape[1]), lambda ni, mi: (mi, 0)),
                  pl.BlockSpec((bm, y_attn.shape[1]), lambda ni, mi: (mi, 0)),
                  w_spec(w1), w_spec(w2), w_spec(w3),
                  gate_spec(0), gate_spec(1), gate_spec(2)],
        out_specs=pl.BlockSpec((bm, bn), lambda ni, mi: (mi, ni)),
        out_shape=jax.ShapeDtypeStruct((t, n), BF16),
        scratch_shapes=[pltpu.VMEM((w1.shape[1], bn), BF16), pltpu.VMEM((w2.shape[1], bn), BF16),
                        pltpu.VMEM((w3.shape[1], bn), BF16)],
        compiler_params=_params("arbitrary", "arbitrary"),
        name="gated_merge",
    )(y_ssd, y_pool, y_attn, w1, w2, w3, gates2, gates2, gates2)


def _tail_kernel(*refs, layer, emit_h):
    if emit_h:
        (mg_ref, x_ref, p_ref, wo_hbm, wg_hbm, wp_ref, gple_ref, gnext_ref,
         xo_ref, ho_ref, wob, wgb, wpb, stage, sem) = refs
    else:
        (mg_ref, x_ref, p_ref, wo_hbm, wg_hbm, wp_ref, gple_ref,
         xo_ref, wob, wgb, wpb, stage, sem) = refs

    @pl.when(pl.program_id(0) == 0)
    def _():
        _stage_resident(lambda r0: wo_hbm.at[layer, pl.ds(r0, STAGE_ROWS), :], wob, stage, sem)
        _stage_resident(lambda r0: wg_hbm.at[layer, pl.ds(r0, STAGE_ROWS), :], wgb, stage, sem)
        wpb[...] = wp_ref[...].astype(BF16)

    def rms(v, g_ref):
        ms = jnp.mean(v * v, axis=-1, keepdims=True)
        return ((v * lax.rsqrt(ms + RMS_EPS)) * g_ref[...]).astype(BF16)

    x1 = x_ref[...] + _dot(mg_ref[...], wob[...])
    gate = _sigmoid(_dot(rms(x1, gple_ref), wgb[...]))
    x2 = x1 + gate * _dot(p_ref[...].astype(BF16), wpb[...])
    xo_ref[...] = x2
    if emit_h:
        ho_ref[...] = rms(x2, gnext_ref)


def _layer_tail(merged, x2d, p, w_out, w_gate, w_proj, layer, ple_g, next_g, bm=512):
    t, d = x2d.shape
    kp = p.shape[-1]
    emit_h = next_g is not None
    once = pl.Buffered(1)
    row = lambda i: (i, 0)
    in_specs = [pl.BlockSpec((bm, d), row), pl.BlockSpec((bm, d), row),
                pl.BlockSpec((None, bm, kp), lambda i: (layer, i, 0)),
                pl.BlockSpec(memory_space=pl.ANY), pl.BlockSpec(memory_space=pl.ANY),
                pl.BlockSpec((None, kp, d), lambda i: (layer, 0, 0), pipeline_mode=once),
                pl.BlockSpec((1, d), lambda i: (0, 0))]
    args = [merged, x2d, p, w_out, w_gate, w_proj, ple_g.reshape(1, d)]
    out_specs = [pl.BlockSpec((bm, d), row)]
    out_shape = [jax.ShapeDtypeStruct((t, d), F32)]
    if emit_h:
        in_specs.append(pl.BlockSpec((1, d), lambda i: (0, 0)))
        args.append(next_g.reshape(1, d))
        out_specs.append(pl.BlockSpec((bm, d), row))
        out_shape.append(jax.ShapeDtypeStruct((t, d), BF16))
    outs = pl.pallas_call(
        functools.partial(_tail_kernel, layer=layer, emit_h=emit_h),
        grid=(t // bm,),
        in_specs=in_specs,
        out_specs=out_specs,
        out_shape=out_shape,
        scratch_shapes=_resident_scratch(d, d) + _resident_scratch(d, d) + _resident_scratch(kp, d)
                       + _stage_buffers(d),
        compiler_params=_params("arbitrary", vmem=VMEM_LIMIT_HIGH),
        name="layer_tail",
    )(*args)
    return (outs[0], outs[1]) if emit_h else (outs[0], None)


def _in_proj_columns(d_model):
    inner = SSD_HEADS * SSD_HEAD_DIM
    bc = SSD_GROUPS * SSD_STATE
    attn_w = len(ATTN_PATTERNS) * ATTN_HEADS_PER_GROUP * ATTN_HEAD_DIM
    attn_out = ATTN_HEADS_PER_GROUP * ATTN_HEAD_DIM
    sizes = [("z_ssd", inner), ("x", inner), ("B", bc), ("C", bc), ("dt", SSD_HEADS),
             ("u_pool", d_model), ("z_pool", d_model), ("q", attn_w), ("k", attn_w), ("v", attn_w),
             ("z_attn", attn_out), ("gates", 3 * d_model)]
    col, start = {}, 0
    for name, size in sizes:
        col[name] = start
        start += size
    col["end"] = start
    return col


def kernel(x, p, norm_g, w_in, conv_w, conv_b, dt_bias, a_log, d_skip, ssd_norm_g, w_br_ssd, pool_w,
           pool_scale, w_br_pool, q_norm_g, k_norm_g, w_br_attn, w_out, ple_norm_g, w_ple_gate,
           w_ple_proj):
    b, seq, d = x.shape
    depth = w_in.shape[0]
    t = b * seq
    col = _in_proj_columns(d)
    assert col["end"] == w_in.shape[-1]
    x2 = x.reshape(t, d)
    p2 = p.reshape(depth, t, -1)
    w_in_t = jnp.swapaxes(w_in, 1, 2)
    h = _rms_norm_bf16(x2, norm_g[0])
    for i in range(depth):
        q3 =_head_proj(h, w_in_t, i, col["q"], b, seq, q_norm_g[i], "q_proj")
        k3 = _head_proj(h, w_in_t, i, col["k"], b, seq, k_norm_g[i], "k_proj")
        v3 = _head_proj(h, w_in_t, i, col["v"], b, seq, None, "v_proj")
        z_attn = _in_proj(h, w_in_t, i, col["z_attn"], col["gates"] - col["z_attn"], 1024, 512, "in_proj_zattn")
        gates = _in_proj(h, w_in_t, i, col["gates"], col["end"] - col["gates"], 1024, 1024, "in_proj_gates")

        assert col["dt"] - col["z_ssd"] == 2 * SSD_HEADS * SSD_HEAD_DIM + 2 * SSD_GROUPS * SSD_STATE
        y_ssd = _ssd_branch(h, w_in_t, i, col["z_ssd"], seq, conv_w[i], conv_b[i], dt_bias[i],
                            a_log[i], d_skip[i], ssd_norm_g[i])
        y_pool = _pool_branch(h, w_in_t, i, col["u_pool"], col["z_pool"], seq, pool_w, pool_scale[i])
        y_attn = _attn_branch(q3, k3, v3, z_attn.reshape(b, seq, -1))

        merged = _merge(y_ssd.reshape(t, -1), y_pool.reshape(t, -1), y_attn.reshape(t, -1),
                        w_br_ssd, w_br_pool, w_br_attn, i, gates)
        next_g = norm_g[i + 1] if i + 1 < depth else None
        x2, h = _layer_tail(merged, x2, p2, w_out, w_ple_gate, w_ple_proj, i, ple_norm_g[i], next_g)
    return x2.reshape(b, seq, d)
```

```python
import functools

import jax
import jax.numpy as jnp
from jax import lax
from jax.experimental import pallas as pl
from jax.experimental.pallas import tpu as pltpu

F32 = jnp.float32
BF16 = jnp.bfloat16

RMS_EPS = 1e-6
LOG2E = 1.4426950408889634
SSD_HEAD_DIM = 64
SSD_HEADS = 32
SSD_GROUPS = 4
SSD_STATE = 128
SSD_CHUNK = 256
CONV_K = 4
POOL_WINDOWS = (2, 4, 8, 16)
POOL_HALO = 16
ATTN_PATTERNS = ((128, 1), (512, 4), (2048, 16))
ATTN_HEAD_DIM = 128
ATTN_HEADS_PER_GROUP = 4
ATTN_BLOCK = 128

LANE = 128
VMEM_LIMIT = 56 * 1024 * 1024
VMEM_LIMIT_HIGH = 58 * 1024 * 1024
STAGE_ROWS = 256


def _params(*sem, vmem=VMEM_LIMIT):
    return pltpu.CompilerParams(dimension_semantics=sem, vmem_limit_bytes=vmem)


def _sigmoid(v):
    return 0.5 * jnp.tanh(0.5 * v) + 0.5


def _silu(v):
    half = 0.5 * v
    return half * jnp.tanh(half) + half


def _dot(a, b):
    return jnp.dot(a, b, preferred_element_type=F32)


def _dot_nt(a, b):
    return lax.dot_general(a, b, (((1,), (1,)), ((), ())), preferred_element_type=F32)


def _dot_tn(a, b):
    return lax.dot_general(a, b, (((0,), (0,)), ((), ())), preferred_element_type=F32)


def _pack_pieces(v, width, n_pieces):
    packed = None
    rest = v
    for k in range(n_pieces):
        piece = rest.astype(BF16).astype(F32)
        rest = rest - piece
        shifted = piece if k == 0 else pltpu.roll(piece, k * width, axis=1)
        packed = shifted if packed is None else packed + shifted
    return packed


def _stage_weight(w_ref, dst):
    rows_total = dst.shape[0]
    step = min(STAGE_ROWS, rows_total)

    def body(c, carry):
        rows = pl.ds(pl.multiple_of(c * step, step), step)
        dst[rows, :] = w_ref[rows, :].astype(BF16)
        return carry

    lax.fori_loop(0, rows_total // step, body, 0)


def _fetch_weight_rows(wt_hbm, wf, wb, sem, layer, row0):
    n = pl.program_id(0)
    n_tiles = pl.num_programs(0)
    bn = wb.shape[0]

    def copy(tile, slot):
        rows = pl.ds(pl.multiple_of(row0 + tile * bn, 8), bn)
        return pltpu.make_async_copy(wt_hbm.at[layer, rows, :], wf.at[slot], sem.at[slot])

    @pl.when(pl.program_id(1) == 0)
    def _():
        slot = n % 2

        @pl.when(n == 0)
        def _():
            copy(0, 0).start()

        copy(n, slot).wait()

        @pl.when(n + 1 < n_tiles)
        def _():
            copy(n + 1, 1 - slot).start()

        _stage_weight(wf.at[slot], wb)


def _weight_scratch(bn, k):
    return [pltpu.VMEM((2, bn, k), F32), pltpu.VMEM((bn, k), BF16), pltpu.SemaphoreType.DMA((2,))]


def _stage_resident(hbm_rows, wb, stage, sem, transpose=False):
    chunk = stage.shape[1]
    n_chunks = (wb.shape[1] if transpose else wb.shape[0]) // chunk

    def copy(c):
        return pltpu.make_async_copy(hbm_rows(c * chunk), stage.at[c % 2], sem.at[c % 2])

    copy(0).start()
    for c in range(n_chunks):
        if c + 1 < n_chunks:
            copy(c + 1).start()
        copy(c).wait()
        if transpose:
            wb[:, c * chunk:(c + 1) * chunk] = stage[c % 2].T.astype(BF16)
        else:
            wb[c * chunk:(c + 1) * chunk, :] = stage[c % 2].astype(BF16)


def _resident_scratch(rows, cols):
    return [pltpu.VMEM((rows, cols), BF16)]


def _stage_buffers(cols, rows=STAGE_ROWS):
    return [pltpu.VMEM((2, rows, cols), F32), pltpu.SemaphoreType.DMA((2,))]


def _rms_kernel(x_ref, g_ref, o_ref):
    x = x_ref[...]
    ms = jnp.mean(x * x, axis=-1, keepdims=True)
    o_ref[...] = ((x * lax.rsqrt(ms + RMS_EPS)) * g_ref[...]).astype(o_ref.dtype)


def _rms_norm_bf16(x2d, g, bm=512):
    t, d = x2d.shape
    return pl.pallas_call(
        _rms_kernel,
        grid=(t // bm,),
        in_specs=[pl.BlockSpec((bm, d), lambda i: (i, 0)),
                  pl.BlockSpec((1, d), lambda i: (0, 0))],
        out_specs=pl.BlockSpec((bm, d), lambda i: (i, 0)),
        out_shape=jax.ShapeDtypeStruct((t, d), BF16),
        compiler_params=_params("parallel"),
        name="rms_norm",
    )(x2d, g.reshape(1, d))


def _proj_kernel(a_ref, wt_hbm, o_ref, wf, wb, sem, *, layer, row0):
    _fetch_weight_rows(wt_hbm, wf, wb, sem, layer, row0)
    o_ref[...] = _dot_nt(a_ref[...], wb[...]).astype(o_ref.dtype)


def _in_proj(h, w_in_t, layer, col0, width, bm, bn, name):
    t, k = h.shape
    assert col0 % 8 == 0 and width % bn == 0
    return pl.pallas_call(
        functools.partial(_proj_kernel, layer=layer, row0=col0),
        grid=(width // bn, t // bm),
        in_specs=[pl.BlockSpec((bm, k), lambda n, m: (m, 0)),
                  pl.BlockSpec(memory_space=pl.ANY)],
        out_specs=pl.BlockSpec((bm, bn), lambda n, m: (m, n)),
        out_shape=jax.ShapeDtypeStruct((t, width), F32),
        scratch_shapes=_weight_scratch(bn, k),
        compiler_params=_params("arbitrary", "arbitrary"),
        name=name,
    )(h, w_in_t)


def _head_proj_kernel(*refs, layer, row0, seq, normalize):
    if normalize:
        a_ref, wt_hbm, g_ref, o_ref, wb, stage, sem, acc_scr = refs
    else:
        a_ref, wt_hbm, o_ref, wb, stage, sem, acc_scr = refs
    m = pl.program_id(0)
    bm = a_ref.shape[0]
    hd = ATTN_HEAD_DIM
    gw = ATTN_HEADS_PER_GROUP * hd
    half = m % (seq // bm)

    @pl.when(m == 0)
    def _():
        _stage_resident(lambda r0: wt_hbm.at[layer, pl.ds(row0 + r0, STAGE_ROWS), :], wb, stage, sem)

    a = a_ref[...]
    slab = 0
    for g, (_, dil) in enumerate(ATTN_PATTERNS):
        acc = _dot_nt(a, wb[g * gw:(g + 1) * gw, :])
        cnt = bm // dil
        ld = seq // dil
        for hh in range(ATTN_HEADS_PER_GROUP):
            cs = slice(g * gw + hh * hd, g * gw + (hh + 1) * hd)
            v = acc[:, hh * hd:(hh + 1) * hd]
            if normalize:
                ms = jnp.mean(v * v, axis=-1, keepdims=True)
                v = (v * lax.rsqrt(ms + RMS_EPS)) * g_ref[...]
            if dil == 1:
                o_ref[pl.ds(pl.multiple_of(half * bm, bm), bm), cs] = v.astype(BF16)
            else:
                acc_scr[slab] = v
                for r in range(dil):
                    rows = acc_scr[slab, pl.ds(r, cnt, stride=dil), :]
                    dst = r * ld + pl.multiple_of(half * cnt, cnt)
                    o_ref[pl.ds(dst, cnt), cs] = rows.astype(BF16)
                slab += 1


def _head_proj(h, w_in_t, layer, col0, b, seq, norm_g, name, bm=1024):
    t, k = h.shape
    hd = ATTN_HEAD_DIM
    width = len(ATTN_PATTERNS) * ATTN_HEADS_PER_GROUP * hd
    n_dilated = sum(ATTN_HEADS_PER_GROUP for _, dil in ATTN_PATTERNS if dil > 1)
    tiles_per_seq = seq // bm
    assert col0 % 8 == 0 and width % STAGE_ROWS == 0
    normalize = norm_g is not None
    in_specs = [pl.BlockSpec((bm, k), lambda m: (m, 0)), pl.BlockSpec(memory_space=pl.ANY)]
    args = [h, w_in_t]
    if normalize:
        in_specs.append(pl.BlockSpec((1, hd), lambda m: (0, 0)))
        args.append(norm_g.reshape(1, hd))
    return pl.pallas_call(
        functools.partial(_head_proj_kernel, layer=layer, row0=col0, seq=seq, normalize=normalize),
        grid=(t // bm,),
        in_specs=in_specs,
        out_specs=pl.BlockSpec((None, seq, width), lambda m: (m // tiles_per_seq, 0, 0)),
        out_shape=jax.ShapeDtypeStruct((b, seq, width), BF16),
        scratch_shapes=_resident_scratch(width, k) + _stage_buffers(k)
                       + [pltpu.VMEM((n_dilated, bm, hd), F32)],
        compiler_params=_params("arbitrary"),
        name=name,
    )(*args)


def _ssd_kernel(h0_ref, hn_ref, wt_hbm, cw_ref, cb_ref, dtb_ref, alog_ref, dsk_ref, ng_ref, e_ref,
                o_ref, wb, stage, sem, proj, z_scr, xpad, xs_scr, bb_scr, cc_scr, s_scr, y_scr,
                *, layer, row0, chunks_per_seq):
    q = SSD_CHUNK
    inner = xs_scr.shape[-1]
    bc = bb_scr.shape[-1]
    gw = inner // SSD_GROUPS
    n_state = SSD_STATE
    halo = 8
    step = pl.program_id(0)

    @pl.when(step == 0)
    def _():
        chunk = stage.shape[1]
        _stage_resident(lambda r0: wt_hbm.at[layer, pl.ds(row0 + r0, chunk), :], wb, stage, sem,
                        transpose=True)
        proj[...] = _dot(h0_ref[...], wb[...])

    @pl.when(step % chunks_per_seq == 0)
    def _():
        xpad[:, 0:halo, :] = jnp.zeros((xpad.shape[0], halo, LANE), F32)
        s_scr[...] = jnp.zeros(s_scr.shape, F32)

    x_slabs = inner // LANE
    bc_slabs = bc // LANE
    for j in range(xpad.shape[0]):
        cs = slice(j * LANE, (j + 1) * LANE)
        xpad[j, halo:halo + q, :] = proj[:, inner + j * LANE:inner + (j + 1) * LANE]
        acc = cb_ref[:, cs] + cw_ref[0:1, cs] * xpad[j, halo - CONV_K + 1:halo - CONV_K + 1 + q, :]
        for k in range(1, CONV_K):
            r0 = halo - CONV_K + 1 + k
            acc = acc + cw_ref[k:k + 1, cs] * xpad[j, r0:r0 + q, :]
        xc = _silu(acc)
        if j < x_slabs:
            xs_scr[:, cs] = xc
        elif j < x_slabs + bc_slabs:
            bb_scr[:, (j - x_slabs) * LANE:(j - x_slabs + 1) * LANE] = xc.astype(BF16)
        else:
            jc = j - x_slabs - bc_slabs
            cc_scr[:, jc * LANE:(jc + 1) * LANE] = xc.astype(BF16)
        xpad[j, 0:halo, :] = xpad[j, q:q + halo, :]
    z_scr[...] = proj[:, 0:inner]

    dt_lanes = lax.broadcasted_iota(jnp.int32, (q, LANE), 1) < SSD_HEADS
    dt_col = 2 * inner + 2 * bc
    dt_raw = jnp.where(dt_lanes, proj[:, dt_col:dt_col + LANE], 0.0)

    h_next = hn_ref[...]
    piece = 2 * LANE
    pieces = [(c0, min(c0 + piece, proj.shape[1])) for c0 in range(0, proj.shape[1], piece)]

    def project_next(count):
        for _ in range(min(count, len(pieces))):
            c0, c1 = pieces.pop(0)
            proj[:, c0:c1] = _dot(h_next, wb[:, c0:c1])

    dtr = dt_raw + dtb_ref[...]
    dtv = jnp.where(dt_lanes, jnp.maximum(dtr, 0.0) + jnp.log1p(jnp.exp(-jnp.abs(dtr))), 0.0)
    da = dtv * (-jnp.exp(alog_ref[...]))
    ri = lax.broadcasted_iota(jnp.int32, (q, q), 0)
    ci = lax.broadcasted_iota(jnp.int32, (q, q), 1)
    tri = jnp.where(ri >= ci, 1.0, 0.0).astype(BF16)
    tri_t = jnp.where(ri <= ci, 1.0, 0.0).astype(BF16)
    nh = SSD_HEADS
    da3 = _pack_pieces(da, nh, 3)
    cum3 = _dot(tri, da3.astype(BF16))
    acum = jnp.where(dt_lanes, cum3 + pltpu.roll(cum3, LANE - nh, axis=1)
                     + pltpu.roll(cum3, LANE - 2 * nh, axis=1), 0.0)
    acum2 = acum * LOG2E
    cum3_row = _dot(da3.T.astype(BF16), tri_t)
    acum2_row = (cum3_row[0:nh] + cum3_row[nh:2 * nh] + cum3_row[2 * nh:3 * nh]) * LOG2E
    acum_p = _pack_pieces(acum, nh, 3).astype(BF16)
    dtv_p = _pack_pieces(dtv, nh, 2).astype(BF16)
    hq = q // 2
    causal_h = (lax.broadcasted_iota(jnp.int32, (hq, hq), 0)
                >= lax.broadcasted_iota(jnp.int32, (hq, hq), 1))

    first_lane_half = lax.broadcasted_iota(jnp.int32, (q, LANE), 1) < SSD_HEAD_DIM
    heads_per_group = SSD_HEADS // SSD_GROUPS
    project_next(1)
    for g in range(SSD_GROUPS):
        project_next(1)
        gs = slice(g * gw, (g + 1) * gw)
        e_g = e_ref[:, gs]
        acum_e = _dot(acum_p, e_g)
        dt_e = _dot(dtv_p, e_g)
        xh = xs_scr[:, gs] * dt_e
        xh_b = xh.astype(BF16)
        last_e = acum_e[q - 1:q, :]
        xdec = (xh * jnp.exp(last_e - acum_e)).astype(BF16)
        bg = bb_scr[:, g * n_state:(g + 1) * n_state]
        cg = cc_scr[:, g * n_state:(g + 1) * n_state]
        s_old = s_scr[g]
        y_off = _dot(cg, s_old.astype(BF16)) * jnp.exp(acum_e)
        s_scr[g] = jnp.exp(last_e) * s_old + _dot_tn(bg, xdec)
        cb = _dot_nt(cg, bg)
        cb00, cb10, cb11 = cb[:hq, :hq], cb[hq:, :hq], cb[hq:, hq:]
        for hh in range(heads_per_group // 2):
            xp = xh_b[:, hh * LANE:(hh + 1) * LANE]
            ys = []
            for e in range(2):
                h = g * heads_per_group + 2 * hh + e
                col = acum2[:, h:h + 1]
                row = acum2_row[h:h + 1, :]
                l00 = jnp.exp2(jnp.where(causal_h, col[:hq] - row[:, :hq], -jnp.inf))
                l10 = jnp.exp2(col[hq:] - row[:, :hq])
                l11 = jnp.exp2(jnp.where(causal_h, col[hq:] - row[:, hq:], -jnp.inf))
                top = _dot((cb00 * l00).astype(BF16), xp[:hq])
                bot = _dot(jnp.concatenate([cb10 * l10, cb11 * l11], axis=1).astype(BF16), xp)
                ys.append(jnp.concatenate([top, bot], axis=0))
            y_pair = jnp.where(first_lane_half, ys[0], ys[1])
            c0 = g * gw + hh * LANE
            y_scr[:, c0:c0 + LANE] = y_pair + y_off[:, hh * LANE:(hh + 1) * LANE]
            project_next(1)
    project_next(len(pieces))

    rb = 32
    for r0 in range(0, q, rb):
        rows = slice(r0, r0 + rb)
        y = y_scr[rows, :] + dsk_ref[...] * xs_scr[rows, :]
        gated = y * _silu(z_scr[rows, :])
        ms = jnp.mean(gated * gated, axis=-1, keepdims=True)
        o_ref[rows, :] = ((gated * lax.rsqrt(ms + RMS_EPS)) * ng_ref[...]).astype(o_ref.dtype)


def _ssd_branch(h, w_in_t, layer, col0, seq, conv_w, conv_b, dt_bias, a_log, d_skip, norm_g):
    t, d = h.shape
    q = SSD_CHUNK
    inner = SSD_HEADS * SSD_HEAD_DIM
    bc = SSD_GROUPS * SSD_STATE
    conv_ch = inner + 2 * bc
    width = 2 * inner + 2 * bc + LANE
    pad = LANE - SSD_HEADS
    n_steps = t // q
    assert col0 % 8 == 0 and seq % q == 0
    piece_rows = jnp.arange(LANE)[:, None]
    expand = ((jnp.arange(inner)[None, :] // SSD_HEAD_DIM == piece_rows % SSD_HEADS)
              & (piece_rows < 3 * SSD_HEADS)).astype(BF16)
    row = lambda v: v.reshape(1, -1)
    const = lambda shape: pl.BlockSpec(shape, lambda s: (0,) * len(shape))
    return pl.pallas_call(
        functools.partial(_ssd_kernel, layer=layer, row0=col0, chunks_per_seq=seq // q),
        grid=(n_steps,),
        in_specs=[
            pl.BlockSpec((q, d), lambda s: (0, 0)),
            pl.BlockSpec((q, d), lambda s: (jnp.minimum(s + 1, n_steps - 1), 0)),
            pl.BlockSpec(memory_space=pl.ANY),
            const((CONV_K, conv_ch)), const((1, conv_ch)), const((1, LANE)), const((1, LANE)),
            const((1, inner)), const((1, inner)), const((LANE, inner)),
        ],
        out_specs=pl.BlockSpec((q, inner), lambda s: (s, 0)),
        out_shape=jax.ShapeDtypeStruct((t, inner), BF16),
        scratch_shapes=_resident_scratch(d, width) + _stage_buffers(d, LANE) + [
            pltpu.VMEM((q, width), F32),
            pltpu.VMEM((q, inner), F32),
            pltpu.VMEM((conv_ch // LANE, q + 8, LANE), F32),
            pltpu.VMEM((q, inner), F32),
            pltpu.VMEM((q, bc), BF16),
            pltpu.VMEM((q, bc), BF16),
            pltpu.VMEM((SSD_GROUPS, SSD_STATE, inner // SSD_GROUPS), F32),
            pltpu.VMEM((q, inner), F32),
        ],
        compiler_params=_params("arbitrary"),
        name="ssd_branch",
    )(h, h, w_in_t, conv_w, row(conv_b),
      row(jnp.pad(dt_bias, (0, pad))), row(jnp.pad(a_log, (0, pad))),
      row(jnp.repeat(d_skip, SSD_HEAD_DIM)), row(norm_g), expand)


def _pool_kernel(a_ref, wt_hbm, pw_ref, sc_ref, o_ref, wb, stage, sem, pwb, tail,
                 *, layer, u_row0, z_row0, seq):
    m = pl.program_id(0)
    bm = a_ref.shape[0]
    ng, gw, _ = pwb.shape
    tile_in_seq = m % (seq // bm)

    @pl.when(m == 0)
    def _():
        def hbm_rows(r0):
            g, part, off = r0 // (2 * gw), (r0 // gw) % 2, r0 % gw
            src = (z_row0 if part else u_row0) + g * gw + off
            return wt_hbm.at[layer, pl.ds(src, STAGE_ROWS), :]

        _stage_resident(hbm_rows, wb, stage, sem)
        for g in range(ng):
            pwb[g] = pw_ref[g].astype(BF16)

    a = a_ref[...]
    pos = tile_in_seq * bm + lax.broadcasted_iota(jnp.int32, (bm, gw), 0)
    for g, w in enumerate(POOL_WINDOWS):
        cs = slice(g * gw, (g + 1) * gw)
        acc = _dot_nt(a, wb[2 * g * gw:2 * (g + 1) * gw, :])
        u = acc[:, :gw]
        hist = jnp.where(tile_in_seq == 0, 0.0, tail[:, cs])
        s = jnp.concatenate([hist, u], axis=0)
        k = 1
        while k < w:
            s = s + pltpu.roll(s, k, axis=0)
            k *= 2
        count = jnp.minimum(pos + 1, w).astype(F32)
        pooled = s[POOL_HALO:, :] / count - u
        mixed = _dot(pooled.astype(BF16), pwb[g])
        o_ref[:, cs] = (mixed * sc_ref[:, cs] * _silu(acc[:, gw:])).astype(o_ref.dtype)
        tail[:, cs] = u[bm - POOL_HALO:, :]


def _pool_branch(h, w_in_t, layer, u_col, z_col, seq, pool_w, pool_scale, bm=1024):
    t, k = h.shape
    _, ng, gw, _ = pool_w.shape
    width = ng * gw
    assert u_col % 8 == 0 and z_col % 8 == 0 and gw % STAGE_ROWS == 0 and seq % bm == 0
    return pl.pallas_call(
        functools.partial(_pool_kernel, layer=layer, u_row0=u_col, z_row0=z_col, seq=seq),
        grid=(t // bm,),
        in_specs=[pl.BlockSpec((bm, k), lambda m: (m, 0)),
                  pl.BlockSpec(memory_space=pl.ANY),
                  pl.BlockSpec((None, ng, gw, gw), lambda m: (layer, 0, 0, 0),
                               pipeline_mode=pl.Buffered(1)),
                  pl.BlockSpec((1, width), lambda m: (0, 0))],
        out_specs=pl.BlockSpec((bm, width), lambda m: (m, 0)),
        out_shape=jax.ShapeDtypeStruct((t, width), BF16),
        scratch_shapes=_resident_scratch(2 * width, k) + _stage_buffers(k)
                       + [pltpu.VMEM((ng, gw, gw), BF16), pltpu.VMEM((POOL_HALO, width), F32)],
        compiler_params=_params("arbitrary"),
        name="pool_branch",
    )(h, w_in_t, pool_w, pool_scale.reshape(1, width))


def _attn_kernel(q0, q1, q2, k0, k1, k2, v0, v1, v2, z_ref, o_ref,
                 op_scr, lp_scr, o0, o1, o2, l0, l1, l2):
    seq, hd = q0.shape
    blk = ATTN_BLOCK
    scale = hd ** -0.5
    qs, ks, vs = (q0, q1, q2), (k0, k1, k2), (v0, v1, v2)
    o_scrs, l_scrs = (o0, o1, o2), (l0, l1, l2)
    qi = lax.broadcasted_iota(jnp.int32, (blk, 2 * blk), 0)
    ci = lax.broadcasted_iota(jnp.int32, (blk, 2 * blk), 1)
    band = (ci >= qi) & (ci <= qi + blk)
    diag = (lax.broadcasted_iota(jnp.int32, (blk, blk), 1)
            <= lax.broadcasted_iota(jnp.int32, (blk, blk), 0))

    for g, (_, dil) in enumerate(ATTN_PATTERNS):
        ld = seq // dil
        nbs = ld // blk
        o_dst = o_scrs[g] if dil == 1 else op_scr
        l_dst = l_scrs[g] if dil == 1 else lp_scr
        for n in range(seq // blk):
            rows = slice(n * blk, (n + 1) * blk)
            if n % nbs == 0:
                krows, mask = rows, diag
            else:
                krows, mask = slice((n - 1) * blk, (n + 1) * blk), band
            s = jnp.where(mask, _dot_nt(qs[g][rows, :], ks[g][krows, :]) * scale, -jnp.inf)
            m = jnp.max(s, axis=-1, keepdims=True)
            e = jnp.exp(s - m)
            den = jnp.sum(e, axis=-1, keepdims=True)
            o_dst[rows, :] = _dot((e / den).astype(BF16), vs[g][krows, :])
            l_dst[rows, :] = jnp.broadcast_to(m + jnp.log(den), (blk, hd))
        if dil > 1:
            for r in range(dil):
                o_scrs[g][pl.ds(r, ld, stride=dil), :] = op_scr[r * ld:(r + 1) * ld, :]
                l_scrs[g][pl.ds(r, ld, stride=dil), :] = lp_scr[r * ld:(r + 1) * ld, :]

    la, lb, lc = l0[...], l1[...], l2[...]
    mx = jnp.maximum(jnp.maximum(la, lb), lc)
    wa, wb, wc = jnp.exp(la - mx), jnp.exp(lb - mx), jnp.exp(lc - mx)
    tot = wa + wb + wc
    comb = (wa / tot) * o0[...] + (wb / tot) * o1[...] + (wc / tot) * o2[...]
    o_ref[...] = (comb * _silu(z_ref[...])).astype(o_ref.dtype)


def _attn_branch(q3, k3, v3, z3):
    b, seq, _ = q3.shape
    hd = ATTN_HEAD_DIM
    hpg = ATTN_HEADS_PER_GROUP
    ngroups = len(ATTN_PATTERNS)

    def head_spec(g):
        return pl.BlockSpec((None, seq, hd), lambda bi, j: (bi, 0, g * hpg + j))

    in_specs = [head_spec(g) for _ in range(3) for g in range(ngroups)]
    in_specs.append(pl.BlockSpec((None, seq, hd), lambda bi, j: (bi, 0, j)))
    scratch = [pltpu.VMEM((seq, hd), F32)] * (2 + 2 * ngroups)
    return pl.pallas_call(
        _attn_kernel,
        grid=(b, hpg),
        in_specs=in_specs,
        out_specs=pl.BlockSpec((None, seq, hd), lambda bi, j: (bi, 0, j)),
        out_shape=jax.ShapeDtypeStruct((b, seq, hpg * hd), BF16),
        scratch_shapes=scratch,
        compiler_params=_params("parallel", "parallel"),
        name="attn_branch",
    )(*([q3] * ngroups + [k3] * ngroups + [v3] * ngroups), z3)


def _merge_kernel(ys_ref, yp_ref, ya_ref, w1_ref, w2_ref, w3_ref, g0_ref, g1_ref, g2_ref, o_ref,
                  wb1, wb2, wb3):
    @pl.when(pl.program_id(1) == 0)
    def _():
        _stage_weight(w1_ref, wb1)
        _stage_weight(w2_ref, wb2)
        _stage_weight(w3_ref, wb3)

    m = _sigmoid(g0_ref[...]) * _dot(ys_ref[...], wb1[...])
    m = m + _sigmoid(g1_ref[...]) * _dot(yp_ref[...], wb2[...])
    m = m + _sigmoid(g2_ref[...]) * _dot(ya_ref[...], wb3[...])
    o_ref[...] = m.astype(o_ref.dtype)


def _merge(y_ssd, y_pool, y_attn, w1, w2, w3, layer, gates2, bm=512, bn=1024):
    t, d = y_ssd.shape
    n = w1.shape[-1]
    gate_blocks = n // bn
    once = pl.Buffered(1)

    def gate_spec(i):
        return pl.BlockSpec((bm, bn), lambda ni, mi: (mi, i * gate_blocks + ni))

    def w_spec(w):
        return pl.BlockSpec((None, w.shape[1], bn), lambda ni, mi: (layer, 0, ni), pipeline_mode=once)

    return pl.pallas_call(
        _merge_kernel,
        grid=(n // bn, t // bm),
        in_specs=[pl.BlockSpec((bm, d), lambda ni, mi: (mi, 0)),
                  pl.BlockSpec((bm, y_pool.shape[1]), lambda ni, mi: (mi, 0)),
                  pl.BlockSpec((bm, y_attn.shape[1]), lambda ni, mi: (mi, 0)),
                  w_spec(w1), w_spec(w2), w_spec(w3),
                  gate_spec(0), gate_spec(1), gate_spec(2)],
        out_specs=pl.BlockSpec((bm, bn), lambda ni, mi: (mi, ni)),
        out_shape=jax.ShapeDtypeStruct((t, n), BF16),
        scratch_shapes=[pltpu.VMEM((w1.shape[1], bn), BF16), pltpu.VMEM((w2.shape[1], bn), BF16),
                        pltpu.VMEM((w3.shape[1], bn), BF16)],
        compiler_params=_params("arbitrary", "arbitrary"),
        name="gated_merge",
    )(y_ssd, y_pool, y_attn, w1, w2, w3, gates2, gates2, gates2)


def _tail_kernel(*refs, layer, emit_h):
    if emit_h:
        (mg_ref, x_ref, p_ref, wo_hbm, wg_hbm, wp_ref, gple_ref, gnext_ref,
         xo_ref, ho_ref, wob, wgb, wpb, stage, sem) = refs
    else:
        (mg_ref, x_ref, p_ref, wo_hbm, wg_hbm, wp_ref, gple_ref,
         xo_ref, wob, wgb, wpb, stage, sem) = refs

    @pl.when(pl.program_id(0) == 0)
    def _():
        _stage_resident(lambda r0: wo_hbm.at[layer, pl.ds(r0, STAGE_ROWS), :], wob, stage, sem)
        _stage_resident(lambda r0: wg_hbm.at[layer, pl.ds(r0, STAGE_ROWS), :], wgb, stage, sem)
        wpb[...] = wp_ref[...].astype(BF16)

    def rms(v, g_ref):
        ms = jnp.mean(v * v, axis=-1, keepdims=True)
        return ((v * lax.rsqrt(ms + RMS_EPS)) * g_ref[...]).astype(BF16)

    x1 = x_ref[...] + _dot(mg_ref[...], wob[...])
    gate = _sigmoid(_dot(rms(x1, gple_ref), wgb[...]))
    x2 = x1 + gate * _dot(p_ref[...].astype(BF16), wpb[...])
    xo_ref[...] = x2
    if emit_h:
        ho_ref[...] = rms(x2, gnext_ref)


def _layer_tail(merged, x2d, p, w_out, w_gate, w_proj, layer, ple_g, next_g, bm=512):
    t, d = x2d.shape
    kp = p.shape[-1]
    emit_h = next_g is not None
    once = pl.Buffered(1)
    row = lambda i: (i, 0)
    in_specs = [pl.BlockSpec((bm, d), row), pl.BlockSpec((bm, d), row),
                pl.BlockSpec((None, bm, kp), lambda i: (layer, i, 0)),
                pl.BlockSpec(memory_space=pl.ANY), pl.BlockSpec(memory_space=pl.ANY),
                pl.BlockSpec((None, kp, d), lambda i: (layer, 0, 0), pipeline_mode=once),
                pl.BlockSpec((1, d), lambda i: (0, 0))]
    args = [merged, x2d, p, w_out, w_gate, w_proj, ple_g.reshape(1, d)]
    out_specs = [pl.BlockSpec((bm, d), row)]
    out_shape = [jax.ShapeDtypeStruct((t, d), F32)]
    if emit_h:
        in_specs.append(pl.BlockSpec((1, d), lambda i: (0, 0)))
        args.append(next_g.reshape(1, d))
        out_specs.append(pl.BlockSpec((bm, d), row))
        out_shape.append(jax.ShapeDtypeStruct((t, d), BF16))
    outs = pl.pallas_call(
        functools.partial(_tail_kernel, layer=layer, emit_h=emit_h),
        grid=(t // bm,),
        in_specs=in_specs,
        out_specs=out_specs,
        out_shape=out_shape,
        scratch_shapes=_resident_scratch(d, d) + _resident_scratch(d, d) + _resident_scratch(kp, d)
                       + _stage_buffers(d),
        compiler_params=_params("arbitrary", vmem=VMEM_LIMIT_HIGH),
        name="layer_tail",
    )(*args)
    return (outs[0], outs[1]) if emit_h else (outs[0], None)


def _in_proj_columns(d_model):
    inner = SSD_HEADS * SSD_HEAD_DIM
    bc = SSD_GROUPS * SSD_STATE
    attn_w = len(ATTN_PATTERNS) * ATTN_HEADS_PER_GROUP * ATTN_HEAD_DIM
    attn_out = ATTN_HEADS_PER_GROUP * ATTN_HEAD_DIM
    sizes = [("z_ssd", inner), ("x", inner), ("B", bc), ("C", bc), ("dt", SSD_HEADS),
             ("u_pool", d_model), ("z_pool", d_model), ("q", attn_w), ("k", attn_w), ("v", attn_w),
             ("z_attn", attn_out), ("gates", 3 * d_model)]
    col, start = {}, 0
    for name, size in sizes:
        col[name] = start
        start += size
    col["end"] = start
    return col


def kernel(x, p, norm_g, w_in, conv_w, conv_b, dt_bias, a_log, d_skip, ssd_norm_g, w_br_ssd, pool_w,
           pool_scale, w_br_pool, q_norm_g, k_norm_g, w_br_attn, w_out, ple_norm_g, w_ple_gate,
           w_ple_proj):
    b, seq, d = x.shape
    depth = w_in.shape[0]
    t = b * seq
    col = _in_proj_columns(d)
    assert col["end"] == w_in.shape[-1]
    x2 = x.reshape(t, d)
    p2 = p.reshape(depth, t, -1)
    w_in_t = jnp.swapaxes(w_in, 1, 2)
    h = _rms_norm_bf16(x2, norm_g[0])
    for i in range(depth):
        q3 =_head_proj(h, w_in_t, i, col["q"], b, seq, q_norm_g[i], "q_proj")
        k3 = _head_proj(h, w_in_t, i, col["k"], b, seq, k_norm_g[i], "k_proj")
        v3 = _head_proj(h, w_in_t, i, col["v"], b, seq, None, "v_proj")
        z_attn = _in_proj(h, w_in_t, i, col["z_attn"], col["gates"] - col["z_attn"], 1024, 512, "in_proj_zattn")
        gates = _in_proj(h, w_in_t, i, col["gates"], col["end"] - col["gates"], 1024, 1024, "in_proj_gates")

        assert col["dt"] - col["z_ssd"] == 2 * SSD_HEADS * SSD_HEAD_DIM + 2 * SSD_GROUPS * SSD_STATE
        y_ssd = _ssd_branch(h, w_in_t, i, col["z_ssd"], seq, conv_w[i], conv_b[i], dt_bias[i],
                            a_log[i], d_skip[i], ssd_norm_g[i])
        y_pool = _pool_branch(h, w_in_t, i, col["u_pool"], col["z_pool"], seq, pool_w, pool_scale[i])
        y_attn = _attn_branch(q3, k3, v3, z_attn.reshape(b, seq, -1))

        merged = _merge(y_ssd.reshape(t, -1), y_pool.reshape(t, -1), y_attn.reshape(t, -1),
                        w_br_ssd, w_br_pool, w_br_attn, i, gates)
        next_g = norm_g[i + 1] if i + 1 < depth else None
        x2, h = _layer_tail(merged, x2, p2, w_out, w_ple_gate, w_ple_proj, i, ple_norm_g[i], next_g)
    return x2.reshape(b, seq, d)
```

```python
import functools

import jax
import jax.numpy as jnp
from jax import lax
from jax.experimental import pallas as pl
from jax.experimental.pallas import tpu as pltpu

F32 = jnp.float32
BF16 = jnp.bfloat16

RMS_EPS = 1e-6
LOG2E = 1.4426950408889634
SSD_HEAD_DIM = 64
SSD_HEADS = 32
SSD_GROUPS = 4
SSD_STATE = 128
SSD_CHUNK = 256
CONV_K = 4
POOL_WINDOWS = (2, 4, 8, 16)
POOL_HALO = 16
ATTN_PATTERNS = ((128, 1), (512, 4), (2048, 16))
ATTN_HEAD_DIM = 128
ATTN_HEADS_PER_GROUP = 4
ATTN_BLOCK = 128

LANE = 128
VMEM_LIMIT = 56 * 1024 * 1024
VMEM_LIMIT_HIGH = 58 * 1024 * 1024
STAGE_ROWS = 256


def _params(*sem, vmem=VMEM_LIMIT):
    return pltpu.CompilerParams(dimension_semantics=sem, vmem_limit_bytes=vmem)


def _sigmoid(v):
    return 0.5 * jnp.tanh(0.5 * v) + 0.5


def _silu(v):
    half = 0.5 * v
    return half * jnp.tanh(half) + half


def _dot(a, b):
    return jnp.dot(a, b, preferred_element_type=F32)


def _dot_nt(a, b):
    return lax.dot_general(a, b, (((1,), (1,)), ((), ())), preferred_element_type=F32)


def _dot_tn(a, b):
    return lax.dot_general(a, b, (((0,), (0,)), ((), ())), preferred_element_type=F32)


def _pack_pieces(v, width, n_pieces):
    packed = None
    rest = v
    for k in range(n_pieces):
        piece = rest.astype(BF16).astype(F32)
        rest = rest - piece
        shifted = piece if k == 0 else pltpu.roll(piece, k * width, axis=1)
        packed = shifted if packed is None else packed + shifted
    return packed


def _stage_weight(w_ref, dst):
    rows_total = dst.shape[0]
    step = min(STAGE_ROWS, rows_total)

    def body(c, carry):
        rows = pl.ds(pl.multiple_of(c * step, step), step)
        dst[rows, :] = w_ref[rows, :].astype(BF16)
        return carry

    lax.fori_loop(0, rows_total // step, body, 0)


def _fetch_weight_rows(wt_hbm, wf, wb, sem, layer, row0):
    n = pl.program_id(0)
    n_tiles = pl.num_programs(0)
    bn = wb.shape[0]

    def copy(tile, slot):
        rows = pl.ds(pl.multiple_of(row0 + tile * bn, 8), bn)
        return pltpu.make_async_copy(wt_hbm.at[layer, rows, :], wf.at[slot], sem.at[slot])

    @pl.when(pl.program_id(1) == 0)
    def _():
        slot = n % 2

        @pl.when(n == 0)
        def _():
            copy(0, 0).start()

        copy(n, slot).wait()

        @pl.when(n + 1 < n_tiles)
        def _():
            copy(n + 1, 1 - slot).start()

        _stage_weight(wf.at[slot], wb)


def _weight_scratch(bn, k):
    return [pltpu.VMEM((2, bn, k), F32), pltpu.VMEM((bn, k), BF16), pltpu.SemaphoreType.DMA((2,))]


def _stage_resident(hbm_rows, wb, stage, sem, transpose=False):
    chunk = stage.shape[1]
    n_chunks = (wb.shape[1] if transpose else wb.shape[0]) // chunk

    def copy(c):
        return pltpu.make_async_copy(hbm_rows(c * chunk), stage.at[c % 2], sem.at[c % 2])

    copy(0).start()
    for c in range(n_chunks):
        if c + 1 < n_chunks:
            copy(c + 1).start()
        copy(c).wait()
        if transpose:
            wb[:, c * chunk:(c + 1) * chunk] = stage[c % 2].T.astype(BF16)
        else:
            wb[c * chunk:(c + 1) * chunk, :] = stage[c % 2].astype(BF16)


def _resident_scratch(rows, cols):
    return [pltpu.VMEM((rows, cols), BF16)]


def _stage_buffers(cols, rows=STAGE_ROWS):
    return [pltpu.VMEM((2, rows, cols), F32), pltpu.SemaphoreType.DMA((2,))]


def _rms_kernel(x_ref, g_ref, o_ref):
    x = x_ref[...]
    ms = jnp.mean(x * x, axis=-1, keepdims=True)
    o_ref[...] = ((x * lax.rsqrt(ms + RMS_EPS)) * g_ref[...]).astype(o_ref.dtype)


def _rms_norm_bf16(x2d, g, bm=512):
    t, d = x2d.shape
    return pl.pallas_call(
        _rms_kernel,
        grid=(t // bm,),
        in_specs=[pl.BlockSpec((bm, d), lambda i: (i, 0)),
                  pl.BlockSpec((1, d), lambda i: (0, 0))],
        out_specs=pl.BlockSpec((bm, d), lambda i: (i, 0)),
        out_shape=jax.ShapeDtypeStruct((t, d), BF16),
        compiler_params=_params("parallel"),
        name="rms_norm",
    )(x2d, g.reshape(1, d))


def _proj_kernel(a_ref, wt_hbm, o_ref, wf, wb, sem, *, layer, row0):
    _fetch_weight_rows(wt_hbm, wf, wb, sem, layer, row0)
    o_ref[...] = _dot_nt(a_ref[...], wb[...]).astype(o_ref.dtype)


def _in_proj(h, w_in_t, layer, col0, width, bm, bn, name):
    t, k = h.shape
    assert col0 % 8 == 0 and width % bn == 0
    return pl.pallas_call(
        functools.partial(_proj_kernel, layer=layer, row0=col0),
        grid=(width // bn, t // bm),
        in_specs=[pl.BlockSpec((bm, k), lambda n, m: (m, 0)),
                  pl.BlockSpec(memory_space=pl.ANY)],
        out_specs=pl.BlockSpec((bm, bn), lambda n, m: (m, n)),
        out_shape=jax.ShapeDtypeStruct((t, width), F32),
        scratch_shapes=_weight_scratch(bn, k),
        compiler_params=_params("arbitrary", "arbitrary"),
        name=name,
    )(h, w_in_t)


def _head_proj_kernel(*refs, layer, row0, seq, normalize):
    if normalize:
        a_ref, wt_hbm, g_ref, o_ref, wb, stage, sem, acc_scr = refs
    else:
        a_ref, wt_hbm, o_ref, wb, stage, sem, acc_scr = refs
    m = pl.program_id(0)
    bm = a_ref.shape[0]
    hd = ATTN_HEAD_DIM
    gw = ATTN_HEADS_PER_GROUP * hd
    half = m % (seq // bm)

    @pl.when(m == 0)
    def _():
        _stage_resident(lambda r0: wt_hbm.at[layer, pl.ds(row0 + r0, STAGE_ROWS), :], wb, stage, sem)

    a = a_ref[...]
    slab = 0
    for g, (_, dil) in enumerate(ATTN_PATTERNS):
        acc = _dot_nt(a, wb[g * gw:(g + 1) * gw, :])
        cnt = bm // dil
        ld = seq // dil
        for hh in range(ATTN_HEADS_PER_GROUP):
            cs = slice(g * gw + hh * hd, g * gw + (hh + 1) * hd)
            v = acc[:, hh * hd:(hh + 1) * hd]
            if normalize:
                ms = jnp.mean(v * v, axis=-1, keepdims=True)
                v = (v * lax.rsqrt(ms + RMS_EPS)) * g_ref[...]
            if dil == 1:
                o_ref[pl.ds(pl.multiple_of(half * bm, bm), bm), cs] = v.astype(BF16)
            else:
                acc_scr[slab] = v
                for r in range(dil):
                    rows = acc_scr[slab, pl.ds(r, cnt, stride=dil), :]
                    dst = r * ld + pl.multiple_of(half * cnt, cnt)
                    o_ref[pl.ds(dst, cnt), cs] = rows.astype(BF16)
                slab += 1


def _head_proj(h, w_in_t, layer, col0, b, seq, norm_g, name, bm=1024):
    t, k = h.shape
    hd = ATTN_HEAD_DIM
    width = len(ATTN_PATTERNS) * ATTN_HEADS_PER_GROUP * hd
    n_dilated = sum(ATTN_HEADS_PER_GROUP for _, dil in ATTN_PATTERNS if dil > 1)
    tiles_per_seq = seq // bm
    assert col0 % 8 == 0 and width % STAGE_ROWS == 0
    normalize = norm_g is not None
    in_specs = [pl.BlockSpec((bm, k), lambda m: (m, 0)), pl.BlockSpec(memory_space=pl.ANY)]
    args = [h, w_in_t]
    if normalize:
        in_specs.append(pl.BlockSpec((1, hd), lambda m: (0, 0)))
        args.append(norm_g.reshape(1, hd))
    return pl.pallas_call(
        functools.partial(_head_proj_kernel, layer=layer, row0=col0, seq=seq, normalize=normalize),
        grid=(t // bm,),
        in_specs=in_specs,
        out_specs=pl.BlockSpec((None, seq, width), lambda m: (m // tiles_per_seq, 0, 0)),
        out_shape=jax.ShapeDtypeStruct((b, seq, width), BF16),
        scratch_shapes=_resident_scratch(width, k) + _stage_buffers(k)
                       + [pltpu.VMEM((n_dilated, bm, hd), F32)],
        compiler_params=_params("arbitrary"),
        name=name,
    )(*args)


def _ssd_kernel(h0_ref, hn_ref, wt_hbm, cw_ref, cb_ref, dtb_ref, alog_ref, dsk_ref, ng_ref, e_ref,
                o_ref, wb, stage, sem, proj, z_scr, xpad, xs_scr, bb_scr, cc_scr, s_scr, y_scr,
                *, layer, row0, chunks_per_seq):
    q = SSD_CHUNK
    inner = xs_scr.shape[-1]
    bc = bb_scr.shape[-1]
    gw = inner // SSD_GROUPS
    n_state = SSD_STATE
    halo = 8
    step = pl.program_id(0)

    @pl.when(step == 0)
    def _():
        chunk = stage.shape[1]
        _stage_resident(lambda r0: wt_hbm.at[layer, pl.ds(row0 + r0, chunk), :], wb, stage, sem,
                        transpose=True)
        proj[...] = _dot(h0_ref[...], wb[...])

    @pl.when(step % chunks_per_seq == 0)
    def _():
        xpad[:, 0:halo, :] = jnp.zeros((xpad.shape[0], halo, LANE), F32)
        s_scr[...] = jnp.zeros(s_scr.shape, F32)

    x_slabs = inner // LANE
    bc_slabs = bc // LANE
    for j in range(xpad.shape[0]):
        cs = slice(j * LANE, (j + 1) * LANE)
        xpad[j, halo:halo + q, :] = proj[:, inner + j * LANE:inner + (j + 1) * LANE]
        acc = cb_ref[:, cs] + cw_ref[0:1, cs] * xpad[j, halo - CONV_K + 1:halo - CONV_K + 1 + q, :]
        for k in range(1, CONV_K):
            r0 = halo - CONV_K + 1 + k
            acc = acc + cw_ref[k:k + 1, cs] * xpad[j, r0:r0 + q, :]
        xc = _silu(acc)
        if j < x_slabs:
            xs_scr[:, cs] = xc
        elif j < x_slabs + bc_slabs:
            bb_scr[:, (j - x_slabs) * LANE:(j - x_slabs + 1) * LANE] = xc.astype(BF16)
        else:
            jc = j - x_slabs - bc_slabs
            cc_scr[:, jc * LANE:(jc + 1) * LANE] = xc.astype(BF16)
        xpad[j, 0:halo, :] = xpad[j, q:q + halo, :]
    z_scr[...] = proj[:, 0:inner]

    dt_lanes = lax.broadcasted_iota(jnp.int32, (q, LANE), 1) < SSD_HEADS
    dt_col = 2 * inner + 2 * bc
    dt_raw = jnp.where(dt_lanes, proj[:, dt_col:dt_col + LANE], 0.0)

    h_next = hn_ref[...]
    piece = 2 * LANE
    pieces = [(c0, min(c0 + piece, proj.shape[1])) for c0 in range(0, proj.shape[1], piece)]

    def project_next(count):
        for _ in range(min(count, len(pieces))):
            c0, c1 = pieces.pop(0)
            proj[:, c0:c1] = _dot(h_next, wb[:, c0:c1])

    dtr = dt_raw + dtb_ref[...]
    dtv = jnp.where(dt_lanes, jnp.maximum(dtr, 0.0) + jnp.log1p(jnp.exp(-jnp.abs(dtr))), 0.0)
    da = dtv * (-jnp.exp(alog_ref[...]))
    ri = lax.broadcasted_iota(jnp.int32, (q, q), 0)
    ci = lax.broadcasted_iota(jnp.int32, (q, q), 1)
    tri = jnp.where(ri >= ci, 1.0, 0.0).astype(BF16)
    tri_t = jnp.where(ri <= ci, 1.0, 0.0).astype(BF16)
    nh = SSD_HEADS
    da3 = _pack_pieces(da, nh, 3)
    cum3 = _dot(tri, da3.astype(BF16))
    acum = jnp.where(dt_lanes, cum3 + pltpu.roll(cum3, LANE - nh, axis=1)
                     + pltpu.roll(cum3, LANE - 2 * nh, axis=1), 0.0)
    acum2 = acum * LOG2E
    cum3_row = _dot(da3.T.astype(BF16), tri_t)
    acum2_row = (cum3_row[0:nh] + cum3_row[nh:2 * nh] + cum3_row[2 * nh:3 * nh]) * LOG2E
    acum_p = _pack_pieces(acum, nh, 3).astype(BF16)
    dtv_p = _pack_pieces(dtv, nh, 2).astype(BF16)
    hq = q // 2
    causal_h = (lax.broadcasted_iota(jnp.int32, (hq, hq), 0)
                >= lax.broadcasted_iota(jnp.int32, (hq, hq), 1))

    first_lane_half = lax.broadcasted_iota(jnp.int32, (q, LANE), 1) < SSD_HEAD_DIM
    heads_per_group = SSD_HEADS // SSD_GROUPS
    project_next(1)
    for g in range(SSD_GROUPS):
        project_next(1)
        gs = slice(g * gw, (g + 1) * gw)
        e_g = e_ref[:, gs]
        acum_e = _dot(acum_p, e_g)
        dt_e = _dot(dtv_p, e_g)
        xh = xs_scr[:, gs] * dt_e
        xh_b = xh.astype(BF16)
        last_e = acum_e[q - 1:q, :]
        xdec = (xh * jnp.exp(last_e - acum_e)).astype(BF16)
        bg = bb_scr[:, g * n_state:(g + 1) * n_state]
        cg = cc_scr[:, g * n_state:(g + 1) * n_state]
        s_old = s_scr[g]
        y_off = _dot(cg, s_old.astype(BF16)) * jnp.exp(acum_e)
        s_scr[g] = jnp.exp(last_e) * s_old + _dot_tn(bg, xdec)
        cb = _dot_nt(cg, bg)
        cb00, cb10, cb11 = cb[:hq, :hq], cb[hq:, :hq], cb[hq:, hq:]
        tops, bots = [], []
        for hl in range(heads_per_group):
            h = g * heads_per_group + hl
            col = acum2[:, h:h + 1]
            row = acum2_row[h:h + 1, :]
            l00 = jnp.exp2(jnp.where(causal_h, col[:hq] - row[:, :hq], -jnp.inf))
            l10 = jnp.exp2(col[hq:] - row[:, :hq])
            l11 = jnp.exp2(jnp.where(causal_h, col[hq:] - row[:, hq:], -jnp.inf))
            tops.append((cb00 * l00).astype(BF16))
            bots.append(jnp.concatenate([cb10 * l10, cb11 * l11], axis=1).astype(BF16))
            if hl % 2:
                project_next(1)
        for hh in range(heads_per_group // 2):
            xp = xh_b[:, hh * LANE:(hh + 1) * LANE]
            ys = [jnp.concatenate([_dot(tops[2 * hh + e], xp[:hq]), _dot(bots[2 * hh + e], xp)], axis=0)
                  for e in range(2)]
            y_pair = jnp.where(first_lane_half, ys[0], ys[1])
            c0 = g * gw + hh * LANE
            y_scr[:, c0:c0 + LANE] = y_pair + y_off[:, hh * LANE:(hh + 1) * LANE]
    project_next(len(pieces))

    rb = 32
    for r0 in range(0, q, rb):
        rows = slice(r0, r0 + rb)
        y = y_scr[rows, :] + dsk_ref[...] * xs_scr[rows, :]
        gated = y * _silu(z_scr[rows, :])
        ms = jnp.mean(gated * gated, axis=-1, keepdims=True)
        o_ref[rows, :] = ((gated * lax.rsqrt(ms + RMS_EPS)) * ng_ref[...]).astype(o_ref.dtype)


def _ssd_branch(h, w_in_t, layer, col0, seq, conv_w, conv_b, dt_bias, a_log, d_skip, norm_g):
    t, d = h.shape
    q = SSD_CHUNK
    inner = SSD_HEADS * SSD_HEAD_DIM
    bc = SSD_GROUPS * SSD_STATE
    conv_ch = inner + 2 * bc
    width = 2 * inner + 2 * bc + LANE
    pad = LANE - SSD_HEADS
    n_steps = t // q
    assert col0 % 8 == 0 and seq % q == 0
    piece_rows = jnp.arange(LANE)[:, None]
    expand = ((jnp.arange(inner)[None, :] // SSD_HEAD_DIM == piece_rows % SSD_HEADS)
              & (piece_rows < 3 * SSD_HEADS)).astype(BF16)
    row = lambda v: v.reshape(1, -1)
    const = lambda shape: pl.BlockSpec(shape, lambda s: (0,) * len(shape))
    return pl.pallas_call(
        functools.partial(_ssd_kernel, layer=layer, row0=col0, chunks_per_seq=seq // q),
        grid=(n_steps,),
        in_specs=[
            pl.BlockSpec((q, d), lambda s: (0, 0)),
            pl.BlockSpec((q, d), lambda s: (jnp.minimum(s + 1, n_steps - 1), 0)),
            pl.BlockSpec(memory_space=pl.ANY),
            const((CONV_K, conv_ch)), const((1, conv_ch)), const((1, LANE)), const((1, LANE)),
            const((1, inner)), const((1, inner)), const((LANE, inner)),
        ],
        out_specs=pl.BlockSpec((q, inner), lambda s: (s, 0)),
        out_shape=jax.ShapeDtypeStruct((t, inner), BF16),
        scratch_shapes=_resident_scratch(d, width) + _stage_buffers(d, LANE) + [
            pltpu.VMEM((q, width), F32),
            pltpu.VMEM((q, inner), F32),
            pltpu.VMEM((conv_ch // LANE, q + 8, LANE), F32),
            pltpu.VMEM((q, inner), F32),
            pltpu.VMEM((q, bc), BF16),
            pltpu.VMEM((q, bc), BF16),
            pltpu.VMEM((SSD_GROUPS, SSD_STATE, inner // SSD_GROUPS), F32),
            pltpu.VMEM((q, inner), F32),
        ],
        compiler_params=_params("arbitrary"),
        name="ssd_branch",
    )(h, h, w_in_t, conv_w, row(conv_b),
      row(jnp.pad(dt_bias, (0, pad))), row(jnp.pad(a_log, (0, pad))),
      row(jnp.repeat(d_skip, SSD_HEAD_DIM)), row(norm_g), expand)


def _pool_kernel(a_ref, wt_hbm, pw_ref, sc_ref, o_ref, wb, stage, sem, pwb, tail,
                 *, layer, u_row0, z_row0, seq):
    m = pl.program_id(0)
    bm = a_ref.shape[0]
    ng, gw, _ = pwb.shape
    tile_in_seq = m % (seq // bm)

    @pl.when(m == 0)
    def _():
        def hbm_rows(r0):
            g, part, off = r0 // (2 * gw), (r0 // gw) % 2, r0 % gw
            src = (z_row0 if part else u_row0) + g * gw + off
            return wt_hbm.at[layer, pl.ds(src, STAGE_ROWS), :]

        _stage_resident(hbm_rows, wb, stage, sem)
        for g in range(ng):
            pwb[g] = pw_ref[g].astype(BF16)

    a = a_ref[...]
    pos = tile_in_seq * bm + lax.broadcasted_iota(jnp.int32, (bm, gw), 0)
    for g, w in enumerate(POOL_WINDOWS):
        cs = slice(g * gw, (g + 1) * gw)
        acc = _dot_nt(a, wb[2 * g * gw:2 * (g + 1) * gw, :])
        u = acc[:, :gw]
        hist = jnp.where(tile_in_seq == 0, 0.0, tail[:, cs])
        s = jnp.concatenate([hist, u], axis=0)
        k = 1
        while k < w:
            s = s + pltpu.roll(s, k, axis=0)
            k *= 2
        count = jnp.minimum(pos + 1, w).astype(F32)
        pooled = s[POOL_HALO:, :] / count - u
        mixed = _dot(pooled.astype(BF16), pwb[g])
        o_ref[:, cs] = (mixed * sc_ref[:, cs] * _silu(acc[:, gw:])).astype(o_ref.dtype)
        tail[:, cs] = u[bm - POOL_HALO:, :]


def _pool_branch(h, w_in_t, layer, u_col, z_col, seq, pool_w, pool_scale, bm=1024):
    t, k = h.shape
    _, ng, gw, _ = pool_w.shape
    width = ng * gw
    assert u_col % 8 == 0 and z_col % 8 == 0 and gw % STAGE_ROWS == 0 and seq % bm == 0
    return pl.pallas_call(
        functools.partial(_pool_kernel, layer=layer, u_row0=u_col, z_row0=z_col, seq=seq),
        grid=(t // bm,),
        in_specs=[pl.BlockSpec((bm, k), lambda m: (m, 0)),
                  pl.BlockSpec(memory_space=pl.ANY),
                  pl.BlockSpec((None, ng, gw, gw), lambda m: (layer, 0, 0, 0),
                               pipeline_mode=pl.Buffered(1)),
                  pl.BlockSpec((1, width), lambda m: (0, 0))],
        out_specs=pl.BlockSpec((bm, width), lambda m: (m, 0)),
        out_shape=jax.ShapeDtypeStruct((t, width), BF16),
        scratch_shapes=_resident_scratch(2 * width, k) + _stage_buffers(k)
                       + [pltpu.VMEM((ng, gw, gw), BF16), pltpu.VMEM((POOL_HALO, width), F32)],
        compiler_params=_params("arbitrary"),
        name="pool_branch",
    )(h, w_in_t, pool_w, pool_scale.reshape(1, width))


def _attn_kernel(q0, q1, q2, k0, k1, k2, v0, v1, v2, z_ref, o_ref, o0, o1, o2, l0, l1, l2):
    seq, hd = q0.shape
    blk = ATTN_BLOCK
    scale2 = hd ** -0.5 * LOG2E
    ln2 = 1.0 / LOG2E
    qs, ks, vs = (q0, q1, q2), (k0, k1, k2), (v0, v1, v2)
    o_scrs, l_scrs = (o0, o1, o2), (l0, l1, l2)
    qi = lax.broadcasted_iota(jnp.int32, (blk, 2 * blk), 0)
    ci = lax.broadcasted_iota(jnp.int32, (blk, 2 * blk), 1)
    band = (ci >= qi) & (ci <= qi + blk)
    diag = (lax.broadcasted_iota(jnp.int32, (blk, blk), 1)
            <= lax.broadcasted_iota(jnp.int32, (blk, blk), 0))

    n_blocks = seq // blk

    def key_rows(n, nbs):
        first = n % nbs == 0
        return (slice(n * blk, (n + 1) * blk), diag) if first else \
               (slice((n - 1) * blk, (n + 1) * blk), band)

    for g, (_, dil) in enumerate(ATTN_PATTERNS):
        nbs = seq // dil // blk
        scores = []
        for n in range(n_blocks):
            krows, _ = key_rows(n, nbs)
            scores.append(_dot_nt(qs[g][n * blk:(n + 1) * blk, :], ks[g][krows, :]))
        probs, lses = [], []
        for n in range(n_blocks):
            _, mask = key_rows(n, nbs)
            s2 = jnp.where(mask, scores[n] * scale2, -jnp.inf)
            m2 = jnp.max(s2, axis=-1, keepdims=True)
            e = jnp.exp2(s2 - m2)
            den = jnp.sum(e, axis=-1, keepdims=True)
            probs.append((e / den).astype(BF16))
            lses.append(m2 * ln2 + jnp.log(den))
        for n in range(n_blocks):
            krows, _ = key_rows(n, nbs)
            tok = (pl.ds((n % nbs) * blk * dil + n // nbs, blk, stride=dil) if dil > 1
                   else slice(n * blk, (n + 1) * blk))
            o_scrs[g][tok, :] = _dot(probs[n], vs[g][krows, :])
            l_scrs[g][tok, :] = jnp.broadcast_to(lses[n], (blk, hd))

    rb = 64
    for r0 in range(0, seq, rb):
        rows = slice(r0, r0 + rb)
        la, lb, lc = l0[rows, :], l1[rows, :], l2[rows, :]
        mx = jnp.maximum(jnp.maximum(la, lb), lc)
        wa, wb, wc = jnp.exp(la - mx), jnp.exp(lb - mx), jnp.exp(lc - mx)
        tot = wa + wb + wc
        comb = (wa / tot) * o0[rows, :] + (wb / tot) * o1[rows, :] + (wc / tot) * o2[rows, :]
        o_ref[rows, :] = (comb * _silu(z_ref[rows, :])).astype(o_ref.dtype)


def _attn_branch(q3, k3, v3, z3):
    b, seq, _ = q3.shape
    hd = ATTN_HEAD_DIM
    hpg = ATTN_HEADS_PER_GROUP
    ngroups = len(ATTN_PATTERNS)

    def head_spec(g):
        return pl.BlockSpec((None, seq, hd), lambda bi, j: (bi, 0, g * hpg + j))

    in_specs = [head_spec(g) for _ in range(3) for g in range(ngroups)]
    in_specs.append(pl.BlockSpec((None, seq, hd), lambda bi, j: (bi, 0, j)))
    scratch = [pltpu.VMEM((seq, hd), F32)] * (2 * ngroups)
    return pl.pallas_call(
        _attn_kernel,
        grid=(b, hpg),
        in_specs=in_specs,
        out_specs=pl.BlockSpec((None, seq, hd), lambda bi, j: (bi, 0, j)),
        out_shape=jax.ShapeDtypeStruct((b, seq, hpg * hd), BF16),
        scratch_shapes=scratch,
        compiler_params=_params("parallel", "parallel"),
        name="attn_branch",
    )(*([q3] * ngroups + [k3] * ngroups + [v3] * ngroups), z3)


def _merge_kernel(ys_ref, yp_ref, ya_ref, w1_ref, w2_ref, w3_ref, g0_ref, g1_ref, g2_ref, o_ref,
                  wb1, wb2, wb3):
    @pl.when(pl.program_id(1) == 0)
    def _():
        _stage_weight(w1_ref, wb1)
        _stage_weight(w2_ref, wb2)
        _stage_weight(w3_ref, wb3)

    m = _sigmoid(g0_ref[...]) * _dot(ys_ref[...], wb1[...])
    m = m + _sigmoid(g1_ref[...]) * _dot(yp_ref[...], wb2[...])
    m = m + _sigmoid(g2_ref[...]) * _dot(ya_ref[...], wb3[...])
    o_ref[...] = m.astype(o_ref.dtype)


def _merge(y_ssd, y_pool, y_attn, w1, w2, w3, layer, gates2, bm=512, bn=1024):
    t, d = y_ssd.shape
    n = w1.shape[-1]
    gate_blocks = n // bn
    once = pl.Buffered(1)

    def gate_spec(i):
        return pl.BlockSpec((bm, bn), lambda ni, mi: (mi, i * gate_blocks + ni))

    def w_spec(w):
        return pl.BlockSpec((None, w.shape[1], bn), lambda ni, mi: (layer, 0, ni), pipeline_mode=once)

    return pl.pallas_call(
        _merge_kernel,
        grid=(n // bn, t // bm),
        in_specs=[pl.BlockSpec((bm, d), lambda ni, mi: (mi, 0)),
                  pl.BlockSpec((bm, y_pool.shape[1]), lambda ni, mi: (mi, 0)),
                  pl.BlockSpec((bm, y_attn.shape[1]), lambda ni, mi: (mi, 0)),
                  w_spec(w1), w_spec(w2), w_spec(w3),
                  gate_spec(0), gate_spec(1), gate_spec(2)],
        out_specs=pl.BlockSpec((bm, bn), lambda ni, mi: (mi, ni)),
        out_shape=jax.ShapeDtypeStruct((t, n), BF16),
        scratch_shapes=[pltpu.VMEM((w1.shape[1], bn), BF16), pltpu.VMEM((w2.shape[1], bn), BF16),
                        pltpu.VMEM((w3.shape[1], bn), BF16)],
        compiler_params=_params("arbitrary", "arbitrary"),
        name="gated_merge",
    )(y_ssd, y_pool, y_attn, w1, w2, w3, gates2, gates2, gates2)


def _tail_kernel(*refs, layer, emit_h):
    if emit_h:
        (mg_ref, x_ref, p_ref, wo_hbm, wg_hbm, wp_ref, gple_ref, gnext_ref,
         xo_ref, ho_ref, wob, wgb, wpb, stage, sem) = refs
    else:
        (mg_ref, x_ref, p_ref, wo_hbm, wg_hbm, wp_ref, gple_ref,
         xo_ref, wob, wgb, wpb, stage, sem) = refs

    @pl.when(pl.program_id(0) == 0)
    def _():
        _stage_resident(lambda r0: wo_hbm.at[layer, pl.ds(r0, STAGE_ROWS), :], wob, stage, sem)
        _stage_resident(lambda r0: wg_hbm.at[layer, pl.ds(r0, STAGE_ROWS), :], wgb, stage, sem)
        wpb[...] = wp_ref[...].astype(BF16)

    def rms(v, g_ref):
        ms = jnp.mean(v * v, axis=-1, keepdims=True)
        return ((v * lax.rsqrt(ms + RMS_EPS)) * g_ref[...]).astype(BF16)

    x1 = x_ref[...] + _dot(mg_ref[...], wob[...])
    gate = _sigmoid(_dot(rms(x1, gple_ref), wgb[...]))
    x2 = x1 + gate * _dot(p_ref[...].astype(BF16), wpb[...])
    xo_ref[...] = x2
    if emit_h:
        ho_ref[...] = rms(x2, gnext_ref)


def _layer_tail(merged, x2d, p, w_out, w_gate, w_proj, layer, ple_g, next_g, bm=512):
    t, d = x2d.shape
    kp = p.shape[-1]
    emit_h = next_g is not None
    once = pl.Buffered(1)
    row = lambda i: (i, 0)
    in_specs = [pl.BlockSpec((bm, d), row), pl.BlockSpec((bm, d), row),
                pl.BlockSpec((None, bm, kp), lambda i: (layer, i, 0)),
                pl.BlockSpec(memory_space=pl.ANY), pl.BlockSpec(memory_space=pl.ANY),
                pl.BlockSpec((None, kp, d), lambda i: (layer, 0, 0), pipeline_mode=once),
                pl.BlockSpec((1, d), lambda i: (0, 0))]
    args = [merged, x2d, p, w_out, w_gate, w_proj, ple_g.reshape(1, d)]
    out_specs = [pl.BlockSpec((bm, d), row)]
    out_shape = [jax.ShapeDtypeStruct((t, d), F32)]
    if emit_h:
        in_specs.append(pl.BlockSpec((1, d), lambda i: (0, 0)))
        args.append(next_g.reshape(1, d))
        out_specs.append(pl.BlockSpec((bm, d), row))
        out_shape.append(jax.ShapeDtypeStruct((t, d), BF16))
    outs = pl.pallas_call(
        functools.partial(_tail_kernel, layer=layer, emit_h=emit_h),
        grid=(t // bm,),
        in_specs=in_specs,
        out_specs=out_specs,
        out_shape=out_shape,
        scratch_shapes=_resident_scratch(d, d) + _resident_scratch(d, d) + _resident_scratch(kp, d)
                       + _stage_buffers(d),
        compiler_params=_params("arbitrary", vmem=VMEM_LIMIT_HIGH),
        name="layer_tail",
    )(*args)
    return (outs[0], outs[1]) if emit_h else (outs[0], None)


def _in_proj_columns(d_model):
    inner = SSD_HEADS * SSD_HEAD_DIM
    bc = SSD_GROUPS * SSD_STATE
    attn_w = len(ATTN_PATTERNS) * ATTN_HEADS_PER_GROUP * ATTN_HEAD_DIM
    attn_out = ATTN_HEADS_PER_GROUP * ATTN_HEAD_DIM
    sizes = [("z_ssd", inner), ("x", inner), ("B", bc), ("C", bc), ("dt", SSD_HEADS),
             ("u_pool", d_model), ("z_pool", d_model), ("q", attn_w), ("k", attn_w), ("v", attn_w),
             ("z_attn", attn_out), ("gates", 3 * d_model)]
    col, start = {}, 0
    for name, size in sizes:
        col[name] = start
        start += size
    col["end"] = start
    return col


def kernel(x, p, norm_g, w_in, conv_w, conv_b, dt_bias, a_log, d_skip, ssd_norm_g, w_br_ssd, pool_w,
           pool_scale, w_br_pool, q_norm_g, k_norm_g, w_br_attn, w_out, ple_norm_g, w_ple_gate,
           w_ple_proj):
    b, seq, d = x.shape
    depth = w_in.shape[0]
    t = b * seq
    col = _in_proj_columns(d)
    assert col["end"] == w_in.shape[-1]
    x2 = x.reshape(t, d)
    p2 = p.reshape(depth, t, -1)
    w_in_t = jnp.swapaxes(w_in, 1, 2)
    h = _rms_norm_bf16(x2, norm_g[0])
    for i in range(depth):
        q3 =_head_proj(h, w_in_t, i, col["q"], b, seq, q_norm_g[i], "q_proj")
        k3 = _head_proj(h, w_in_t, i, col["k"], b, seq, k_norm_g[i], "k_proj")
        v3 = _head_proj(h, w_in_t, i, col["v"], b, seq, None, "v_proj")
        z_attn = _in_proj(h, w_in_t, i, col["z_attn"], col["gates"] - col["z_attn"], 1024, 512, "in_proj_zattn")
        gates = _in_proj(h, w_in_t, i, col["gates"], col["end"] - col["gates"], 1024, 1024, "in_proj_gates")

        assert col["dt"] - col["z_ssd"] == 2 * SSD_HEADS * SSD_HEAD_DIM + 2 * SSD_GROUPS * SSD_STATE
        y_ssd = _ssd_branch(h, w_in_t, i, col["z_ssd"], seq, conv_w[i], conv_b[i], dt_bias[i],
                            a_log[i], d_skip[i], ssd_norm_g[i])
        y_pool = _pool_branch(h, w_in_t, i, col["u_pool"], col["z_pool"], seq, pool_w, pool_scale[i])
        y_attn = _attn_branch(q3, k3, v3, z_attn.reshape(b, seq, -1))

        merged = _merge(y_ssd.reshape(t, -1), y_pool.reshape(t, -1), y_attn.reshape(t, -1),
                        w_br_ssd, w_br_pool, w_br_attn, i, gates)
        next_g = norm_g[i + 1] if i + 1 < depth else None
        x2, h = _layer_tail(merged, x2, p2, w_out, w_ple_gate, w_ple_proj, i, ple_norm_g[i], next_g)
    return x2.reshape(b, seq, d)
```

```python
import functools

import jax
import jax.numpy as jnp
from jax import lax
from jax.experimental import pallas as pl
from jax.experimental.pallas import tpu as pltpu

F32 = jnp.float32
BF16 = jnp.bfloat16

RMS_EPS = 1e-6
LOG2E = 1.4426950408889634
SSD_HEAD_DIM = 64
SSD_HEADS = 32
SSD_GROUPS = 4
SSD_STATE = 128
SSD_CHUNK = 256
CONV_K = 4
POOL_WINDOWS = (2, 4, 8, 16)
POOL_HALO = 16
ATTN_PATTERNS = ((128, 1), (512, 4), (2048, 16))
ATTN_HEAD_DIM = 128
ATTN_HEADS_PER_GROUP = 4
ATTN_BLOCK = 128

LANE = 128
VMEM_LIMIT = 56 * 1024 * 1024
VMEM_LIMIT_HIGH = 58 * 1024 * 1024
STAGE_ROWS = 256


def _params(*sem, vmem=VMEM_LIMIT):
    return pltpu.CompilerParams(dimension_semantics=sem, vmem_limit_bytes=vmem)


def _sigmoid(v):
    return 0.5 * jnp.tanh(0.5 * v) + 0.5


def _silu(v):
    half = 0.5 * v
    return half * jnp.tanh(half) + half


def _dot(a, b):
    return jnp.dot(a, b, preferred_element_type=F32)


def _dot_nt(a, b):
    return lax.dot_general(a, b, (((1,), (1,)), ((), ())), preferred_element_type=F32)


def _dot_tn(a, b):
    return lax.dot_general(a, b, (((0,), (0,)), ((), ())), preferred_element_type=F32)


def _pack_pieces(v, width, n_pieces):
    packed = None
    rest = v
    for k in range(n_pieces):
        piece = rest.astype(BF16).astype(F32)
        rest = rest - piece
        shifted = piece if k == 0 else pltpu.roll(piece, k * width, axis=1)
        packed = shifted if packed is None else packed + shifted
    return packed


def _stage_weight(w_ref, dst):
    rows_total = dst.shape[0]
    step = min(STAGE_ROWS, rows_total)

    def body(c, carry):
        rows = pl.ds(pl.multiple_of(c * step, step), step)
        dst[rows, :] = w_ref[rows, :].astype(BF16)
        return carry

    lax.fori_loop(0, rows_total // step, body, 0)


def _fetch_weight_rows(wt_hbm, wf, wb, sem, layer, row0):
    n = pl.program_id(0)
    n_tiles = pl.num_programs(0)
    bn = wb.shape[0]

    def copy(tile, slot):
        rows = pl.ds(pl.multiple_of(row0 + tile * bn, 8), bn)
        return pltpu.make_async_copy(wt_hbm.at[layer, rows, :], wf.at[slot], sem.at[slot])

    @pl.when(pl.program_id(1) == 0)
    def _():
        slot = n % 2

        @pl.when(n == 0)
        def _():
            copy(0, 0).start()

        copy(n, slot).wait()

        @pl.when(n + 1 < n_tiles)
        def _():
            copy(n + 1, 1 - slot).start()

        _stage_weight(wf.at[slot], wb)


def _weight_scratch(bn, k):
    return [pltpu.VMEM((2, bn, k), F32), pltpu.VMEM((bn, k), BF16), pltpu.SemaphoreType.DMA((2,))]


def _stage_resident(hbm_rows, wb, stage, sem, transpose=False):
    chunk = stage.shape[1]
    n_chunks = (wb.shape[1] if transpose else wb.shape[0]) // chunk

    def copy(c):
        return pltpu.make_async_copy(hbm_rows(c * chunk), stage.at[c % 2], sem.at[c % 2])

    copy(0).start()
    for c in range(n_chunks):
        if c + 1 < n_chunks:
            copy(c + 1).start()
        copy(c).wait()
        if transpose:
            wb[:, c * chunk:(c + 1) * chunk] = stage[c % 2].T.astype(BF16)
        else:
            wb[c * chunk:(c + 1) * chunk, :] = stage[c % 2].astype(BF16)


def _resident_scratch(rows, cols):
    return [pltpu.VMEM((rows, cols), BF16)]


def _stage_buffers(cols, rows=STAGE_ROWS):
    return [pltpu.VMEM((2, rows, cols), F32), pltpu.SemaphoreType.DMA((2,))]


def _rms_kernel(x_ref, g_ref, o_ref):
    x = x_ref[...]
    ms = jnp.mean(x * x, axis=-1, keepdims=True)
    o_ref[...] = ((x * lax.rsqrt(ms + RMS_EPS)) * g_ref[...]).astype(o_ref.dtype)


def _rms_norm_bf16(x2d, g, bm=512):
    t, d = x2d.shape
    return pl.pallas_call(
        _rms_kernel,
        grid=(t // bm,),
        in_specs=[pl.BlockSpec((bm, d), lambda i: (i, 0)),
                  pl.BlockSpec((1, d), lambda i: (0, 0))],
        out_specs=pl.BlockSpec((bm, d), lambda i: (i, 0)),
        out_shape=jax.ShapeDtypeStruct((t, d), BF16),
        compiler_params=_params("parallel"),
        name="rms_norm",
    )(x2d, g.reshape(1, d))


def _proj_kernel(a_ref, wt_hbm, o_ref, wf, wb, sem, *, layer, row0):
    _fetch_weight_rows(wt_hbm, wf, wb, sem, layer, row0)
    o_ref[...] = _dot_nt(a_ref[...], wb[...]).astype(o_ref.dtype)


def _in_proj(h, w_in_t, layer, col0, width, bm, bn, name):
    t, k = h.shape
    assert col0 % 8 == 0 and width % bn == 0
    return pl.pallas_call(
        functools.partial(_proj_kernel, layer=layer, row0=col0),
        grid=(width // bn, t // bm),
        in_specs=[pl.BlockSpec((bm, k), lambda n, m: (m, 0)),
                  pl.BlockSpec(memory_space=pl.ANY)],
        out_specs=pl.BlockSpec((bm, bn), lambda n, m: (m, n)),
        out_shape=jax.ShapeDtypeStruct((t, width), F32),
        scratch_shapes=_weight_scratch(bn, k),
        compiler_params=_params("arbitrary", "arbitrary"),
        name=name,
    )(h, w_in_t)


def _head_proj_kernel(*refs, layer, row0, seq, normalize):
    if normalize:
        a_ref, wt_hbm, g_ref, o_ref, wb, stage, sem, acc_scr = refs
    else:
        a_ref, wt_hbm, o_ref, wb, stage, sem, acc_scr = refs
    m = pl.program_id(0)
    bm = a_ref.shape[0]
    hd = ATTN_HEAD_DIM
    gw = ATTN_HEADS_PER_GROUP * hd
    half = m % (seq // bm)

    @pl.when(m == 0)
    def _():
        _stage_resident(lambda r0: wt_hbm.at[layer, pl.ds(row0 + r0, STAGE_ROWS), :], wb, stage, sem)

    a = a_ref[...]
    accs = [_dot_nt(a, wb[g * gw:(g + 1) * gw, :]) for g in range(len(ATTN_PATTERNS))]
    slab = 0
    for g, (_, dil) in enumerate(ATTN_PATTERNS):
        acc = accs[g]
        cnt = bm // dil
        ld = seq // dil
        for hh in range(ATTN_HEADS_PER_GROUP):
            cs = slice(g * gw + hh * hd, g * gw + (hh + 1) * hd)
            v = acc[:, hh * hd:(hh + 1) * hd]
            if normalize:
                ms = jnp.mean(v * v, axis=-1, keepdims=True)
                v = (v * lax.rsqrt(ms + RMS_EPS)) * g_ref[...]
            if dil == 1:
                o_ref[pl.ds(pl.multiple_of(half * bm, bm), bm), cs] = v.astype(BF16)
            else:
                acc_scr[slab] = v
                for r in range(dil):
                    rows = acc_scr[slab, pl.ds(r, cnt, stride=dil), :]
                    dst = r * ld + pl.multiple_of(half * cnt, cnt)
                    o_ref[pl.ds(dst, cnt), cs] = rows.astype(BF16)
                slab += 1


def _head_proj(h, w_in_t, layer, col0, b, seq, norm_g, name, bm=1024):
    t, k = h.shape
    hd = ATTN_HEAD_DIM
    width = len(ATTN_PATTERNS) * ATTN_HEADS_PER_GROUP * hd
    n_dilated = sum(ATTN_HEADS_PER_GROUP for _, dil in ATTN_PATTERNS if dil > 1)
    tiles_per_seq = seq // bm
    assert col0 % 8 == 0 and width % STAGE_ROWS == 0
    normalize = norm_g is not None
    in_specs = [pl.BlockSpec((bm, k), lambda m: (m, 0)), pl.BlockSpec(memory_space=pl.ANY)]
    args = [h, w_in_t]
    if normalize:
        in_specs.append(pl.BlockSpec((1, hd), lambda m: (0, 0)))
        args.append(norm_g.reshape(1, hd))
    return pl.pallas_call(
        functools.partial(_head_proj_kernel, layer=layer, row0=col0, seq=seq, normalize=normalize),
        grid=(t // bm,),
        in_specs=in_specs,
        out_specs=pl.BlockSpec((None, seq, width), lambda m: (m // tiles_per_seq, 0, 0)),
        out_shape=jax.ShapeDtypeStruct((b, seq, width), BF16),
        scratch_shapes=_resident_scratch(width, k) + _stage_buffers(k)
                       + [pltpu.VMEM((n_dilated, bm, hd), F32)],
        compiler_params=_params("arbitrary"),
        name=name,
    )(*args)


def _ssd_kernel(h0_ref, hn_ref, wt_hbm, cw_ref, cb_ref, dtb_ref, alog_ref, dsk_ref, ng_ref, e_ref,
                o_ref, wb, stage, sem, proj, z_scr, xpad, xs_scr, bb_scr, cc_scr, s_scr, y_scr,
                *, layer, row0, chunks_per_seq):
    q = SSD_CHUNK
    inner = xs_scr.shape[-1]
    bc = bb_scr.shape[-1]
    gw = inner // SSD_GROUPS
    n_state = SSD_STATE
    halo = 8
    step = pl.program_id(0)

    @pl.when(step == 0)
    def _():
        chunk = stage.shape[1]
        _stage_resident(lambda r0: wt_hbm.at[layer, pl.ds(row0 + r0, chunk), :], wb, stage, sem,
                        transpose=True)
        proj[...] = _dot(h0_ref[...], wb[...])

    @pl.when(step % chunks_per_seq == 0)
    def _():
        xpad[:, 0:halo, :] = jnp.zeros((xpad.shape[0], halo, LANE), F32)
        s_scr[...] = jnp.zeros(s_scr.shape, F32)

    x_slabs = inner // LANE
    bc_slabs = bc // LANE
    for j in range(xpad.shape[0]):
        cs = slice(j * LANE, (j + 1) * LANE)
        xpad[j, halo:halo + q, :] = proj[:, inner + j * LANE:inner + (j + 1) * LANE]
        acc = cb_ref[:, cs] + cw_ref[0:1, cs] * xpad[j, halo - CONV_K + 1:halo - CONV_K + 1 + q, :]
        for k in range(1, CONV_K):
            r0 = halo - CONV_K + 1 + k
            acc = acc + cw_ref[k:k + 1, cs] * xpad[j, r0:r0 + q, :]
        xc = _silu(acc)
        if j < x_slabs:
            xs_scr[:, cs] = xc
        elif j < x_slabs + bc_slabs:
            bb_scr[:, (j - x_slabs) * LANE:(j - x_slabs + 1) * LANE] = xc.astype(BF16)
        else:
            jc = j - x_slabs - bc_slabs
            cc_scr[:, jc * LANE:(jc + 1) * LANE] = xc.astype(BF16)
        xpad[j, 0:halo, :] = xpad[j, q:q + halo, :]
    z_scr[...] = proj[:, 0:inner]

    dt_lanes = lax.broadcasted_iota(jnp.int32, (q, LANE), 1) < SSD_HEADS
    dt_col = 2 * inner + 2 * bc
    dt_raw = jnp.where(dt_lanes, proj[:, dt_col:dt_col + LANE], 0.0)

    h_next = hn_ref[...]
    piece = 2 * LANE
    pieces = [(c0, min(c0 + piece, proj.shape[1])) for c0 in range(0, proj.shape[1], piece)]

    def project_next(count):
        for _ in range(min(count, len(pieces))):
            c0, c1 = pieces.pop(0)
            proj[:, c0:c1] = _dot(h_next, wb[:, c0:c1])

    dtr = dt_raw + dtb_ref[...]
    dtv = jnp.where(dt_lanes, jnp.maximum(dtr, 0.0) + jnp.log1p(jnp.exp(-jnp.abs(dtr))), 0.0)
    da = dtv * (-jnp.exp(alog_ref[...]))
    ri = lax.broadcasted_iota(jnp.int32, (q, q), 0)
    ci = lax.broadcasted_iota(jnp.int32, (q, q), 1)
    tri = jnp.where(ri >= ci, 1.0, 0.0).astype(BF16)
    tri_t = jnp.where(ri <= ci, 1.0, 0.0).astype(BF16)
    nh = SSD_HEADS
    da3 = _pack_pieces(da, nh, 3)
    cum3 = _dot(tri, da3.astype(BF16))
    acum = jnp.where(dt_lanes, cum3 + pltpu.roll(cum3, LANE - nh, axis=1)
                     + pltpu.roll(cum3, LANE - 2 * nh, axis=1), 0.0)
    acum2 = acum * LOG2E
    cum3_row = _dot(da3.T.astype(BF16), tri_t)
    acum2_row = (cum3_row[0:nh] + cum3_row[nh:2 * nh] + cum3_row[2 * nh:3 * nh]) * LOG2E
    acum_p = _pack_pieces(acum, nh, 3).astype(BF16)
    dtv_p = _pack_pieces(dtv, nh, 2).astype(BF16)
    hq = q // 2
    causal_h = (lax.broadcasted_iota(jnp.int32, (hq, hq), 0)
                >= lax.broadcasted_iota(jnp.int32, (hq, hq), 1))

    first_lane_half = lax.broadcasted_iota(jnp.int32, (q, LANE), 1) < SSD_HEAD_DIM
    heads_per_group = SSD_HEADS // SSD_GROUPS
    project_next(1)
    for g in range(SSD_GROUPS):
        project_next(1)
        gs = slice(g * gw, (g + 1) * gw)
        e_g = e_ref[:, gs]
        bg = bb_scr[:, g * n_state:(g + 1) * n_state]
        cg = cc_scr[:, g * n_state:(g + 1) * n_state]
        s_old = s_scr[g]
        acum_e = _dot(acum_p, e_g)
        dt_e = _dot(dtv_p, e_g)
        xh = xs_scr[:, gs] * dt_e
        xh_b = xh.astype(BF16)
        last_e = acum_e[q - 1:q, :]
        xdec = (xh * jnp.exp(last_e - acum_e)).astype(BF16)
        y_off = _dot(cg, s_old.astype(BF16)) * jnp.exp(acum_e)
        s_scr[g] = jnp.exp(last_e) * s_old + _dot_tn(bg, xdec)
        cb = _dot_nt(cg, bg)
        cb00, cb10, cb11 = cb[:hq, :hq], cb[hq:, :hq], cb[hq:, hq:]
        tops, bots = [], []
        for hl in range(heads_per_group):
            h = g * heads_per_group + hl
            col = acum2[:, h:h + 1]
            row = acum2_row[h:h + 1, :]
            l00 = jnp.exp2(jnp.where(causal_h, col[:hq] - row[:, :hq], -jnp.inf))
            l10 = jnp.exp2(col[hq:] - row[:, :hq])
            l11 = jnp.exp2(jnp.where(causal_h, col[hq:] - row[:, hq:], -jnp.inf))
            tops.append((cb00 * l00).astype(BF16))
            bots.append(jnp.concatenate([cb10 * l10, cb11 * l11], axis=1).astype(BF16))
            if hl % 2:
                project_next(1)
        for hh in range(heads_per_group // 2):
            xp = xh_b[:, hh * LANE:(hh + 1) * LANE]
            ys = [jnp.concatenate([_dot(tops[2 * hh + e], xp[:hq]), _dot(bots[2 * hh + e], xp)], axis=0)
                  for e in range(2)]
            y_pair = jnp.where(first_lane_half, ys[0], ys[1])
            c0 = g * gw + hh * LANE
            y_scr[:, c0:c0 + LANE] = y_pair + y_off[:, hh * LANE:(hh + 1) * LANE]
    project_next(len(pieces))

    rb = 32
    for r0 in range(0, q, rb):
        rows = slice(r0, r0 + rb)
        y = y_scr[rows, :] + dsk_ref[...] * xs_scr[rows, :]
        gated = y * _silu(z_scr[rows, :])
        ms = jnp.mean(gated * gated, axis=-1, keepdims=True)
        o_ref[rows, :] = ((gated * lax.rsqrt(ms + RMS_EPS)) * ng_ref[...]).astype(o_ref.dtype)


def _ssd_branch(h, w_in_t, layer, col0, seq, conv_w, conv_b, dt_bias, a_log, d_skip, norm_g):
    t, d = h.shape
    q = SSD_CHUNK
    inner = SSD_HEADS * SSD_HEAD_DIM
    bc = SSD_GROUPS * SSD_STATE
    conv_ch = inner + 2 * bc
    width = 2 * inner + 2 * bc + LANE
    pad = LANE - SSD_HEADS
    n_steps = t // q
    assert col0 % 8 == 0 and seq % q == 0
    piece_rows = jnp.arange(LANE)[:, None]
    expand = ((jnp.arange(inner)[None, :] // SSD_HEAD_DIM == piece_rows % SSD_HEADS)
              & (piece_rows < 3 * SSD_HEADS)).astype(BF16)
    row = lambda v: v.reshape(1, -1)
    const = lambda shape: pl.BlockSpec(shape, lambda s: (0,) * len(shape))
    return pl.pallas_call(
        functools.partial(_ssd_kernel, layer=layer, row0=col0, chunks_per_seq=seq // q),
        grid=(n_steps,),
        in_specs=[
            pl.BlockSpec((q, d), lambda s: (0, 0)),
            pl.BlockSpec((q, d), lambda s: (jnp.minimum(s + 1, n_steps - 1), 0)),
            pl.BlockSpec(memory_space=pl.ANY),
            const((CONV_K, conv_ch)), const((1, conv_ch)), const((1, LANE)), const((1, LANE)),
            const((1, inner)), const((1, inner)), const((LANE, inner)),
        ],
        out_specs=pl.BlockSpec((q, inner), lambda s: (s, 0)),
        out_shape=jax.ShapeDtypeStruct((t, inner), BF16),
        scratch_shapes=_resident_scratch(d, width) + _stage_buffers(d, LANE) + [
            pltpu.VMEM((q, width), F32),
            pltpu.VMEM((q, inner), F32),
            pltpu.VMEM((conv_ch // LANE, q + 8, LANE), F32),
            pltpu.VMEM((q, inner), F32),
            pltpu.VMEM((q, bc), BF16),
            pltpu.VMEM((q, bc), BF16),
            pltpu.VMEM((SSD_GROUPS, SSD_STATE, inner // SSD_GROUPS), F32),
            pltpu.VMEM((q, inner), F32),
        ],
        compiler_params=_params("arbitrary"),
        name="ssd_branch",
    )(h, h, w_in_t, conv_w, row(conv_b),
      row(jnp.pad(dt_bias, (0, pad))), row(jnp.pad(a_log, (0, pad))),
      row(jnp.repeat(d_skip, SSD_HEAD_DIM)), row(norm_g), expand)


def _pool_kernel(a_ref, wt_hbm, pw_ref, sc_ref, o_ref, wb, stage, sem, pwb, tail,
                 *, layer, u_row0, z_row0, seq):
    m = pl.program_id(0)
    bm = a_ref.shape[0]
    ng, gw, _ = pwb.shape
    tile_in_seq = m % (seq // bm)

    @pl.when(m == 0)
    def _():
        def hbm_rows(r0):
            g, part, off = r0 // (2 * gw), (r0 // gw) % 2, r0 % gw
            src = (z_row0 if part else u_row0) + g * gw + off
            return wt_hbm.at[layer, pl.ds(src, STAGE_ROWS), :]

        _stage_resident(hbm_rows, wb, stage, sem)
        for g in range(ng):
            pwb[g] = pw_ref[g].astype(BF16)

    a = a_ref[...]
    pos = tile_in_seq * bm + lax.broadcasted_iota(jnp.int32, (bm, gw), 0)
    for g, w in enumerate(POOL_WINDOWS):
        cs = slice(g * gw, (g + 1) * gw)
        acc = _dot_nt(a, wb[2 * g * gw:2 * (g + 1) * gw, :])
        u = acc[:, :gw]
        hist = jnp.where(tile_in_seq == 0, 0.0, tail[:, cs])
        s = jnp.concatenate([hist, u], axis=0)
        k = 1
        while k < w:
            s = s + pltpu.roll(s, k, axis=0)
            k *= 2
        count = jnp.minimum(pos + 1, w).astype(F32)
        pooled = s[POOL_HALO:, :] / count - u
        mixed = _dot(pooled.astype(BF16), pwb[g])
        o_ref[:, cs] = (mixed * sc_ref[:, cs] * _silu(acc[:, gw:])).astype(o_ref.dtype)
        tail[:, cs] = u[bm - POOL_HALO:, :]


def _pool_branch(h, w_in_t, layer, u_col, z_col, seq, pool_w, pool_scale, bm=1024):
    t, k = h.shape
    _, ng, gw, _ = pool_w.shape
    width = ng * gw
    assert u_col % 8 == 0 and z_col % 8 == 0 and gw % STAGE_ROWS == 0 and seq % bm == 0
    return pl.pallas_call(
        functools.partial(_pool_kernel, layer=layer, u_row0=u_col, z_row0=z_col, seq=seq),
        grid=(t // bm,),
        in_specs=[pl.BlockSpec((bm, k), lambda m: (m, 0)),
                  pl.BlockSpec(memory_space=pl.ANY),
                  pl.BlockSpec((None, ng, gw, gw), lambda m: (layer, 0, 0, 0),
                               pipeline_mode=pl.Buffered(1)),
                  pl.BlockSpec((1, width), lambda m: (0, 0))],
        out_specs=pl.BlockSpec((bm, width), lambda m: (m, 0)),
        out_shape=jax.ShapeDtypeStruct((t, width), BF16),
        scratch_shapes=_resident_scratch(2 * width, k) + _stage_buffers(k)
                       + [pltpu.VMEM((ng, gw, gw), BF16), pltpu.VMEM((POOL_HALO, width), F32)],
        compiler_params=_params("arbitrary"),
        name="pool_branch",
    )(h, w_in_t, pool_w, pool_scale.reshape(1, width))


def _attn_kernel(q0, q1, q2, k0, k1, k2, v0, v1, v2, z_ref, o_ref, o0, o1, o2, l0, l1, l2):
    seq, hd = q0.shape
    blk = ATTN_BLOCK
    scale2 = hd ** -0.5 * LOG2E
    ln2 = 1.0 / LOG2E
    qs, ks, vs = (q0, q1, q2), (k0, k1, k2), (v0, v1, v2)
    o_scrs, l_scrs = (o0, o1, o2), (l0, l1, l2)
    qi = lax.broadcasted_iota(jnp.int32, (blk, 2 * blk), 0)
    ci = lax.broadcasted_iota(jnp.int32, (blk, 2 * blk), 1)
    band = (ci >= qi) & (ci <= qi + blk)
    diag = (lax.broadcasted_iota(jnp.int32, (blk, blk), 1)
            <= lax.broadcasted_iota(jnp.int32, (blk, blk), 0))

    n_blocks = seq // blk

    def key_rows(n, nbs):
        first = n % nbs == 0
        return (slice(n * blk, (n + 1) * blk), diag) if first else \
               (slice((n - 1) * blk, (n + 1) * blk), band)

    dils = [dil for _, dil in ATTN_PATTERNS]
    nbss = [seq // dil // blk for dil in dils]
    scores, probs, lses = {}, {}, {}

    def score_phase(g):
        scores[g] = [_dot_nt(qs[g][n * blk:(n + 1) * blk, :], ks[g][key_rows(n, nbss[g])[0], :])
                     for n in range(n_blocks)]

    def softmax_phase(g):
        probs[g], lses[g] = [], []
        for n in range(n_blocks):
            s2 = jnp.where(key_rows(n, nbss[g])[1], scores[g][n] * scale2, -jnp.inf)
            m2 = jnp.max(s2, axis=-1, keepdims=True)
            e = jnp.exp2(s2 - m2)
            den = jnp.sum(e, axis=-1, keepdims=True)
            probs[g].append((e / den).astype(BF16))
            lses[g].append(m2 * ln2 + jnp.log(den))

    def value_phase(g):
        dil, nbs = dils[g], nbss[g]
        for n in range(n_blocks):
            tok = (pl.ds((n % nbs) * blk * dil + n // nbs, blk, stride=dil) if dil > 1
                   else slice(n * blk, (n + 1) * blk))
            o_scrs[g][tok, :] = _dot(probs[g][n], vs[g][key_rows(n, nbs)[0], :])
            l_scrs[g][tok, :] = jnp.broadcast_to(lses[g][n], (blk, hd))

    for g in range(len(ATTN_PATTERNS)):
        score_phase(g)
        softmax_phase(g)
        value_phase(g)

    rb = 64
    for r0 in range(0, seq, rb):
        rows = slice(r0, r0 + rb)
        la, lb, lc = l0[rows, :], l1[rows, :], l2[rows, :]
        mx = jnp.maximum(jnp.maximum(la, lb), lc)
        wa, wb, wc = jnp.exp(la - mx), jnp.exp(lb - mx), jnp.exp(lc - mx)
        tot = wa + wb + wc
        comb = (wa / tot) * o0[rows, :] + (wb / tot) * o1[rows, :] + (wc / tot) * o2[rows, :]
        o_ref[rows, :] = (comb * _silu(z_ref[rows, :])).astype(o_ref.dtype)


def _attn_branch(q3, k3, v3, z3):
    b, seq, _ = q3.shape
    hd = ATTN_HEAD_DIM
    hpg = ATTN_HEADS_PER_GROUP
    ngroups = len(ATTN_PATTERNS)

    def head_spec(g):
        return pl.BlockSpec((None, seq, hd), lambda bi, j: (bi, 0, g * hpg + j))

    in_specs = [head_spec(g) for _ in range(3) for g in range(ngroups)]
    in_specs.append(pl.BlockSpec((None, seq, hd), lambda bi, j: (bi, 0, j)))
    scratch = [pltpu.VMEM((seq, hd), F32)] * (2 * ngroups)
    return pl.pallas_call(
        _attn_kernel,
        grid=(b, hpg),
        in_specs=in_specs,
        out_specs=pl.BlockSpec((None, seq, hd), lambda bi, j: (bi, 0, j)),
        out_shape=jax.ShapeDtypeStruct((b, seq, hpg * hd), BF16),
        scratch_shapes=scratch,
        compiler_params=_params("parallel", "parallel"),
        name="attn_branch",
    )(*([q3] * ngroups + [k3] * ngroups + [v3] * ngroups), z3)


def _merge_kernel(ys_ref, yp_ref, ya_ref, w1_ref, w2_ref, w3_ref, g0_ref, g1_ref, g2_ref, o_ref,
                  wb1, wb2, wb3):
    @pl.when(pl.program_id(1) == 0)
    def _():
        _stage_weight(w1_ref, wb1)
        _stage_weight(w2_ref, wb2)
        _stage_weight(w3_ref, wb3)

    d1 = _dot(ys_ref[...], wb1[...])
    d2 = _dot(yp_ref[...], wb2[...])
    d3 = _dot(ya_ref[...], wb3[...])
    m = _sigmoid(g0_ref[...]) * d1
    m = m + _sigmoid(g1_ref[...]) * d2
    m = m + _sigmoid(g2_ref[...]) * d3
    o_ref[...] = m.astype(o_ref.dtype)


def _merge(y_ssd, y_pool, y_attn, w1, w2, w3, layer, gates2, bm=512, bn=1024):
    t, d = y_ssd.shape
    n = w1.shape[-1]
    gate_blocks = n // bn
    once = pl.Buffered(1)

    def gate_spec(i):
        return pl.BlockSpec((bm, bn), lambda ni, mi: (mi, i * gate_blocks + ni))

    def w_spec(w):
        return pl.BlockSpec((None, w.shape[1], bn), lambda ni, mi: (layer, 0, ni), pipeline_mode=once)

    return pl.pallas_call(
        _merge_kernel,
        grid=(n // bn, t // bm),
        in_specs=[pl.BlockSpec((bm, d), lambda ni, mi: (mi, 0)),
                  pl.BlockSpec((bm, y_pool.shape[1]), lambda ni, mi: (mi, 0)),
                  pl.BlockSpec((bm, y_attn.shape[1]), lambda ni, mi: (mi, 0)),
                  w_spec(w1), w_spec(w2), w_spec(w3),
                  gate_spec(0), gate_spec(1), gate_spec(2)],
        out_specs=pl.BlockSpec((bm, bn), lambda ni, mi: (mi, ni)),
        out_shape=jax.ShapeDtypeStruct((t, n), BF16),
        scratch_shapes=[pltpu.VMEM((w1.shape[1], bn), BF16), pltpu.VMEM((w2.shape[1], bn), BF16),
                        pltpu.VMEM((w3.shape[1], bn), BF16)],
        compiler_params=_params("arbitrary", "arbitrary", vmem=VMEM_LIMIT_HIGH),
        name="gated_merge",
    )(y_ssd, y_pool, y_attn, w1, w2, w3, gates2, gates2, gates2)


def _tail_kernel(*refs, layer, emit_h):
    if emit_h:
        (mg_ref, x_ref, p_ref, wo_hbm, wg_hbm, wp_ref, gple_ref, gnext_ref,
         xo_ref, ho_ref, wob, wgb, wpb, stage, sem) = refs
    else:
        (mg_ref, x_ref, p_ref, wo_hbm, wg_hbm, wp_ref, gple_ref,
         xo_ref, wob, wgb, wpb, stage, sem) = refs

    @pl.when(pl.program_id(0) == 0)
    def _():
        _stage_resident(lambda r0: wo_hbm.at[layer, pl.ds(r0, STAGE_ROWS), :], wob, stage, sem)
        _stage_resident(lambda r0: wg_hbm.at[layer, pl.ds(r0, STAGE_ROWS), :], wgb, stage, sem)
        wpb[...] = wp_ref[...].astype(BF16)

    def rms(v, g_ref):
        ms = jnp.mean(v * v, axis=-1, keepdims=True)
        return ((v * lax.rsqrt(ms + RMS_EPS)) * g_ref[...]).astype(BF16)

    x1 = x_ref[...] + _dot(mg_ref[...], wob[...])
    gate = _sigmoid(_dot(rms(x1, gple_ref), wgb[...]))
    x2 = x1 + gate * _dot(p_ref[...].astype(BF16), wpb[...])
    xo_ref[...] = x2
    if emit_h:
        ho_ref[...] = rms(x2, gnext_ref)


def _layer_tail(merged, x2d, p, w_out, w_gate, w_proj, layer, ple_g, next_g, bm=512):
    t, d = x2d.shape
    kp = p.shape[-1]
    emit_h = next_g is not None
    once = pl.Buffered(1)
    row = lambda i: (i, 0)
    in_specs = [pl.BlockSpec((bm, d), row), pl.BlockSpec((bm, d), row),
                pl.BlockSpec((None, bm, kp), lambda i: (layer, i, 0)),
                pl.BlockSpec(memory_space=pl.ANY), pl.BlockSpec(memory_space=pl.ANY),
                pl.BlockSpec((None, kp, d), lambda i: (layer, 0, 0), pipeline_mode=once),
                pl.BlockSpec((1, d), lambda i: (0, 0))]
    args = [merged, x2d, p, w_out, w_gate, w_proj, ple_g.reshape(1, d)]
    out_specs = [pl.BlockSpec((bm, d), row)]
    out_shape = [jax.ShapeDtypeStruct((t, d), F32)]
    if emit_h:
        in_specs.append(pl.BlockSpec((1, d), lambda i: (0, 0)))
        args.append(next_g.reshape(1, d))
        out_specs.append(pl.BlockSpec((bm, d), row))
        out_shape.append(jax.ShapeDtypeStruct((t, d), BF16))
    outs = pl.pallas_call(
        functools.partial(_tail_kernel, layer=layer, emit_h=emit_h),
        grid=(t // bm,),
        in_specs=in_specs,
        out_specs=out_specs,
        out_shape=out_shape,
        scratch_shapes=_resident_scratch(d, d) + _resident_scratch(d, d) + _resident_scratch(kp, d)
                       + _stage_buffers(d),
        compiler_params=_params("arbitrary", vmem=VMEM_LIMIT_HIGH),
        name="layer_tail",
    )(*args)
    return (outs[0], outs[1]) if emit_h else (outs[0], None)


def _in_proj_columns(d_model):
    inner = SSD_HEADS * SSD_HEAD_DIM
    bc = SSD_GROUPS * SSD_STATE
    attn_w = len(ATTN_PATTERNS) * ATTN_HEADS_PER_GROUP * ATTN_HEAD_DIM
    attn_out = ATTN_HEADS_PER_GROUP * ATTN_HEAD_DIM
    sizes = [("z_ssd", inner), ("x", inner), ("B", bc), ("C", bc), ("dt", SSD_HEADS),
             ("u_pool", d_model), ("z_pool", d_model), ("q", attn_w), ("k", attn_w), ("v", attn_w),
             ("z_attn", attn_out), ("gates", 3 * d_model)]
    col, start = {}, 0
    for name, size in sizes:
        col[name] = start
        start += size
    col["end"] = start
    return col


def kernel(x, p, norm_g, w_in, conv_w, conv_b, dt_bias, a_log, d_skip, ssd_norm_g, w_br_ssd, pool_w,
           pool_scale, w_br_pool, q_norm_g, k_norm_g, w_br_attn, w_out, ple_norm_g, w_ple_gate,
           w_ple_proj):
    b, seq, d = x.shape
    depth = w_in.shape[0]
    t = b * seq
    col = _in_proj_columns(d)
    assert col["end"] == w_in.shape[-1]
    x2 = x.reshape(t, d)
    p2 = p.reshape(depth, t, -1)
    w_in_t = jnp.swapaxes(w_in, 1, 2)
    h = _rms_norm_bf16(x2, norm_g[0])
    for i in range(depth):
        q3 =_head_proj(h, w_in_t, i, col["q"], b, seq, q_norm_g[i], "q_proj")
        k3 = _head_proj(h, w_in_t, i, col["k"], b, seq, k_norm_g[i], "k_proj")
        v3 = _head_proj(h, w_in_t, i, col["v"], b, seq, None, "v_proj")
        z_attn = _in_proj(h, w_in_t, i, col["z_attn"], col["gates"] - col["z_attn"], 1024, 512, "in_proj_zattn")
        gates = _in_proj(h, w_in_t, i, col["gates"], col["end"] - col["gates"], 1024, 1024, "in_proj_gates")

        assert col["dt"] - col["z_ssd"] == 2 * SSD_HEADS * SSD_HEAD_DIM + 2 * SSD_GROUPS * SSD_STATE
        y_ssd = _ssd_branch(h, w_in_t, i, col["z_ssd"], seq, conv_w[i], conv_b[i], dt_bias[i],
                            a_log[i], d_skip[i], ssd_norm_g[i])
        y_pool = _pool_branch(h, w_in_t, i, col["u_pool"], col["z_pool"], seq, pool_w, pool_scale[i])
        y_attn = _attn_branch(q3, k3, v3, z_attn.reshape(b, seq, -1))

        merged = _merge(y_ssd.reshape(t, -1), y_pool.reshape(t, -1), y_attn.reshape(t, -1),
                        w_br_ssd, w_br_pool, w_br_attn, i, gates)
        next_g = norm_g[i + 1] if i + 1 < depth else None
        x2, h = _layer_tail(merged, x2, p2, w_out, w_ple_gate, w_ple_proj, i, ple_norm_g[i], next_g)
    return x2.reshape(b, seq, d)
```

```python
import functools

import jax
import jax.numpy as jnp
from jax import lax
from jax.experimental import pallas as pl
from jax.experimental.pallas import tpu as pltpu

F32 = jnp.float32
BF16 = jnp.bfloat16

RMS_EPS = 1e-6
LOG2E = 1.4426950408889634
SSD_HEAD_DIM = 64
SSD_HEADS = 32
SSD_GROUPS = 4
SSD_STATE = 128
SSD_CHUNK = 256
CONV_K = 4
POOL_WINDOWS = (2, 4, 8, 16)
POOL_HALO = 16
ATTN_PATTERNS = ((128, 1), (512, 4), (2048, 16))
ATTN_HEAD_DIM = 128
ATTN_HEADS_PER_GROUP = 4
ATTN_BLOCK = 128

LANE = 128
VMEM_LIMIT = 56 * 1024 * 1024
VMEM_LIMIT_HIGH = 58 * 1024 * 1024
STAGE_ROWS = 256


def _params(*sem, vmem=VMEM_LIMIT):
    return pltpu.CompilerParams(dimension_semantics=sem, vmem_limit_bytes=vmem)


def _sigmoid(v):
    return 0.5 * jnp.tanh(0.5 * v) + 0.5


def _silu(v):
    half = 0.5 * v
    return half * jnp.tanh(half) + half


def _dot(a, b):
    return jnp.dot(a, b, preferred_element_type=F32)


def _dot_nt(a, b):
    return lax.dot_general(a, b, (((1,), (1,)), ((), ())), preferred_element_type=F32)


def _dot_tn(a, b):
    return lax.dot_general(a, b, (((0,), (0,)), ((), ())), preferred_element_type=F32)


def _pack_pieces(v, width, n_pieces):
    packed = None
    rest = v
    for k in range(n_pieces):
        piece = rest.astype(BF16).astype(F32)
        rest = rest - piece
        shifted = piece if k == 0 else pltpu.roll(piece, k * width, axis=1)
        packed = shifted if packed is None else packed + shifted
    return packed


def _stage_weight(w_ref, dst):
    rows_total = dst.shape[0]
    step = min(STAGE_ROWS, rows_total)

    def body(c, carry):
        rows = pl.ds(pl.multiple_of(c * step, step), step)
        dst[rows, :] = w_ref[rows, :].astype(BF16)
        return carry

    lax.fori_loop(0, rows_total // step, body, 0)


def _fetch_weight_rows(wt_hbm, wf, wb, sem, layer, row0):
    n = pl.program_id(0)
    n_tiles = pl.num_programs(0)
    bn = wb.shape[0]

    def copy(tile, slot):
        rows = pl.ds(pl.multiple_of(row0 + tile * bn, 8), bn)
        return pltpu.make_async_copy(wt_hbm.at[layer, rows, :], wf.at[slot], sem.at[slot])

    @pl.when(pl.program_id(1) == 0)
    def _():
        slot = n % 2

        @pl.when(n == 0)
        def _():
            copy(0, 0).start()

        copy(n, slot).wait()

        @pl.when(n + 1 < n_tiles)
        def _():
            copy(n + 1, 1 - slot).start()

        _stage_weight(wf.at[slot], wb)


def _weight_scratch(bn, k):
    return [pltpu.VMEM((2, bn, k), F32), pltpu.VMEM((bn, k), BF16), pltpu.SemaphoreType.DMA((2,))]


def _stage_resident(hbm_rows, wb, stage, sem, transpose=False):
    chunk = stage.shape[1]
    n_chunks = (wb.shape[1] if transpose else wb.shape[0]) // chunk

    def copy(c):
        return pltpu.make_async_copy(hbm_rows(c * chunk), stage.at[c % 2], sem.at[c % 2])

    copy(0).start()
    for c in range(n_chunks):
        if c + 1 < n_chunks:
            copy(c + 1).start()
        copy(c).wait()
        if transpose:
            wb[:, c * chunk:(c + 1) * chunk] = stage[c % 2].T.astype(BF16)
        else:
            wb[c * chunk:(c + 1) * chunk, :] = stage[c % 2].astype(BF16)


def _resident_scratch(rows, cols):
    return [pltpu.VMEM((rows, cols), BF16)]


def _stage_buffers(cols, rows=STAGE_ROWS):
    return [pltpu.VMEM((2, rows, cols), F32), pltpu.SemaphoreType.DMA((2,))]


def _rms_kernel(x_ref, g_ref, o_ref):
    x = x_ref[...]
    ms = jnp.mean(x * x, axis=-1, keepdims=True)
    o_ref[...] = ((x * lax.rsqrt(ms + RMS_EPS)) * g_ref[...]).astype(o_ref.dtype)


def _rms_norm_bf16(x2d, g, bm=512):
    t, d = x2d.shape
    return pl.pallas_call(
        _rms_kernel,
        grid=(t // bm,),
        in_specs=[pl.BlockSpec((bm, d), lambda i: (i, 0)),
                  pl.BlockSpec((1, d), lambda i: (0, 0))],
        out_specs=pl.BlockSpec((bm, d), lambda i: (i, 0)),
        out_shape=jax.ShapeDtypeStruct((t, d), BF16),
        compiler_params=_params("parallel"),
        name="rms_norm",
    )(x2d, g.reshape(1, d))


def _proj_kernel(a_ref, wt_hbm, o_ref, wf, wb, sem, *, layer, row0):
    _fetch_weight_rows(wt_hbm, wf, wb, sem, layer, row0)
    o_ref[...] = _dot_nt(a_ref[...], wb[...]).astype(o_ref.dtype)


def _in_proj(h, w_in_t, layer, col0, width, bm, bn, name):
    t, k = h.shape
    assert col0 % 8 == 0 and width % bn == 0
    return pl.pallas_call(
        functools.partial(_proj_kernel, layer=layer, row0=col0),
        grid=(width // bn, t // bm),
        in_specs=[pl.BlockSpec((bm, k), lambda n, m: (m, 0)),
                  pl.BlockSpec(memory_space=pl.ANY)],
        out_specs=pl.BlockSpec((bm, bn), lambda n, m: (m, n)),
        out_shape=jax.ShapeDtypeStruct((t, width), F32),
        scratch_shapes=_weight_scratch(bn, k),
        compiler_params=_params("arbitrary", "arbitrary"),
        name=name,
    )(h, w_in_t)


def _head_proj_kernel(*refs, layer, row0, seq, normalize):
    if normalize:
        a_ref, wt_hbm, g_ref, o_ref, wb, stage, sem, acc_scr = refs
    else:
        a_ref, wt_hbm, o_ref, wb, stage, sem, acc_scr = refs
    m = pl.program_id(0)
    bm = a_ref.shape[0]
    hd = ATTN_HEAD_DIM
    gw = ATTN_HEADS_PER_GROUP * hd
    half = m % (seq // bm)

    @pl.when(m == 0)
    def _():
        _stage_resident(lambda r0: wt_hbm.at[layer, pl.ds(row0 + r0, STAGE_ROWS), :], wb, stage, sem)

    a = a_ref[...]
    order = sorted(range(len(ATTN_PATTERNS)), key=lambda gi: -ATTN_PATTERNS[gi][1])
    slab = 0
    for g in order:
        dil = ATTN_PATTERNS[g][1]
        acc = _dot_nt(a, wb[g * gw:(g + 1) * gw, :])
        cnt = bm // dil
        ld = seq // dil
        for hh in range(ATTN_HEADS_PER_GROUP):
            cs = slice(g * gw + hh * hd, g * gw + (hh + 1) * hd)
            v = acc[:, hh * hd:(hh + 1) * hd]
            if normalize:
                ms = jnp.mean(v * v, axis=-1, keepdims=True)
                v = (v * lax.rsqrt(ms + RMS_EPS)) * g_ref[...]
            if dil == 1:
                o_ref[pl.ds(pl.multiple_of(half * bm, bm), bm), cs] = v.astype(BF16)
            else:
                acc_scr[slab] = v
                for r in range(dil):
                    rows = acc_scr[slab, pl.ds(r, cnt, stride=dil), :]
                    dst = r * ld + pl.multiple_of(half * cnt, cnt)
                    o_ref[pl.ds(dst, cnt), cs] = rows.astype(BF16)
                slab += 1


def _head_proj(h, w_in_t, layer, col0, b, seq, norm_g, name, bm=1024):
    t, k = h.shape
    hd = ATTN_HEAD_DIM
    width = len(ATTN_PATTERNS) * ATTN_HEADS_PER_GROUP * hd
    n_dilated = sum(ATTN_HEADS_PER_GROUP for _, dil in ATTN_PATTERNS if dil > 1)
    tiles_per_seq = seq // bm
    assert col0 % 8 == 0 and width % STAGE_ROWS == 0
    normalize = norm_g is not None
    in_specs = [pl.BlockSpec((bm, k), lambda m: (m, 0)), pl.BlockSpec(memory_space=pl.ANY)]
    args = [h, w_in_t]
    if normalize:
        in_specs.append(pl.BlockSpec((1, hd), lambda m: (0, 0)))
        args.append(norm_g.reshape(1, hd))
    return pl.pallas_call(
        functools.partial(_head_proj_kernel, layer=layer, row0=col0, seq=seq, normalize=normalize),
        grid=(t // bm,),
        in_specs=in_specs,
        out_specs=pl.BlockSpec((None, seq, width), lambda m: (m // tiles_per_seq, 0, 0)),
        out_shape=jax.ShapeDtypeStruct((b, seq, width), BF16),
        scratch_shapes=_resident_scratch(width, k) + _stage_buffers(k)
                       + [pltpu.VMEM((n_dilated, bm, hd), F32)],
        compiler_params=_params("arbitrary"),
        name=name,
    )(*args)


def _ssd_kernel(h0_ref, hn_ref, wt_hbm, cw_ref, cb_ref, dtb_ref, alog_ref, dsk_ref, ng_ref, e_ref,
                o_ref, wb, stage, sem, proj, z_scr, xpad, xs_scr, bb_scr, cc_scr, s_scr, y_scr,
                *, layer, row0, chunks_per_seq):
    q = SSD_CHUNK
    inner = xs_scr.shape[-1]
    bc = bb_scr.shape[-1]
    gw = inner // SSD_GROUPS
    n_state = SSD_STATE
    halo = 8
    step = pl.program_id(0)

    @pl.when(step == 0)
    def _():
        chunk = stage.shape[1]
        _stage_resident(lambda r0: wt_hbm.at[layer, pl.ds(row0 + r0, chunk), :], wb, stage, sem,
                        transpose=True)
        proj[...] = _dot(h0_ref[...], wb[...])

    @pl.when(step % chunks_per_seq == 0)
    def _():
        xpad[:, 0:halo, :] = jnp.zeros((xpad.shape[0], halo, LANE), F32)
        s_scr[...] = jnp.zeros(s_scr.shape, F32)

    x_slabs = inner // LANE
    bc_slabs = bc // LANE
    for j in range(xpad.shape[0]):
        cs = slice(j * LANE, (j + 1) * LANE)
        xpad[j, halo:halo + q, :] = proj[:, inner + j * LANE:inner + (j + 1) * LANE]
        acc = cb_ref[:, cs] + cw_ref[0:1, cs] * xpad[j, halo - CONV_K + 1:halo - CONV_K + 1 + q, :]
        for k in range(1, CONV_K):
            r0 = halo - CONV_K + 1 + k
            acc = acc + cw_ref[k:k + 1, cs] * xpad[j, r0:r0 + q, :]
        xc = _silu(acc)
        if j < x_slabs:
            xs_scr[:, cs] = xc
        elif j < x_slabs + bc_slabs:
            bb_scr[:, (j - x_slabs) * LANE:(j - x_slabs + 1) * LANE] = xc.astype(BF16)
        else:
            jc = j - x_slabs - bc_slabs
            cc_scr[:, jc * LANE:(jc + 1) * LANE] = xc.astype(BF16)
        xpad[j, 0:halo, :] = xpad[j, q:q + halo, :]
    z_scr[...] = proj[:, 0:inner]

    dt_lanes = lax.broadcasted_iota(jnp.int32, (q, LANE), 1) < SSD_HEADS
    dt_col = 2 * inner + 2 * bc
    dt_raw = jnp.where(dt_lanes, proj[:, dt_col:dt_col + LANE], 0.0)

    h_next = hn_ref[...]
    piece = 2 * LANE
    pieces = [(c0, min(c0 + piece, proj.shape[1])) for c0 in range(0, proj.shape[1], piece)]

    def project_next(count):
        for _ in range(min(count, len(pieces))):
            c0, c1 = pieces.pop(0)
            proj[:, c0:c1] = _dot(h_next, wb[:, c0:c1])

    dtr = dt_raw + dtb_ref[...]
    dtv = jnp.where(dt_lanes, jnp.maximum(dtr, 0.0) + jnp.log1p(jnp.exp(-jnp.abs(dtr))), 0.0)
    da = dtv * (-jnp.exp(alog_ref[...]))
    ri = lax.broadcasted_iota(jnp.int32, (q, q), 0)
    ci = lax.broadcasted_iota(jnp.int32, (q, q), 1)
    tri = jnp.where(ri >= ci, 1.0, 0.0).astype(BF16)
    tri_t = jnp.where(ri <= ci, 1.0, 0.0).astype(BF16)
    nh = SSD_HEADS
    da3 = _pack_pieces(da, nh, 3)
    cum3 = _dot(tri, da3.astype(BF16))
    acum = jnp.where(dt_lanes, cum3 + pltpu.roll(cum3, LANE - nh, axis=1)
                     + pltpu.roll(cum3, LANE - 2 * nh, axis=1), 0.0)
    acum2 = acum * LOG2E
    cum3_row = _dot(da3.T.astype(BF16), tri_t)
    acum2_row = (cum3_row[0:nh] + cum3_row[nh:2 * nh] + cum3_row[2 * nh:3 * nh]) * LOG2E
    acum_p = _pack_pieces(acum, nh, 3).astype(BF16)
    dtv_p = _pack_pieces(dtv, nh, 2).astype(BF16)
    hq = q // 2
    causal_h = (lax.broadcasted_iota(jnp.int32, (hq, hq), 0)
                >= lax.broadcasted_iota(jnp.int32, (hq, hq), 1))

    first_lane_half = lax.broadcasted_iota(jnp.int32, (q, LANE), 1) < SSD_HEAD_DIM
    heads_per_group = SSD_HEADS // SSD_GROUPS
    project_next(1)

    def prepare(g):
        project_next(1)
        gs = slice(g * gw, (g + 1) * gw)
        e_g = e_ref[:, gs]
        bg = bb_scr[:, g * n_state:(g + 1) * n_state]
        cg = cc_scr[:, g * n_state:(g + 1) * n_state]
        s_old = s_scr[g]
        acum_e = _dot(acum_p, e_g)
        dt_e = _dot(dtv_p, e_g)
        xh = xs_scr[:, gs] * dt_e
        xh_b = xh.astype(BF16)
        last_e = acum_e[q - 1:q, :]
        xdec = (xh * jnp.exp(last_e - acum_e)).astype(BF16)
        y_off = _dot(cg, s_old.astype(BF16)) * jnp.exp(acum_e)
        s_scr[g] = jnp.exp(last_e) * s_old + _dot_tn(bg, xdec)
        cb = _dot_nt(cg, bg)
        cb00, cb10, cb11 = cb[:hq, :hq], cb[hq:, :hq], cb[hq:, hq:]
        tops, bots = [], []
        for hl in range(heads_per_group):
            h = g * heads_per_group + hl
            col = acum2[:, h:h + 1]
            row = acum2_row[h:h + 1, :]
            l00 = jnp.exp2(jnp.where(causal_h, col[:hq] - row[:, :hq], -jnp.inf))
            l10 = jnp.exp2(col[hq:] - row[:, :hq])
            l11 = jnp.exp2(jnp.where(causal_h, col[hq:] - row[:, hq:], -jnp.inf))
            tops.append(jnp.concatenate([cb00 * l00, jnp.zeros((hq, hq), F32)], axis=1).astype(BF16))
            bots.append(jnp.concatenate([cb10 * l10, cb11 * l11], axis=1).astype(BF16))
            if hl % 2:
                project_next(1)
        return xh_b, y_off, tops, bots

    def finish(g, prepared):
        xh_b, y_off, tops, bots = prepared
        for hh in range(heads_per_group // 2):
            xp = xh_b[:, hh * LANE:(hh + 1) * LANE]
            stacked = jnp.concatenate([tops[2 * hh], bots[2 * hh], tops[2 * hh + 1], bots[2 * hh + 1]],
                                      axis=0)
            both = _dot(stacked, xp)
            y_pair = jnp.where(first_lane_half, both[:q], both[q:])
            c0 = g * gw + hh * LANE
            y_scr[:, c0:c0 + LANE] = y_pair + y_off[:, hh * LANE:(hh + 1) * LANE]

    for g in range(SSD_GROUPS):
        finish(g, prepare(g))
    project_next(len(pieces))

    rb = 32
    for r0 in range(0, q, rb):
        rows = slice(r0, r0 + rb)
        y = y_scr[rows, :] + dsk_ref[...] * xs_scr[rows, :]
        gated = y * _silu(z_scr[rows, :])
        ms = jnp.mean(gated * gated, axis=-1, keepdims=True)
        o_ref[rows, :] = ((gated * lax.rsqrt(ms + RMS_EPS)) * ng_ref[...]).astype(o_ref.dtype)


def _ssd_branch(h, w_in_t, layer, col0, seq, conv_w, conv_b, dt_bias, a_log, d_skip, norm_g):
    t, d = h.shape
    q = SSD_CHUNK
    inner = SSD_HEADS * SSD_HEAD_DIM
    bc = SSD_GROUPS * SSD_STATE
    conv_ch = inner + 2 * bc
    width = 2 * inner + 2 * bc + LANE
    pad = LANE - SSD_HEADS
    n_steps = t // q
    assert col0 % 8 == 0 and seq % q == 0
    piece_rows = jnp.arange(LANE)[:, None]
    expand = ((jnp.arange(inner)[None, :] // SSD_HEAD_DIM == piece_rows % SSD_HEADS)
              & (piece_rows < 3 * SSD_HEADS)).astype(BF16)
    row = lambda v: v.reshape(1, -1)
    const = lambda shape: pl.BlockSpec(shape, lambda s: (0,) * len(shape))
    return pl.pallas_call(
        functools.partial(_ssd_kernel, layer=layer, row0=col0, chunks_per_seq=seq // q),
        grid=(n_steps,),
        in_specs=[
            pl.BlockSpec((q, d), lambda s: (0, 0)),
            pl.BlockSpec((q, d), lambda s: (jnp.minimum(s + 1, n_steps - 1), 0)),
            pl.BlockSpec(memory_space=pl.ANY),
            const((CONV_K, conv_ch)), const((1, conv_ch)), const((1, LANE)), const((1, LANE)),
            const((1, inner)), const((1, inner)), const((LANE, inner)),
        ],
        out_specs=pl.BlockSpec((q, inner), lambda s: (s, 0)),
        out_shape=jax.ShapeDtypeStruct((t, inner), BF16),
        scratch_shapes=_resident_scratch(d, width) + _stage_buffers(d, LANE) + [
            pltpu.VMEM((q, width), F32),
            pltpu.VMEM((q, inner), F32),
            pltpu.VMEM((conv_ch // LANE, q + 8, LANE), F32),
            pltpu.VMEM((q, inner), F32),
            pltpu.VMEM((q, bc), BF16),
            pltpu.VMEM((q, bc), BF16),
            pltpu.VMEM((SSD_GROUPS, SSD_STATE, inner // SSD_GROUPS), F32),
            pltpu.VMEM((q, inner), F32),
        ],
        compiler_params=_params("arbitrary"),
        name="ssd_branch",
    )(h, h, w_in_t, conv_w, row(conv_b),
      row(jnp.pad(dt_bias, (0, pad))), row(jnp.pad(a_log, (0, pad))),
      row(jnp.repeat(d_skip, SSD_HEAD_DIM)), row(norm_g), expand)


def _pool_kernel(a_ref, wt_hbm, pw_ref, sc_ref, o_ref, wb, stage, sem, pwb, tail,
                 *, layer, u_row0, z_row0, seq):
    m = pl.program_id(0)
    bm = a_ref.shape[0]
    ng, gw, _ = pwb.shape
    tile_in_seq = m % (seq // bm)

    @pl.when(m == 0)
    def _():
        def hbm_rows(r0):
            g, part, off = r0 // (2 * gw), (r0 // gw) % 2, r0 % gw
            src = (z_row0 if part else u_row0) + g * gw + off
            return wt_hbm.at[layer, pl.ds(src, STAGE_ROWS), :]

        _stage_resident(hbm_rows, wb, stage, sem)
        for g in range(ng):
            pwb[g] = pw_ref[g].astype(BF16)

    a = a_ref[...]
    pos = tile_in_seq * bm + lax.broadcasted_iota(jnp.int32, (bm, gw), 0)
    for g, w in enumerate(POOL_WINDOWS):
        cs = slice(g * gw, (g + 1) * gw)
        acc = _dot_nt(a, wb[2 * g * gw:2 * (g + 1) * gw, :])
        u = acc[:, :gw]
        hist = jnp.where(tile_in_seq == 0, 0.0, tail[:, cs])
        s = jnp.concatenate([hist, u], axis=0)
        k = 1
        while k < w:
            s = s + pltpu.roll(s, k, axis=0)
            k *= 2
        count = jnp.minimum(pos + 1, w).astype(F32)
        pooled = s[POOL_HALO:, :] / count - u
        mixed = _dot(pooled.astype(BF16), pwb[g])
        o_ref[:, cs] = (mixed * sc_ref[:, cs] * _silu(acc[:, gw:])).astype(o_ref.dtype)
        tail[:, cs] = u[bm - POOL_HALO:, :]


def _pool_branch(h, w_in_t, layer, u_col, z_col, seq, pool_w, pool_scale, bm=1024):
    t, k = h.shape
    _, ng, gw, _ = pool_w.shape
    width = ng * gw
    assert u_col % 8 == 0 and z_col % 8 == 0 and gw % STAGE_ROWS == 0 and seq % bm == 0
    return pl.pallas_call(
        functools.partial(_pool_kernel, layer=layer, u_row0=u_col, z_row0=z_col, seq=seq),
        grid=(t // bm,),
        in_specs=[pl.BlockSpec((bm, k), lambda m: (m, 0)),
                  pl.BlockSpec(memory_space=pl.ANY),
                  pl.BlockSpec((None, ng, gw, gw), lambda m: (layer, 0, 0, 0),
                               pipeline_mode=pl.Buffered(1)),
                  pl.BlockSpec((1, width), lambda m: (0, 0))],
        out_specs=pl.BlockSpec((bm, width), lambda m: (m, 0)),
        out_shape=jax.ShapeDtypeStruct((t, width), BF16),
        scratch_shapes=_resident_scratch(2 * width, k) + _stage_buffers(k)
                       + [pltpu.VMEM((ng, gw, gw), BF16), pltpu.VMEM((POOL_HALO, width), F32)],
        compiler_params=_params("arbitrary"),
        name="pool_branch",
    )(h, w_in_t, pool_w, pool_scale.reshape(1, width))


def _attn_kernel(q0, q1, q2, k0, k1, k2, v0, v1, v2, z_ref, o_ref, o0, o1, o2, l0, l1, l2):
    seq, hd = q0.shape
    blk = ATTN_BLOCK
    scale2 = hd ** -0.5 * LOG2E
    ln2 = 1.0 / LOG2E
    qs, ks, vs = (q0, q1, q2), (k0, k1, k2), (v0, v1, v2)
    o_scrs, l_scrs = (o0, o1, o2), (l0, l1, l2)
    qi = lax.broadcasted_iota(jnp.int32, (blk, 2 * blk), 0)
    ci = lax.broadcasted_iota(jnp.int32, (blk, 2 * blk), 1)
    band = (ci >= qi) & (ci <= qi + blk)
    diag = (lax.broadcasted_iota(jnp.int32, (blk, blk), 1)
            <= lax.broadcasted_iota(jnp.int32, (blk, blk), 0))

    n_blocks = seq // blk

    def key_rows(n, nbs):
        first = n % nbs == 0
        return (slice(n * blk, (n + 1) * blk), diag) if first else \
               (slice((n - 1) * blk, (n + 1) * blk), band)

    dils = [dil for _, dil in ATTN_PATTERNS]
    nbss = [seq // dil // blk for dil in dils]
    scores, probs, lses = {}, {}, {}

    def score_phase(g):
        scores[g] = [_dot_nt(qs[g][n * blk:(n + 1) * blk, :], ks[g][key_rows(n, nbss[g])[0], :])
                     for n in range(n_blocks)]

    def softmax_phase(g):
        probs[g], lses[g] = [], []
        for n in range(n_blocks):
            s2 = jnp.where(key_rows(n, nbss[g])[1], scores[g][n] * scale2, -jnp.inf)
            m2 = jnp.max(s2, axis=-1, keepdims=True)
            e = jnp.exp2(s2 - m2)
            den = jnp.sum(e, axis=-1, keepdims=True)
            probs[g].append((e / den).astype(BF16))
            lses[g].append(m2 * ln2 + jnp.log(den))

    def value_phase(g):
        dil, nbs = dils[g], nbss[g]
        for n in range(n_blocks):
            tok = (pl.ds((n % nbs) * blk * dil + n // nbs, blk, stride=dil) if dil > 1
                   else slice(n * blk, (n + 1) * blk))
            o_scrs[g][tok, :] = _dot(probs[g][n], vs[g][key_rows(n, nbs)[0], :])
            l_scrs[g][tok, :] = jnp.broadcast_to(lses[g][n], (blk, hd))

    for g in range(len(ATTN_PATTERNS)):
        score_phase(g)
        softmax_phase(g)
        value_phase(g)

    rb = 64
    for r0 in range(0, seq, rb):
        rows = slice(r0, r0 + rb)
        la, lb, lc = l0[rows, :], l1[rows, :], l2[rows, :]
        mx = jnp.maximum(jnp.maximum(la, lb), lc)
        wa, wb, wc = jnp.exp(la - mx), jnp.exp(lb - mx), jnp.exp(lc - mx)
        tot = wa + wb + wc
        comb = (wa / tot) * o0[rows, :] + (wb / tot) * o1[rows, :] + (wc / tot) * o2[rows, :]
        o_ref[rows, :] = (comb * _silu(z_ref[rows, :])).astype(o_ref.dtype)


def _attn_branch(q3, k3, v3, z3):
    b, seq, _ = q3.shape
    hd = ATTN_HEAD_DIM
    hpg = ATTN_HEADS_PER_GROUP
    ngroups = len(ATTN_PATTERNS)

    def head_spec(g):
        return pl.BlockSpec((None, seq, hd), lambda bi, j: (bi, 0, g * hpg + j))

    in_specs = [head_spec(g) for _ in range(3) for g in range(ngroups)]
    in_specs.append(pl.BlockSpec((None, seq, hd), lambda bi, j: (bi, 0, j)))
    scratch = [pltpu.VMEM((seq, hd), F32)] * (2 * ngroups)
    return pl.pallas_call(
        _attn_kernel,
        grid=(b, hpg),
        in_specs=in_specs,
        out_specs=pl.BlockSpec((None, seq, hd), lambda bi, j: (bi, 0, j)),
        out_shape=jax.ShapeDtypeStruct((b, seq, hpg * hd), BF16),
        scratch_shapes=scratch,
        compiler_params=_params("parallel", "parallel"),
        name="attn_branch",
    )(*([q3] * ngroups + [k3] * ngroups + [v3] * ngroups), z3)


def _merge_kernel(ys_ref, yp_ref, ya_ref, w1_ref, w2_ref, w3_ref, g0_ref, g1_ref, g2_ref, o_ref,
                  wb1, wb2, wb3):
    @pl.when(pl.program_id(1) == 0)
    def _():
        _stage_weight(w1_ref, wb1)
        _stage_weight(w2_ref, wb2)
        _stage_weight(w3_ref, wb3)

    d1 = _dot(ys_ref[...], wb1[...])
    d2 = _dot(yp_ref[...], wb2[...])
    d3 = _dot(ya_ref[...], wb3[...])
    m = _sigmoid(g0_ref[...]) * d1
    m = m + _sigmoid(g1_ref[...]) * d2
    m = m + _sigmoid(g2_ref[...]) * d3
    o_ref[...] = m.astype(o_ref.dtype)


def _merge(y_ssd, y_pool, y_attn, w1, w2, w3, layer, gates2, bm=512, bn=1024):
    t, d = y_ssd.shape
    n = w1.shape[-1]
    gate_blocks = n // bn
    once = pl.Buffered(1)

    def gate_spec(i):
        return pl.BlockSpec((bm, bn), lambda ni, mi: (mi, i * gate_blocks + ni))

    def w_spec(w):
        return pl.BlockSpec((None, w.shape[1], bn), lambda ni, mi: (layer, 0, ni), pipeline_mode=once)

    return pl.pallas_call(
        _merge_kernel,
        grid=(n // bn, t // bm),
        in_specs=[pl.BlockSpec((bm, d), lambda ni, mi: (mi, 0)),
                  pl.BlockSpec((bm, y_pool.shape[1]), lambda ni, mi: (mi, 0)),
                  pl.BlockSpec((bm, y_attn.shape[1]), lambda ni, mi: (mi, 0)),
                  w_spec(w1), w_spec(w2), w_spec(w3),
                  gate_spec(0), gate_spec(1), gate_spec(2)],
        out_specs=pl.BlockSpec((bm, bn), lambda ni, mi: (mi, ni)),
        out_shape=jax.ShapeDtypeStruct((t, n), BF16),
        scratch_shapes=[pltpu.VMEM((w1.shape[1], bn), BF16), pltpu.VMEM((w2.shape[1], bn), BF16),
                        pltpu.VMEM((w3.shape[1], bn), BF16)],
        compiler_params=_params("arbitrary", "arbitrary", vmem=VMEM_LIMIT_HIGH),
        name="gated_merge",
    )(y_ssd, y_pool, y_attn, w1, w2, w3, gates2, gates2, gates2)


def _tail_kernel(*refs, layer, emit_h):
    if emit_h:
        (mg_ref, x_ref, p_ref, wo_hbm, wg_hbm, wp_ref, gple_ref, gnext_ref,
         xo_ref, ho_ref, wob, wgb, wpb, stage, sem) = refs
    else:
        (mg_ref, x_ref, p_ref, wo_hbm, wg_hbm, wp_ref, gple_ref,
         xo_ref, wob, wgb, wpb, stage, sem) = refs

    @pl.when(pl.program_id(0) == 0)
    def _():
        _stage_resident(lambda r0: wo_hbm.at[layer, pl.ds(r0, STAGE_ROWS), :], wob, stage, sem)
        _stage_resident(lambda r0: wg_hbm.at[layer, pl.ds(r0, STAGE_ROWS), :], wgb, stage, sem)
        wpb[...] = wp_ref[...].astype(BF16)

    def rms(v, g_ref):
        ms = jnp.mean(v * v, axis=-1, keepdims=True)
        return ((v * lax.rsqrt(ms + RMS_EPS)) * g_ref[...]).astype(BF16)

    x1 = x_ref[...] + _dot(mg_ref[...], wob[...])
    gate = _sigmoid(_dot(rms(x1, gple_ref), wgb[...]))
    x2 = x1 + gate * _dot(p_ref[...].astype(BF16), wpb[...])
    xo_ref[...] = x2
    if emit_h:
        ho_ref[...] = rms(x2, gnext_ref)


def _layer_tail(merged, x2d, p, w_out, w_gate, w_proj, layer, ple_g, next_g, bm=512):
    t, d = x2d.shape
    kp = p.shape[-1]
    emit_h = next_g is not None
    once = pl.Buffered(1)
    row = lambda i: (i, 0)
    in_specs = [pl.BlockSpec((bm, d), row), pl.BlockSpec((bm, d), row),
                pl.BlockSpec((None, bm, kp), lambda i: (layer, i, 0)),
                pl.BlockSpec(memory_space=pl.ANY), pl.BlockSpec(memory_space=pl.ANY),
                pl.BlockSpec((None, kp, d), lambda i: (layer, 0, 0), pipeline_mode=once),
                pl.BlockSpec((1, d), lambda i: (0, 0))]
    args = [merged, x2d, p, w_out, w_gate, w_proj, ple_g.reshape(1, d)]
    out_specs = [pl.BlockSpec((bm, d), row)]
    out_shape = [jax.ShapeDtypeStruct((t, d), F32)]
    if emit_h:
        in_specs.append(pl.BlockSpec((1, d), lambda i: (0, 0)))
        args.append(next_g.reshape(1, d))
        out_specs.append(pl.BlockSpec((bm, d), row))
        out_shape.append(jax.ShapeDtypeStruct((t, d), BF16))
    outs = pl.pallas_call(
        functools.partial(_tail_kernel, layer=layer, emit_h=emit_h),
        grid=(t // bm,),
        in_specs=in_specs,
        out_specs=out_specs,
        out_shape=out_shape,
        scratch_shapes=_resident_scratch(d, d) + _resident_scratch(d, d) + _resident_scratch(kp, d)
                       + _stage_buffers(d),
        compiler_params=_params("arbitrary", vmem=VMEM_LIMIT_HIGH),
        name="layer_tail",
    )(*args)
    return (outs[0], outs[1]) if emit_h else (outs[0], None)


def _in_proj_columns(d_model):
    inner = SSD_HEADS * SSD_HEAD_DIM
    bc = SSD_GROUPS * SSD_STATE
    attn_w = len(ATTN_PATTERNS) * ATTN_HEADS_PER_GROUP * ATTN_HEAD_DIM
    attn_out = ATTN_HEADS_PER_GROUP * ATTN_HEAD_DIM
    sizes = [("z_ssd", inner), ("x", inner), ("B", bc), ("C", bc), ("dt", SSD_HEADS),
             ("u_pool", d_model), ("z_pool", d_model), ("q", attn_w), ("k", attn_w), ("v", attn_w),
             ("z_attn", attn_out), ("gates", 3 * d_model)]
    col, start = {}, 0
    for name, size in sizes:
        col[name] = start
        start += size
    col["end"] = start
    return col


def kernel(x, p, norm_g, w_in, conv_w, conv_b, dt_bias, a_log, d_skip, ssd_norm_g, w_br_ssd, pool_w,
           pool_scale, w_br_pool, q_norm_g, k_norm_g, w_br_attn, w_out, ple_norm_g, w_ple_gate,
           w_ple_proj):
    b, seq, d = x.shape
    depth = w_in.shape[0]
    t = b * seq
    col = _in_proj_columns(d)
    assert col["end"] == w_in.shape[-1]
    x2 = x.reshape(t, d)
    p2 = p.reshape(depth, t, -1)
    w_in_t = jnp.swapaxes(w_in, 1, 2)
    h = _rms_norm_bf16(x2, norm_g[0])
    for i in range(depth):
        q3 =_head_proj(h, w_in_t, i, col["q"], b, seq, q_norm_g[i], "q_proj")
        k3 = _head_proj(h, w_in_t, i, col["k"], b, seq, k_norm_g[i], "k_proj")
        v3 = _head_proj(h, w_in_t, i, col["v"], b, seq, None, "v_proj")
        z_attn = _in_proj(h, w_in_t, i, col["z_attn"], col["gates"] - col["z_attn"], 1024, 512, "in_proj_zattn")
        gates = _in_proj(h, w_in_t, i, col["gates"], col["end"] - col["gates"], 1024, 1024, "in_proj_gates")

        assert col["dt"] - col["z_ssd"] == 2 * SSD_HEADS * SSD_HEAD_DIM + 2 * SSD_GROUPS * SSD_STATE
        y_ssd = _ssd_branch(h, w_in_t, i, col["z_ssd"], seq, conv_w[i], conv_b[i], dt_bias[i],
                            a_log[i], d_skip[i], ssd_norm_g[i])
        y_pool = _pool_branch(h, w_in_t, i, col["u_pool"], col["z_pool"], seq, pool_w, pool_scale[i])
        y_attn = _attn_branch(q3, k3, v3, z_attn.reshape(b, seq, -1))

        merged = _merge(y_ssd.reshape(t, -1), y_pool.reshape(t, -1), y_attn.reshape(t, -1),
                        w_br_ssd, w_br_pool, w_br_attn, i, gates)
        next_g = norm_g[i + 1] if i + 1 < depth else None
        x2, h = _layer_tail(merged, x2, p2, w_out, w_ple_gate, w_ple_proj, i, ple_norm_g[i], next_g)
    return x2.reshape(b, seq, d)
```

```python
import functools

import jax
import jax.numpy as jnp
from jax import lax
from jax.experimental import pallas as pl
from jax.experimental.pallas import tpu as pltpu

F32 = jnp.float32
BF16 = jnp.bfloat16

RMS_EPS = 1e-6
LOG2E = 1.4426950408889634
SSD_HEAD_DIM = 64
SSD_HEADS = 32
SSD_GROUPS = 4
SSD_STATE = 128
SSD_CHUNK = 256
CONV_K = 4
POOL_WINDOWS = (2, 4, 8, 16)
POOL_HALO = 16
ATTN_PATTERNS = ((128, 1), (512, 4), (2048, 16))
ATTN_HEAD_DIM = 128
ATTN_HEADS_PER_GROUP = 4
ATTN_BLOCK = 128

LANE = 128
VMEM_LIMIT = 56 * 1024 * 1024
VMEM_LIMIT_HIGH = 58 * 1024 * 1024
STAGE_ROWS = 256


def _params(*sem, vmem=VMEM_LIMIT):
    return pltpu.CompilerParams(dimension_semantics=sem, vmem_limit_bytes=vmem)


def _sigmoid(v):
    return 0.5 * jnp.tanh(0.5 * v) + 0.5


def _silu(v):
    half = 0.5 * v
    return half * jnp.tanh(half) + half


def _dot(a, b):
    return jnp.dot(a, b, preferred_element_type=F32)


def _dot_nt(a, b):
    return lax.dot_general(a, b, (((1,), (1,)), ((), ())), preferred_element_type=F32)


def _dot_tn(a, b):
    return lax.dot_general(a, b, (((0,), (0,)), ((), ())), preferred_element_type=F32)


def _pack_pieces(v, width, n_pieces):
    packed = None
    rest = v
    for k in range(n_pieces):
        piece = rest.astype(BF16).astype(F32)
        rest = rest - piece
        shifted = piece if k == 0 else pltpu.roll(piece, k * width, axis=1)
        packed = shifted if packed is None else packed + shifted
    return packed


def _stage_weight(w_ref, dst):
    rows_total = dst.shape[0]
    step = min(STAGE_ROWS, rows_total)

    def body(c, carry):
        rows = pl.ds(pl.multiple_of(c * step, step), step)
        dst[rows, :] = w_ref[rows, :].astype(BF16)
        return carry

    lax.fori_loop(0, rows_total // step, body, 0)


def _fetch_weight_rows(wt_hbm, wf, wb, sem, layer, row0):
    n = pl.program_id(0)
    n_tiles = pl.num_programs(0)
    bn = wb.shape[0]

    def copy(tile, slot):
        rows = pl.ds(pl.multiple_of(row0 + tile * bn, 8), bn)
        return pltpu.make_async_copy(wt_hbm.at[layer, rows, :], wf.at[slot], sem.at[slot])

    @pl.when(pl.program_id(1) == 0)
    def _():
        slot = n % 2

        @pl.when(n == 0)
        def _():
            copy(0, 0).start()

        copy(n, slot).wait()

        @pl.when(n + 1 < n_tiles)
        def _():
            copy(n + 1, 1 - slot).start()

        _stage_weight(wf.at[slot], wb)


def _weight_scratch(bn, k):
    return [pltpu.VMEM((2, bn, k), F32), pltpu.VMEM((bn, k), BF16), pltpu.SemaphoreType.DMA((2,))]


def _stage_resident(hbm_rows, wb, stage, sem, transpose=False):
    chunk = stage.shape[1]
    n_chunks = (wb.shape[1] if transpose else wb.shape[0]) // chunk

    def copy(c):
        return pltpu.make_async_copy(hbm_rows(c * chunk), stage.at[c % 2], sem.at[c % 2])

    copy(0).start()
    for c in range(n_chunks):
        if c + 1 < n_chunks:
            copy(c + 1).start()
        copy(c).wait()
        if transpose:
            wb[:, c * chunk:(c + 1) * chunk] = stage[c % 2].T.astype(BF16)
        else:
            wb[c * chunk:(c + 1) * chunk, :] = stage[c % 2].astype(BF16)


def _resident_scratch(rows, cols):
    return [pltpu.VMEM((rows, cols), BF16)]


def _stage_buffers(cols, rows=STAGE_ROWS):
    return [pltpu.VMEM((2, rows, cols), F32), pltpu.SemaphoreType.DMA((2,))]


def _rms_kernel(x_ref, g_ref, o_ref):
    x = x_ref[...]
    ms = jnp.mean(x * x, axis=-1, keepdims=True)
    o_ref[...] = ((x * lax.rsqrt(ms + RMS_EPS)) * g_ref[...]).astype(o_ref.dtype)


def _rms_norm_bf16(x2d, g, bm=512):
    t, d = x2d.shape
    return pl.pallas_call(
        _rms_kernel,
        grid=(t // bm,),
        in_specs=[pl.BlockSpec((bm, d), lambda i: (i, 0)),
                  pl.BlockSpec((1, d), lambda i: (0, 0))],
        out_specs=pl.BlockSpec((bm, d), lambda i: (i, 0)),
        out_shape=jax.ShapeDtypeStruct((t, d), BF16),
        compiler_params=_params("parallel"),
        name="rms_norm",
    )(x2d, g.reshape(1, d))


def _proj_kernel(a_ref, wt_hbm, o_ref, wf, wb, sem, *, layer, row0):
    _fetch_weight_rows(wt_hbm, wf, wb, sem, layer, row0)
    o_ref[...] = _dot_nt(a_ref[...], wb[...]).astype(o_ref.dtype)


def _in_proj(h, w_in_t, layer, col0, width, bm, bn, name):
    t, k = h.shape
    assert col0 % 8 == 0 and width % bn == 0
    return pl.pallas_call(
        functools.partial(_proj_kernel, layer=layer, row0=col0),
        grid=(width // bn, t // bm),
        in_specs=[pl.BlockSpec((bm, k), lambda n, m: (m, 0)),
                  pl.BlockSpec(memory_space=pl.ANY)],
        out_specs=pl.BlockSpec((bm, bn), lambda n, m: (m, n)),
        out_shape=jax.ShapeDtypeStruct((t, width), F32),
        scratch_shapes=_weight_scratch(bn, k),
        compiler_params=_params("arbitrary", "arbitrary"),
        name=name,
    )(h, w_in_t)


def _head_proj_kernel(*refs, layer, row0, seq, normalize, extra_row0):
    refs = list(refs)
    a_ref, wt_hbm = refs.pop(0), refs.pop(0)
    g_ref = refs.pop(0) if normalize else None
    o_ref = refs.pop(0)
    xo_ref = refs.pop(0) if extra_row0 is not None else None
    wb, stage, sem, acc_scr = refs
    m = pl.program_id(0)
    bm = a_ref.shape[0]
    hd = ATTN_HEAD_DIM
    gw = ATTN_HEADS_PER_GROUP * hd
    half = m % (seq // bm)

    width = len(ATTN_PATTERNS) * gw

    @pl.when(m == 0)
    def _():
        def hbm_rows(r0):
            src = row0 + r0 if r0 < width else extra_row0 + (r0 - width)
            return wt_hbm.at[layer, pl.ds(src, STAGE_ROWS), :]

        _stage_resident(hbm_rows, wb, stage, sem)

    a = a_ref[...]
    if xo_ref is not None:
        xo_ref[...] = _dot_nt(a, wb[width:, :])
    order = sorted(range(len(ATTN_PATTERNS)), key=lambda gi: -ATTN_PATTERNS[gi][1])
    slab = 0
    for g in order:
        dil = ATTN_PATTERNS[g][1]
        acc = _dot_nt(a, wb[g * gw:(g + 1) * gw, :])
        cnt = bm // dil
        ld = seq // dil
        for hh in range(ATTN_HEADS_PER_GROUP):
            cs = slice(g * gw + hh * hd, g * gw + (hh + 1) * hd)
            v = acc[:, hh * hd:(hh + 1) * hd]
            if normalize:
                ms = jnp.mean(v * v, axis=-1, keepdims=True)
                v = (v * lax.rsqrt(ms + RMS_EPS)) * g_ref[...]
            if dil == 1:
                o_ref[pl.ds(pl.multiple_of(half * bm, bm), bm), cs] = v.astype(BF16)
            else:
                acc_scr[slab] = v
                for r in range(dil):
                    rows = acc_scr[slab, pl.ds(r, cnt, stride=dil), :]
                    dst = r * ld + pl.multiple_of(half * cnt, cnt)
                    o_ref[pl.ds(dst, cnt), cs] = rows.astype(BF16)
                slab += 1


def _head_proj(h, w_in_t, layer, col0, b, seq, norm_g, name, extra=None, bm=1024):
    t, k = h.shape
    hd = ATTN_HEAD_DIM
    width = len(ATTN_PATTERNS) * ATTN_HEADS_PER_GROUP * hd
    n_dilated = sum(ATTN_HEADS_PER_GROUP for _, dil in ATTN_PATTERNS if dil > 1)
    tiles_per_seq = seq // bm
    assert col0 % 8 == 0 and width % STAGE_ROWS == 0
    normalize = norm_g is not None
    in_specs = [pl.BlockSpec((bm, k), lambda m: (m, 0)), pl.BlockSpec(memory_space=pl.ANY)]
    args = [h, w_in_t]
    if normalize:
        in_specs.append(pl.BlockSpec((1, hd), lambda m: (0, 0)))
        args.append(norm_g.reshape(1, hd))
    out_specs = [pl.BlockSpec((None, seq, width), lambda m: (m // tiles_per_seq, 0, 0))]
    out_shape = [jax.ShapeDtypeStruct((b, seq, width), BF16)]
    extra_col, extra_w = extra if extra is not None else (None, 0)
    if extra is not None:
        assert extra_col % 8 == 0 and extra_w % STAGE_ROWS == 0
        out_specs.append(pl.BlockSpec((bm, extra_w), lambda m: (m, 0)))
        out_shape.append(jax.ShapeDtypeStruct((t, extra_w), F32))
    outs = pl.pallas_call(
        functools.partial(_head_proj_kernel, layer=layer, row0=col0, seq=seq, normalize=normalize,
                          extra_row0=extra_col),
        grid=(t // bm,),
        in_specs=in_specs,
        out_specs=out_specs,
        out_shape=out_shape,
        scratch_shapes=_resident_scratch(width + extra_w, k) + _stage_buffers(k)
                       + [pltpu.VMEM((n_dilated, bm, hd), F32)],
        compiler_params=_params("arbitrary"),
        name=name,
    )(*args)
    return outs if extra is not None else outs[0]


def _ssd_kernel(h0_ref, hn_ref, wt_hbm, cw_ref, cb_ref, dtb_ref, alog_ref, dsk_ref, ng_ref, e_ref,
                o_ref, wb, stage, sem, proj, z_scr, xpad, xs_scr, bb_scr, cc_scr, s_scr, y_scr,
                *, layer, row0, chunks_per_seq):
    q = SSD_CHUNK
    inner = xs_scr.shape[-1]
    bc = bb_scr.shape[-1]
    gw = inner // SSD_GROUPS
    n_state = SSD_STATE
    halo = 8
    step = pl.program_id(0)

    @pl.when(step == 0)
    def _():
        chunk = stage.shape[1]
        _stage_resident(lambda r0: wt_hbm.at[layer, pl.ds(row0 + r0, chunk), :], wb, stage, sem,
                        transpose=True)
        proj[...] = _dot(h0_ref[...], wb[...])

    @pl.when(step % chunks_per_seq == 0)
    def _():
        xpad[:, 0:halo, :] = jnp.zeros((xpad.shape[0], halo, LANE), F32)
        s_scr[...] = jnp.zeros(s_scr.shape, F32)

    x_slabs = inner // LANE
    bc_slabs = bc // LANE
    for j in range(xpad.shape[0]):
        cs = slice(j * LANE, (j + 1) * LANE)
        xpad[j, halo:halo + q, :] = proj[:, inner + j * LANE:inner + (j + 1) * LANE]
        acc = cb_ref[:, cs] + cw_ref[0:1, cs] * xpad[j, halo - CONV_K + 1:halo - CONV_K + 1 + q, :]
        for k in range(1, CONV_K):
            r0 = halo - CONV_K + 1 + k
            acc = acc + cw_ref[k:k + 1, cs] * xpad[j, r0:r0 + q, :]
        xc = _silu(acc)
        if j < x_slabs:
            xs_scr[:, cs] = xc
        elif j < x_slabs + bc_slabs:
            bb_scr[:, (j - x_slabs) * LANE:(j - x_slabs + 1) * LANE] = xc.astype(BF16)
        else:
            jc = j - x_slabs - bc_slabs
            cc_scr[:, jc * LANE:(jc + 1) * LANE] = xc.astype(BF16)
        xpad[j, 0:halo, :] = xpad[j, q:q + halo, :]
    z_scr[...] = proj[:, 0:inner]

    dt_lanes = lax.broadcasted_iota(jnp.int32, (q, LANE), 1) < SSD_HEADS
    dt_col = 2 * inner + 2 * bc
    dt_raw = jnp.where(dt_lanes, proj[:, dt_col:dt_col + LANE], 0.0)

    h_next = hn_ref[...]
    piece = 2 * LANE
    pieces = [(c0, min(c0 + piece, proj.shape[1])) for c0 in range(0, proj.shape[1], piece)]

    def project_next(count):
        for _ in range(min(count, len(pieces))):
            c0, c1 = pieces.pop(0)
            proj[:, c0:c1] = _dot(h_next, wb[:, c0:c1])

    dtr = dt_raw + dtb_ref[...]
    dtv = jnp.where(dt_lanes, jnp.maximum(dtr, 0.0) + jnp.log1p(jnp.exp(-jnp.abs(dtr))), 0.0)
    da = dtv * (-jnp.exp(alog_ref[...]))
    ri = lax.broadcasted_iota(jnp.int32, (q, q), 0)
    ci = lax.broadcasted_iota(jnp.int32, (q, q), 1)
    tri = jnp.where(ri >= ci, 1.0, 0.0).astype(BF16)
    tri_t = jnp.where(ri <= ci, 1.0, 0.0).astype(BF16)
    nh = SSD_HEADS
    da3 = _pack_pieces(da, nh, 3)
    cum3 = _dot(tri, da3.astype(BF16))
    acum = jnp.where(dt_lanes, cum3 + pltpu.roll(cum3, LANE - nh, axis=1)
                     + pltpu.roll(cum3, LANE - 2 * nh, axis=1), 0.0)
    acum2 = acum * LOG2E
    cum3_row = _dot(da3.T.astype(BF16), tri_t)
    acum2_row = (cum3_row[0:nh] + cum3_row[nh:2 * nh] + cum3_row[2 * nh:3 * nh]) * LOG2E
    acum_p = _pack_pieces(acum, nh, 3).astype(BF16)
    dtv_p = _pack_pieces(dtv, nh, 2).astype(BF16)
    hq = q // 2
    causal_h = (lax.broadcasted_iota(jnp.int32, (hq, hq), 0)
                >= lax.broadcasted_iota(jnp.int32, (hq, hq), 1))

    first_lane_half = lax.broadcasted_iota(jnp.int32, (q, LANE), 1) < SSD_HEAD_DIM
    heads_per_group = SSD_HEADS // SSD_GROUPS
    project_next(1)

    def prepare(g):
        project_next(1)
        gs = slice(g * gw, (g + 1) * gw)
        e_g = e_ref[:, gs]
        bg = bb_scr[:, g * n_state:(g + 1) * n_state]
        cg = cc_scr[:, g * n_state:(g + 1) * n_state]
        s_old = s_scr[g]
        acum_e = _dot(acum_p, e_g)
        dt_e = _dot(dtv_p, e_g)
        xh = xs_scr[:, gs] * dt_e
        xh_b = xh.astype(BF16)
        last_e = acum_e[q - 1:q, :]
        xdec = (xh * jnp.exp(last_e - acum_e)).astype(BF16)
        y_off = _dot(cg, s_old.astype(BF16)) * jnp.exp(acum_e)
        s_scr[g] = jnp.exp(last_e) * s_old + _dot_tn(bg, xdec)
        cb = _dot_nt(cg, bg)
        cb00, cb10, cb11 = cb[:hq, :hq], cb[hq:, :hq], cb[hq:, hq:]
        tops, bots = [], []
        for hl in range(heads_per_group):
            h = g * heads_per_group + hl
            col = acum2[:, h:h + 1]
            row = acum2_row[h:h + 1, :]
            l00 = jnp.exp2(jnp.where(causal_h, col[:hq] - row[:, :hq], -jnp.inf))
            l10 = jnp.exp2(col[hq:] - row[:, :hq])
            l11 = jnp.exp2(jnp.where(causal_h, col[hq:] - row[:, hq:], -jnp.inf))
            tops.append(jnp.concatenate([cb00 * l00, jnp.zeros((hq, hq), F32)], axis=1).astype(BF16))
            bots.append(jnp.concatenate([cb10 * l10, cb11 * l11], axis=1).astype(BF16))
            if hl % 2:
                project_next(1)
        return xh_b, y_off, tops, bots

    def finish(g, prepared):
        xh_b, y_off, tops, bots = prepared
        for hh in range(heads_per_group // 2):
            xp = xh_b[:, hh * LANE:(hh + 1) * LANE]
            stacked = jnp.concatenate([tops[2 * hh], bots[2 * hh], tops[2 * hh + 1], bots[2 * hh + 1]],
                                      axis=0)
            both = _dot(stacked, xp)
            y_pair = jnp.where(first_lane_half, both[:q], both[q:])
            c0 = g * gw + hh * LANE
            y_scr[:, c0:c0 + LANE] = y_pair + y_off[:, hh * LANE:(hh + 1) * LANE]

    for g in range(SSD_GROUPS):
        finish(g, prepare(g))
    project_next(len(pieces))

    rb = 32
    for r0 in range(0, q, rb):
        rows = slice(r0, r0 + rb)
        y = y_scr[rows, :] + dsk_ref[...] * xs_scr[rows, :]
        gated = y * _silu(z_scr[rows, :])
        ms = jnp.mean(gated * gated, axis=-1, keepdims=True)
        o_ref[rows, :] = ((gated * lax.rsqrt(ms + RMS_EPS)) * ng_ref[...]).astype(o_ref.dtype)


def _ssd_branch(h, w_in_t, layer, col0, seq, conv_w, conv_b, dt_bias, a_log, d_skip, norm_g):
    t, d = h.shape
    q = SSD_CHUNK
    inner = SSD_HEADS * SSD_HEAD_DIM
    bc = SSD_GROUPS * SSD_STATE
    conv_ch = inner + 2 * bc
    width = 2 * inner + 2 * bc + LANE
    pad = LANE - SSD_HEADS
    n_steps = t // q
    assert col0 % 8 == 0 and seq % q == 0
    piece_rows = jnp.arange(LANE)[:, None]
    expand = ((jnp.arange(inner)[None, :] // SSD_HEAD_DIM == piece_rows % SSD_HEADS)
              & (piece_rows < 3 * SSD_HEADS)).astype(BF16)
    row = lambda v: v.reshape(1, -1)
    const = lambda shape: pl.BlockSpec(shape, lambda s: (0,) * len(shape))
    return pl.pallas_call(
        functools.partial(_ssd_kernel, layer=layer, row0=col0, chunks_per_seq=seq // q),
        grid=(n_steps,),
        in_specs=[
            pl.BlockSpec((q, d), lambda s: (0, 0)),
            pl.BlockSpec((q, d), lambda s: (jnp.minimum(s + 1, n_steps - 1), 0)),
            pl.BlockSpec(memory_space=pl.ANY),
            const((CONV_K, conv_ch)), const((1, conv_ch)), const((1, LANE)), const((1, LANE)),
            const((1, inner)), const((1, inner)), const((LANE, inner)),
        ],
        out_specs=pl.BlockSpec((q, inner), lambda s: (s, 0)),
        out_shape=jax.ShapeDtypeStruct((t, inner), BF16),
        scratch_shapes=_resident_scratch(d, width) + _stage_buffers(d, LANE) + [
            pltpu.VMEM((q, width), F32),
            pltpu.VMEM((q, inner), F32),
            pltpu.VMEM((conv_ch // LANE, q + 8, LANE), F32),
            pltpu.VMEM((q, inner), F32),
            pltpu.VMEM((q, bc), BF16),
            pltpu.VMEM((q, bc), BF16),
            pltpu.VMEM((SSD_GROUPS, SSD_STATE, inner // SSD_GROUPS), F32),
            pltpu.VMEM((q, inner), F32),
        ],
        compiler_params=_params("arbitrary"),
        name="ssd_branch",
    )(h, h, w_in_t, conv_w, row(conv_b),
      row(jnp.pad(dt_bias, (0, pad))), row(jnp.pad(a_log, (0, pad))),
      row(jnp.repeat(d_skip, SSD_HEAD_DIM)), row(norm_g), expand)


def _pool_kernel(a_ref, wt_hbm, pw_ref, sc_ref, o_ref, wb, stage, sem, pwb, tail,
                 *, layer, u_row0, z_row0, seq):
    m = pl.program_id(0)
    bm = a_ref.shape[0]
    ng, gw, _ = pwb.shape
    tile_in_seq = m % (seq // bm)

    @pl.when(m == 0)
    def _():
        def hbm_rows(r0):
            g, part, off = r0 // (2 * gw), (r0 // gw) % 2, r0 % gw
            src = (z_row0 if part else u_row0) + g * gw + off
            return wt_hbm.at[layer, pl.ds(src, STAGE_ROWS), :]

        _stage_resident(hbm_rows, wb, stage, sem)
        for g in range(ng):
            pwb[g] = pw_ref[g].astype(BF16)

    a = a_ref[...]
    pos = tile_in_seq * bm + lax.broadcasted_iota(jnp.int32, (bm, gw), 0)
    for g, w in enumerate(POOL_WINDOWS):
        cs = slice(g * gw, (g + 1) * gw)
        acc = _dot_nt(a, wb[2 * g * gw:2 * (g + 1) * gw, :])
        u = acc[:, :gw]
        hist = jnp.where(tile_in_seq == 0, 0.0, tail[:, cs])
        s = jnp.concatenate([hist, u], axis=0)
        k = 1
        while k < w:
            s = s + pltpu.roll(s, k, axis=0)
            k *= 2
        count = jnp.minimum(pos + 1, w).astype(F32)
        pooled = s[POOL_HALO:, :] / count - u
        mixed = _dot(pooled.astype(BF16), pwb[g])
        o_ref[:, cs] = (mixed * sc_ref[:, cs] * _silu(acc[:, gw:])).astype(o_ref.dtype)
        tail[:, cs] = u[bm - POOL_HALO:, :]


def _pool_branch(h, w_in_t, layer, u_col, z_col, seq, pool_w, pool_scale, bm=1024):
    t, k = h.shape
    _, ng, gw, _ = pool_w.shape
    width = ng * gw
    assert u_col % 8 == 0 and z_col % 8 == 0 and gw % STAGE_ROWS == 0 and seq % bm == 0
    return pl.pallas_call(
        functools.partial(_pool_kernel, layer=layer, u_row0=u_col, z_row0=z_col, seq=seq),
        grid=(t // bm,),
        in_specs=[pl.BlockSpec((bm, k), lambda m: (m, 0)),
                  pl.BlockSpec(memory_space=pl.ANY),
                  pl.BlockSpec((None, ng, gw, gw), lambda m: (layer, 0, 0, 0),
                               pipeline_mode=pl.Buffered(1)),
                  pl.BlockSpec((1, width), lambda m: (0, 0))],
        out_specs=pl.BlockSpec((bm, width), lambda m: (m, 0)),
        out_shape=jax.ShapeDtypeStruct((t, width), BF16),
        scratch_shapes=_resident_scratch(2 * width, k) + _stage_buffers(k)
                       + [pltpu.VMEM((ng, gw, gw), BF16), pltpu.VMEM((POOL_HALO, width), F32)],
        compiler_params=_params("arbitrary"),
        name="pool_branch",
    )(h, w_in_t, pool_w, pool_scale.reshape(1, width))


def _attn_kernel(q0, q1, q2, k0, k1, k2, v0, v1, v2, z_ref, o_ref, o0, o1, o2, l0, l1, l2):
    seq, hd = q0.shape
    blk = ATTN_BLOCK
    scale2 = hd ** -0.5 * LOG2E
    ln2 = 1.0 / LOG2E
    qs, ks, vs = (q0, q1, q2), (k0, k1, k2), (v0, v1, v2)
    o_scrs, l_scrs = (o0, o1, o2), (l0, l1, l2)
    qi = lax.broadcasted_iota(jnp.int32, (blk, 2 * blk), 0)
    ci = lax.broadcasted_iota(jnp.int32, (blk, 2 * blk), 1)
    band = (ci >= qi) & (ci <= qi + blk)
    diag = (lax.broadcasted_iota(jnp.int32, (blk, blk), 1)
            <= lax.broadcasted_iota(jnp.int32, (blk, blk), 0))

    n_blocks = seq // blk

    def key_rows(n, nbs):
        first = n % nbs == 0
        return (slice(n * blk, (n + 1) * blk), diag) if first else \
               (slice((n - 1) * blk, (n + 1) * blk), band)

    dils = [dil for _, dil in ATTN_PATTERNS]
    nbss = [seq // dil // blk for dil in dils]
    scores, probs, lses = {}, {}, {}

    def score_phase(g):
        scores[g] = [_dot_nt(qs[g][n * blk:(n + 1) * blk, :], ks[g][key_rows(n, nbss[g])[0], :])
                     for n in range(n_blocks)]

    def softmax_phase(g):
        probs[g], lses[g] = [], []
        for n in range(n_blocks):
            s2 = jnp.where(key_rows(n, nbss[g])[1], scores[g][n] * scale2, -jnp.inf)
            m2 = jnp.max(s2, axis=-1, keepdims=True)
            e = jnp.exp2(s2 - m2)
            den = jnp.sum(e, axis=-1, keepdims=True)
            probs[g].append((e / den).astype(BF16))
            lses[g].append(m2 * ln2 + jnp.log(den))

    def value_phase(g):
        dil, nbs = dils[g], nbss[g]
        for n in range(n_blocks):
            tok = (pl.ds((n % nbs) * blk * dil + n // nbs, blk, stride=dil) if dil > 1
                   else slice(n * blk, (n + 1) * blk))
            o_scrs[g][tok, :] = _dot(probs[g][n], vs[g][key_rows(n, nbs)[0], :])
            l_scrs[g][tok, :] = jnp.broadcast_to(lses[g][n], (blk, hd))

    for g in range(len(ATTN_PATTERNS)):
        score_phase(g)
        softmax_phase(g)
        value_phase(g)

    rb = 64
    for r0 in range(0, seq, rb):
        rows = slice(r0, r0 + rb)
        la, lb, lc = l0[rows, :], l1[rows, :], l2[rows, :]
        mx = jnp.maximum(jnp.maximum(la, lb), lc)
        wa, wb, wc = jnp.exp(la - mx), jnp.exp(lb - mx), jnp.exp(lc - mx)
        tot = wa + wb + wc
        comb = (wa / tot) * o0[rows, :] + (wb / tot) * o1[rows, :] + (wc / tot) * o2[rows, :]
        o_ref[rows, :] = (comb * _silu(z_ref[rows, :])).astype(o_ref.dtype)


def _attn_branch(q3, k3, v3, z3):
    b, seq, _ = q3.shape
    hd = ATTN_HEAD_DIM
    hpg = ATTN_HEADS_PER_GROUP
    ngroups = len(ATTN_PATTERNS)

    def head_spec(g):
        return pl.BlockSpec((None, seq, hd), lambda bi, j: (bi, 0, g * hpg + j))

    in_specs = [head_spec(g) for _ in range(3) for g in range(ngroups)]
    in_specs.append(pl.BlockSpec((None, seq, hd), lambda bi, j: (bi, 0, j)))
    scratch = [pltpu.VMEM((seq, hd), F32)] * (2 * ngroups)
    return pl.pallas_call(
        _attn_kernel,
        grid=(b, hpg),
        in_specs=in_specs,
        out_specs=pl.BlockSpec((None, seq, hd), lambda bi, j: (bi, 0, j)),
        out_shape=jax.ShapeDtypeStruct((b, seq, hpg * hd), BF16),
        scratch_shapes=scratch,
        compiler_params=_params("parallel", "parallel"),
        name="attn_branch",
    )(*([q3] * ngroups + [k3] * ngroups + [v3] * ngroups), z3)


def _merge_kernel(ys_ref, yp_ref, ya_ref, w1_ref, w2_ref, w3_ref, g0_ref, g1_ref, g2_ref, o_ref,
                  wb1, wb2, wb3):
    @pl.when(pl.program_id(1) == 0)
    def _():
        _stage_weight(w1_ref, wb1)
        _stage_weight(w2_ref, wb2)
        _stage_weight(w3_ref, wb3)

    d1 = _dot(ys_ref[...], wb1[...])
    d2 = _dot(yp_ref[...], wb2[...])
    d3 = _dot(ya_ref[...], wb3[...])
    m = _sigmoid(g0_ref[...]) * d1
    m = m + _sigmoid(g1_ref[...]) * d2
    m = m + _sigmoid(g2_ref[...]) * d3
    o_ref[...] = m.astype(o_ref.dtype)


def _merge(y_ssd, y_pool, y_attn, w1, w2, w3, layer, gates2, bm=512, bn=1024):
    t, d = y_ssd.shape
    n = w1.shape[-1]
    gate_blocks = n // bn
    once = pl.Buffered(1)

    def gate_spec(i):
        return pl.BlockSpec((bm, bn), lambda ni, mi: (mi, i * gate_blocks + ni))

    def w_spec(w):
        return pl.BlockSpec((None, w.shape[1], bn), lambda ni, mi: (layer, 0, ni), pipeline_mode=once)

    return pl.pallas_call(
        _merge_kernel,
        grid=(n // bn, t // bm),
        in_specs=[pl.BlockSpec((bm, d), lambda ni, mi: (mi, 0)),
                  pl.BlockSpec((bm, y_pool.shape[1]), lambda ni, mi: (mi, 0)),
                  pl.BlockSpec((bm, y_attn.shape[1]), lambda ni, mi: (mi, 0)),
                  w_spec(w1), w_spec(w2), w_spec(w3),
                  gate_spec(0), gate_spec(1), gate_spec(2)],
        out_specs=pl.BlockSpec((bm, bn), lambda ni, mi: (mi, ni)),
        out_shape=jax.ShapeDtypeStruct((t, n), BF16),
        scratch_shapes=[pltpu.VMEM((w1.shape[1], bn), BF16), pltpu.VMEM((w2.shape[1], bn), BF16),
                        pltpu.VMEM((w3.shape[1], bn), BF16)],
        compiler_params=_params("arbitrary", "arbitrary", vmem=VMEM_LIMIT_HIGH),
        name="gated_merge",
    )(y_ssd, y_pool, y_attn, w1, w2, w3, gates2, gates2, gates2)


def _tail_kernel(*refs, layer, emit_h):
    if emit_h:
        (mg_ref, x_ref, p_ref, wo_hbm, wg_hbm, wp_ref, gple_ref, gnext_ref,
         xo_ref, ho_ref, wob, wgb, wpb, stage, sem) = refs
    else:
        (mg_ref, x_ref, p_ref, wo_hbm, wg_hbm, wp_ref, gple_ref,
         xo_ref, wob, wgb, wpb, stage, sem) = refs

    @pl.when(pl.program_id(0) == 0)
    def _():
        _stage_resident(lambda r0: wo_hbm.at[layer, pl.ds(r0, STAGE_ROWS), :], wob, stage, sem)
        _stage_resident(lambda r0: wg_hbm.at[layer, pl.ds(r0, STAGE_ROWS), :], wgb, stage, sem)
        wpb[...] = wp_ref[...].astype(BF16)

    def rms(v, g_ref):
        ms = jnp.mean(v * v, axis=-1, keepdims=True)
        return ((v * lax.rsqrt(ms + RMS_EPS)) * g_ref[...]).astype(BF16)

    x1 = x_ref[...] + _dot(mg_ref[...], wob[...])
    gate = _sigmoid(_dot(rms(x1, gple_ref), wgb[...]))
    x2 = x1 + gate * _dot(p_ref[...].astype(BF16), wpb[...])
    xo_ref[...] = x2
    if emit_h:
        ho_ref[...] = rms(x2, gnext_ref)


def _layer_tail(merged, x2d, p, w_out, w_gate, w_proj, layer, ple_g, next_g, bm=512):
    t, d = x2d.shape
    kp = p.shape[-1]
    emit_h = next_g is not None
    once = pl.Buffered(1)
    row = lambda i: (i, 0)
    in_specs = [pl.BlockSpec((bm, d), row), pl.BlockSpec((bm, d), row),
                pl.BlockSpec((None, bm, kp), lambda i: (layer, i, 0)),
                pl.BlockSpec(memory_space=pl.ANY), pl.BlockSpec(memory_space=pl.ANY),
                pl.BlockSpec((None, kp, d), lambda i: (layer, 0, 0), pipeline_mode=once),
                pl.BlockSpec((1, d), lambda i: (0, 0))]
    args = [merged, x2d, p, w_out, w_gate, w_proj, ple_g.reshape(1, d)]
    out_specs = [pl.BlockSpec((bm, d), row)]
    out_shape = [jax.ShapeDtypeStruct((t, d), F32)]
    if emit_h:
        in_specs.append(pl.BlockSpec((1, d), lambda i: (0, 0)))
        args.append(next_g.reshape(1, d))
        out_specs.append(pl.BlockSpec((bm, d), row))
        out_shape.append(jax.ShapeDtypeStruct((t, d), BF16))
    outs = pl.pallas_call(
        functools.partial(_tail_kernel, layer=layer, emit_h=emit_h),
        grid=(t // bm,),
        in_specs=in_specs,
        out_specs=out_specs,
        out_shape=out_shape,
        scratch_shapes=_resident_scratch(d, d) + _resident_scratch(d, d) + _resident_scratch(kp, d)
                       + _stage_buffers(d),
        compiler_params=_params("arbitrary", vmem=VMEM_LIMIT_HIGH),
        name="layer_tail",
    )(*args)
    return (outs[0], outs[1]) if emit_h else (outs[0], None)


def _in_proj_columns(d_model):
    inner = SSD_HEADS * SSD_HEAD_DIM
    bc = SSD_GROUPS * SSD_STATE
    attn_w = len(ATTN_PATTERNS) * ATTN_HEADS_PER_GROUP * ATTN_HEAD_DIM
    attn_out = ATTN_HEADS_PER_GROUP * ATTN_HEAD_DIM
    sizes = [("z_ssd", inner), ("x", inner), ("B", bc), ("C", bc), ("dt", SSD_HEADS),
             ("u_pool", d_model), ("z_pool", d_model), ("q", attn_w), ("k", attn_w), ("v", attn_w),
             ("z_attn", attn_out), ("gates", 3 * d_model)]
    col, start = {}, 0
    for name, size in sizes:
        col[name] = start
        start += size
    col["end"] = start
    return col


def kernel(x, p, norm_g, w_in, conv_w, conv_b, dt_bias, a_log, d_skip, ssd_norm_g, w_br_ssd, pool_w,
           pool_scale, w_br_pool, q_norm_g, k_norm_g, w_br_attn, w_out, ple_norm_g, w_ple_gate,
           w_ple_proj):
    b, seq, d = x.shape
    depth = w_in.shape[0]
    t = b * seq
    col = _in_proj_columns(d)
    assert col["end"] == w_in.shape[-1]
    x2 = x.reshape(t, d)
    p2 = p.reshape(depth, t, -1)
    w_in_t = jnp.swapaxes(w_in, 1, 2)
    h = _rms_norm_bf16(x2, norm_g[0])
    for i in range(depth):
        q3 =_head_proj(h, w_in_t, i, col["q"], b, seq, q_norm_g[i], "q_proj")
        k3 = _head_proj(h, w_in_t, i, col["k"], b, seq, k_norm_g[i], "k_proj")
        v3, z_attn = _head_proj(h, w_in_t, i, col["v"], b, seq, None, "v_proj",
                                extra=(col["z_attn"], col["gates"] - col["z_attn"]))
        gates = _in_proj(h, w_in_t, i, col["gates"], col["end"] - col["gates"], 1024, 1024, "in_proj_gates")

        assert col["dt"] - col["z_ssd"] == 2 * SSD_HEADS * SSD_HEAD_DIM + 2 * SSD_GROUPS * SSD_STATE
        y_ssd = _ssd_branch(h, w_in_t, i, col["z_ssd"], seq, conv_w[i], conv_b[i], dt_bias[i],
                            a_log[i], d_skip[i], ssd_norm_g[i])
        y_pool = _pool_branch(h, w_in_t, i, col["u_pool"], col["z_pool"], seq, pool_w, pool_scale[i])
        y_attn = _attn_branch(q3, k3, v3, z_attn.reshape(b, seq, -1))

        merged = _merge(y_ssd.reshape(t, -1), y_pool.reshape(t, -1), y_attn.reshape(t, -1),
                        w_br_ssd, w_br_pool, w_br_attn, i, gates)
        next_g = norm_g[i + 1] if i + 1 < depth else None
        x2, h = _layer_tail(merged, x2, p2, w_out, w_ple_gate, w_ple_proj, i, ple_norm_g[i], next_g)
    return x2.reshape(b, seq, d)
```
